```python
import jax, jax.numpy as jnp
from jax import lax
import numpy as np

D_MODEL = 2048
BATCH = 8
SEQ = 2048
DEPTH = 2
DEC_BATCH = 1
DEC_SEQ = 8192
PAST_LEN = 128

HEAD_DIM = 128
N_ATTN_HEADS = 12
ATTN_WIDTH = N_ATTN_HEADS * HEAD_DIM
N_FOURIER_GROUPS = 4
FOURIER_GROUP = 128
FOURIER_WIDTH = N_FOURIER_GROUPS * FOURIER_GROUP
MIX_WIDTH = ATTN_WIDTH + FOURIER_WIDTH
IN_WIDTH = 3 * ATTN_WIDTH + FOURIER_WIDTH
DILATED_PATTERNS = ((128, 1), (512, 4), (2048, 16))
BLOCK_Q = 64
ROPE_THETA = 10000.0
D_FF_DENSE = 5632
D_FF_EXPERT = 7168
N_EXPERTS = 8
TOP_K = 2
EPS = 1e-6
NEG_INF = -1e30
N_DENSE = (DEPTH + 1) // 2
N_MOE = DEPTH // 2

kernel_name = "hybrid_dilated_fourier_adaln_encoder"


def rms_norm(x, g):
    xf = x.astype(jnp.float32)
    y = xf * lax.rsqrt(jnp.mean(xf * xf, axis=-1, keepdims=True) + EPS)
    return (y * g.astype(jnp.float32)).astype(x.dtype)


def modulate(h, shift, scale):
    return h * (1 + scale[:, None, :]) + shift[:, None, :]


def rope(x, pos):
    half = HEAD_DIM // 2
    inv = jnp.power(ROPE_THETA, -jnp.arange(half, dtype=jnp.float32) * 2.0 / HEAD_DIM)
    ang = pos.astype(jnp.float32)[:, None] * inv[None, :]
    cos = jnp.cos(ang)[None, :, None, :]
    sin = jnp.sin(ang)[None, :, None, :]
    x1 = x[..., :half].astype(jnp.float32)
    x2 = x[..., half:].astype(jnp.float32)
    return jnp.concatenate([x1 * cos - x2 * sin, x2 * cos + x1 * sin], axis=-1).astype(x.dtype)


def dilated_window_attention(q, k, v, dilation, half_window):
    B, S, H, Dh = q.shape
    r = dilation
    L = S // r
    W = half_window // r
    nblk = -(-L // BLOCK_Q)
    Lp = nblk * BLOCK_Q

    def to_sub(t):
        return t.reshape(B, L, r, H, Dh).transpose(0, 2, 1, 3, 4)

    qs = jnp.pad(to_sub(q), ((0, 0), (0, 0), (0, Lp - L), (0, 0), (0, 0)))
    qs = qs.reshape(B, r, nblk, BLOCK_Q, H, Dh)

    def band(t):
        t = jnp.pad(to_sub(t), ((0, 0), (0, 0), (BLOCK_Q, Lp - L + BLOCK_Q), (0, 0), (0, 0)))
        t = t.reshape(B, r, nblk + 2, BLOCK_Q, H, Dh)
        return jnp.concatenate([t[:, :, :-2], t[:, :, 1:-1], t[:, :, 2:]], axis=3)

    kb, vb = band(k), band(v)
    qi = np.arange(BLOCK_Q)[:, None]
    kc = np.arange(3 * BLOCK_Q)[None, :]
    band_ok = np.abs(qi - kc + BLOCK_Q) <= W
    kpos = np.arange(nblk)[:, None] * BLOCK_Q - BLOCK_Q + np.arange(3 * BLOCK_Q)[None, :]
    key_ok = (kpos >= 0) & (kpos < L)
    mask = jnp.asarray(band_ok[None] & key_ok[:, None, :])

    s = jnp.einsum('brnqhd,brnkhd->brnhqk', qs, kb,
                   preferred_element_type=jnp.float32) * (1.0 / np.sqrt(HEAD_DIM))
    s = jnp.where(mask[None, None, :, None], s, NEG_INF)
    lse = jax.nn.logsumexp(s, axis=-1)
    p = jnp.exp(s - lse[..., None])
    o = jnp.einsum('brnhqk,brnkhd->brnqhd', p.astype(v.dtype), vb,
                   preferred_element_type=jnp.float32)
    o = o.reshape(B, r, Lp, H, Dh)[:, :, :L].transpose(0, 2, 1, 3, 4).reshape(B, S, H, Dh)
    lse = lse.transpose(0, 1, 2, 4, 3).reshape(B, r, Lp, H)[:, :, :L]
    lse = lse.transpose(0, 2, 1, 3).reshape(B, S, H)
    return o, lse


def token_mixer(h, pos, w_in, w_fourier, w_out):
    B, S, _ = h.shape
    z = h @ w_in
    q, k, v, f = jnp.split(z, [ATTN_WIDTH, 2 * ATTN_WIDTH, 3 * ATTN_WIDTH], axis=-1)
    q = rope(q.reshape(B, S, N_ATTN_HEADS, HEAD_DIM), pos)
    k = rope(k.reshape(B, S, N_ATTN_HEADS, HEAD_DIM), pos)
    v = v.reshape(B, S, N_ATTN_HEADS, HEAD_DIM)
    outs, lses = [], []
    for window, dil in DILATED_PATTERNS:
        o_p, l_p = dilated_window_attention(q, k, v, dil, window // 2)
        outs.append(o_p)
        lses.append(l_p)
    wts = jax.nn.softmax(jnp.stack(lses), axis=0)
    o_attn = jnp.einsum('pbsh,pbshd->bshd', wts, jnp.stack(outs))
    o_attn = o_attn.reshape(B, S, ATTN_WIDTH).astype(h.dtype)
    fg = f.reshape(B, S, N_FOURIER_GROUPS, FOURIER_GROUP).astype(jnp.float32)
    ff = jnp.fft.fft2(fg, axes=(1, 3), norm="ortho").real.astype(h.dtype)
    o_f = jnp.einsum('bsgc,gce->bsge', ff, w_fourier).reshape(B, S, FOURIER_WIDTH)
    return jnp.concatenate([o_attn, o_f], axis=-1) @ w_out


def swiglu(h, w1, w3, w2):
    return (jax.nn.silu(h @ w1) * (h @ w3)) @ w2


def moe_swiglu(h, router_w, w1, w3, w2):
    logits = (h @ router_w).astype(jnp.float32)
    top_val, top_idx = lax.top_k(logits, TOP_K)
    gates = jax.nn.softmax(top_val, axis=-1)
    combine = jnp.sum(jax.nn.one_hot(top_idx, N_EXPERTS, dtype=jnp.float32) * gates[..., None], axis=-2)
    out = jnp.zeros(h.shape, jnp.float32)
    for e in range(N_EXPERTS):
        out = out + combine[..., e:e + 1] * swiglu(h, w1[e], w3[e], w2[e]).astype(jnp.float32)
    return out.astype(h.dtype)


def encoder_trunk(x, c, w_mod, b_mod, norm_mix, w_in, w_fourier, w_out, norm_ffn,
                  dense_w1, dense_w3, dense_w2, router_w, moe_w1, moe_w3, moe_w2,
                  w_final_mod, b_final_mod, norm_final):
    S = x.shape[1]
    pos = jnp.arange(S)
    cs = jax.nn.silu(c)
    for l in range(DEPTH):
        mod = cs @ w_mod[l] + b_mod[l]
        sh1, sc1, g1, sh2, sc2, g2 = jnp.split(mod, 6, axis=-1)
        h = modulate(rms_norm(x, norm_mix[l]), sh1, sc1)
        x = x + g1[:, None, :] * token_mixer(h, pos, w_in[l], w_fourier[l], w_out[l])
        h = modulate(rms_norm(x, norm_ffn[l]), sh2, sc2)
        if l % 2 == 0:
            i = l // 2
            f = swiglu(h, dense_w1[i], dense_w3[i], dense_w2[i])
        else:
            i = l // 2
            f = moe_swiglu(h, router_w[i], moe_w1[i], moe_w3[i], moe_w2[i])
        x = x + g2[:, None, :] * f
    fmod = cs @ w_final_mod + b_final_mod
    shf, scf = jnp.split(fmod, 2, axis=-1)
    return modulate(rms_norm(x, norm_final), shf, scf)


def setup_inputs(seed: int = 0) -> dict:
    key = jax.random.key(seed)
    ks = jax.random.split(key, 24)
    f32 = jnp.float32

    def nrm(k, shape, s):
        return jax.random.normal(k, shape, f32) * s

    def gain(k, shape):
        return 1.0 + 0.02 * jax.random.normal(k, shape, f32)

    return {
        "x_prompt": nrm(ks[0], (BATCH, SEQ, D_MODEL), 1.0),
        "x_sample": nrm(ks[1], (DEC_BATCH, DEC_SEQ, D_MODEL), 1.0),
        "c_prompt": nrm(ks[2], (BATCH, D_MODEL), 1.0),
        "c_sample": nrm(ks[3], (DEC_BATCH, D_MODEL), 1.0),
        "w_mod": nrm(ks[4], (DEPTH, D_MODEL, 6 * D_MODEL), 0.5 * D_MODEL ** -0.5),
        "b_mod": nrm(ks[5], (DEPTH, 6 * D_MODEL), 0.02),
        "norm_mix": gain(ks[6], (DEPTH, D_MODEL)),
        "w_in": nrm(ks[7], (DEPTH, D_MODEL, IN_WIDTH), D_MODEL ** -0.5),
        "w_fourier": nrm(ks[8], (DEPTH, N_FOURIER_GROUPS, FOURIER_GROUP, FOURIER_GROUP), FOURIER_GROUP ** -0.5),
        "w_out": nrm(ks[9], (DEPTH, MIX_WIDTH, D_MODEL), MIX_WIDTH ** -0.5),
        "norm_ffn": gain(ks[10], (DEPTH, D_MODEL)),
        "dense_w1": nrm(ks[11], (N_DENSE, D_MODEL, D_FF_DENSE), D_MODEL ** -0.5),
        "dense_w3": nrm(ks[12], (N_DENSE, D_MODEL, D_FF_DENSE), D_MODEL ** -0.5),
        "dense_w2": nrm(ks[13], (N_DENSE, D_FF_DENSE, D_MODEL), D_FF_DENSE ** -0.5),
        "router_w": nrm(ks[14], (N_MOE, D_MODEL, N_EXPERTS), D_MODEL ** -0.5),
        "moe_w1": nrm(ks[15], (N_MOE, N_EXPERTS, D_MODEL, D_FF_EXPERT), D_MODEL ** -0.5),
        "moe_w3": nrm(ks[16], (N_MOE, N_EXPERTS, D_MODEL, D_FF_EXPERT), D_MODEL ** -0.5),
        "moe_w2": nrm(ks[17], (N_MOE, N_EXPERTS, D_FF_EXPERT, D_MODEL), D_FF_EXPERT ** -0.5),
        "w_final_mod": nrm(ks[18], (D_MODEL, 2 * D_MODEL), 0.5 * D_MODEL ** -0.5),
        "b_final_mod": nrm(ks[19], (2 * D_MODEL,), 0.02),
        "norm_final": gain(ks[20], (D_MODEL,)),
    }


def reference(x_prompt, x_sample, c_prompt, c_sample, w_mod, b_mod, norm_mix, w_in, w_fourier, w_out,
              norm_ffn, dense_w1, dense_w3, dense_w2, router_w, moe_w1, moe_w3, moe_w2,
              w_final_mod, b_final_mod, norm_final):
    y_prompt = encoder_trunk(x_prompt, c_prompt, w_mod, b_mod, norm_mix, w_in, w_fourier, w_out, norm_ffn,
                             dense_w1, dense_w3, dense_w2, router_w, moe_w1, moe_w3, moe_w2,
                             w_final_mod, b_final_mod, norm_final)
    y_sample = encoder_trunk(x_sample, c_sample, w_mod, b_mod, norm_mix, w_in, w_fourier, w_out, norm_ffn,
                             dense_w1, dense_w3, dense_w2, router_w, moe_w1, moe_w3, moe_w2,
                             w_final_mod, b_final_mod, norm_final)
    return (y_prompt, y_sample)
```

```python
import functools

import numpy as np
import jax
import jax.numpy as jnp
from jax import lax
from jax.experimental import pallas as pl
from jax.experimental.pallas import tpu as pltpu

F32 = jnp.float32
BF16 = jnp.bfloat16

D_MODEL = 2048
HEAD_DIM = 128
N_ATTN_HEADS = 12
ATTN_WIDTH = N_ATTN_HEADS * HEAD_DIM
N_FOURIER_GROUPS = 4
FOURIER_GROUP = 128
FOURIER_WIDTH = N_FOURIER_GROUPS * FOURIER_GROUP
IN_WIDTH = 3 * ATTN_WIDTH + FOURIER_WIDTH
DILATIONS = (1, 4, 16)
HALF_NEIGHBOURS = 64
ROPE_THETA = 10000.0
N_EXPERTS = 8
EPS = 1e-6
NEG_INF = -1e30
LANES = 128
MOD_ROWS = 16

VMEM_LIMIT = 48 * 1024 * 1024
ATTN_VMEM_LIMIT = 58 * 1024 * 1024


def _cparams(sem, vmem=VMEM_LIMIT, **kw):
    return pltpu.CompilerParams(dimension_semantics=sem, vmem_limit_bytes=vmem, **kw)


def _norm_mod(x, g, sh, sc):
    ms = jnp.mean(x * x, axis=-1, keepdims=True)
    return (x * lax.rsqrt(ms + EPS)) * g * (1.0 + sc) + sh


def _silu(a):
    return a / (1.0 + jnp.exp(-a))


def _mod_kernel(c_ref, w_ref, b_ref, o_ref):
    cs = _silu(c_ref[...]).astype(BF16)
    o_ref[...] = jnp.dot(cs, w_ref[...].astype(BF16), preferred_element_type=F32) + b_ref[...]


def _modulation(c_all, w, b, tn=1024):
    nl, d, n = w.shape
    return pl.pallas_call(
        _mod_kernel,
        grid=(nl, n // tn),
        in_specs=[
            pl.BlockSpec((MOD_ROWS, d), lambda l, j: (0, 0)),
            pl.BlockSpec((None, d, tn), lambda l, j: (l, 0, j)),
            pl.BlockSpec((None, 1, tn), lambda l, j: (l, 0, j)),
        ],
        out_specs=pl.BlockSpec((None, MOD_ROWS, tn), lambda l, j: (l, 0, j)),
        out_shape=jax.ShapeDtypeStruct((nl, MOD_ROWS, n), F32),
        compiler_params=_cparams(("parallel", "parallel")),
    )(c_all, w, b.reshape(nl, 1, n))


def _mod_spec(which, n_vec, row_batch, tm):
    def index(i, *_):
        return (row_batch(i * tm) * n_vec + which, 0, 0)
    return pl.BlockSpec((None, 1, D_MODEL), index)


def _in_proj_kernel(x_ref, g_ref, sh_ref, sc_ref, w_ref, o_ref, h_scr):
    @pl.when(pl.program_id(1) == 0)
    def _():
        h_scr[...] = _norm_mod(x_ref[...], g_ref[...], sh_ref[...], sc_ref[...]).astype(BF16)
    o_ref[...] = jnp.dot(h_scr[...], w_ref[...], preferred_element_type=F32)


def _in_proj(x, gain, mod, w_in, layer, row_batch, tm=512, tn=1280):
    n = x.shape[0]
    return pl.pallas_call(
        _in_proj_kernel,
        grid=(n // tm, IN_WIDTH // tn),
        in_specs=[
            pl.BlockSpec((tm, D_MODEL), lambda i, j: (i, 0)),
            pl.BlockSpec((None, 1, D_MODEL), lambda i, j: (layer, 0, 0)),
            _mod_spec(0, 6, row_batch, tm),
            _mod_spec(1, 6, row_batch, tm),
            pl.BlockSpec((None, D_MODEL, tn), lambda i, j: (layer, 0, j)),
        ],
        out_specs=pl.BlockSpec((tm, tn), lambda i, j: (i, j)),
        out_shape=jax.ShapeDtypeStruct((n, IN_WIDTH), F32),
        scratch_shapes=[pltpu.VMEM((tm, D_MODEL), BF16)],
        compiler_params=_cparams(("parallel", "arbitrary")),
    )(x, gain, mod, mod, w_in)


def _attn_kernel(q_ref, k_ref, v_ref, cos_ref, sin_ref, o_ref,
                 qr, kr, acc, mx, den, *, seq, bq):
    chunk = bq * DILATIONS[-1]
    cos = cos_ref[...]
    sin = sin_ref[...]
    q = q_ref[...]
    qr[...] = (q * cos + pltpu.roll(q, HEAD_DIM // 2, 1) * sin) * (1.0 / np.sqrt(HEAD_DIM))
    k = k_ref[...]
    kr[...] = k * cos + pltpu.roll(k, HEAD_DIM // 2, 1) * sin

    def do_chunk(c, carry):
        base = c * chunk
        for p, r in enumerate(DILATIONS):
            sub_len = seq // r
            kw = min(bq + 2 * HALF_NEIGHBOURS, sub_len)
            nb = chunk // (r * bq)
            row = lax.broadcasted_iota(jnp.int32, (bq, kw), 0)
            col = lax.broadcasted_iota(jnp.int32, (bq, kw), 1)
            rel = col - row

            def do_block(t, carry2, p=p, r=r, sub_len=sub_len, kw=kw, nb=nb, rel=rel):
                m = t // nb
                i = t % nb
                jb = c * nb + i
                q0 = jb * bq
                k0 = jnp.clip(q0 - HALF_NEIGHBOURS, 0, sub_len - kw)
                qb = qr[pl.ds(m + r * q0, bq, stride=r), :].astype(BF16)
                kb = kr[pl.ds(m + r * k0, kw, stride=r), :].astype(BF16)
                vb = v_ref[pl.ds(m + r * k0, kw, stride=r), :].astype(BF16)
                s = lax.dot_general(qb, kb, (((1,), (1,)), ((), ())),
                                    preferred_element_type=F32)
                ok = jnp.abs(rel + (k0 - q0)) <= HALF_NEIGHBOURS
                s = jnp.where(ok, s, NEG_INF)
                smax = jnp.max(s, axis=-1, keepdims=True)
                e = jnp.exp(s - smax)
                esum = jnp.sum(e, axis=-1, keepdims=True)
                a = jnp.dot(e.astype(BF16), vb, preferred_element_type=F32)
                dst = pl.ds(m + r * (i * bq), bq, stride=r)
                acc[p, dst, :] = a
                mx[p, dst, :] = jnp.broadcast_to(smax, (bq, HEAD_DIM))
                den[p, dst, :] = jnp.broadcast_to(esum, (bq, HEAD_DIM))
                return carry2

            lax.fori_loop(0, r * nb, do_block, 0)

        top = jnp.maximum(jnp.maximum(mx[0], mx[1]), mx[2])
        num = jnp.zeros((chunk, HEAD_DIM), F32)
        tot = jnp.zeros((chunk, HEAD_DIM), F32)
        for p in range(len(DILATIONS)):
            w = jnp.exp(mx[p] - top)
            num = num + w * acc[p]
            tot = tot + w * den[p]
        o_ref[pl.ds(pl.multiple_of(base, chunk), chunk), :] = (num / tot).astype(o_ref.dtype)
        return carry

    lax.fori_loop(0, seq // chunk, do_chunk, 0)


def _attention(z, o_prev, rope_cos, rope_sin, seq, batch_lo, batch_n, bq=64):
    n = z.shape[0]
    chunk = bq * DILATIONS[-1]
    assert seq % chunk == 0 and n % seq == 0
    zb = z.reshape(n // seq, seq, IN_WIDTH)

    def col_spec(off):
        return pl.BlockSpec((None, seq, HEAD_DIM),
                            lambda b, h: (batch_lo + b, 0, off + h))

    in_specs = [
        col_spec(0), col_spec(N_ATTN_HEADS), col_spec(2 * N_ATTN_HEADS),
        pl.BlockSpec((seq, HEAD_DIM), lambda b, h: (0, 0), pipeline_mode=pl.Buffered(1)),
        pl.BlockSpec((seq, HEAD_DIM), lambda b, h: (0, 0), pipeline_mode=pl.Buffered(1)),
    ]
    args = [zb, zb, zb, rope_cos, rope_sin]
    aliases = {}
    if o_prev is not None:
        in_specs.append(pl.BlockSpec(memory_space=pl.ANY))
        args.append(o_prev.reshape(n // seq, seq, ATTN_WIDTH))
        aliases = {5: 0}

    def kern(*refs):
        if o_prev is not None:
            refs = refs[:5] + refs[6:]
        _attn_kernel(*refs, seq=seq, bq=bq)

    out = pl.pallas_call(
        kern,
        grid=(batch_n, N_ATTN_HEADS),
        in_specs=in_specs,
        out_specs=pl.BlockSpec((None, seq, HEAD_DIM), lambda b, h: (batch_lo + b, 0, h)),
        out_shape=jax.ShapeDtypeStruct((n // seq, seq, ATTN_WIDTH), BF16),
        scratch_shapes=[
            pltpu.VMEM((seq, HEAD_DIM), F32),
            pltpu.VMEM((seq, HEAD_DIM), F32),
            pltpu.VMEM((len(DILATIONS), chunk, HEAD_DIM), F32),
            pltpu.VMEM((len(DILATIONS), chunk, HEAD_DIM), F32),
            pltpu.VMEM((len(DILATIONS), chunk, HEAD_DIM), F32),
        ],
        input_output_aliases=aliases,
        compiler_params=_cparams(("parallel", "parallel"), vmem=ATTN_VMEM_LIMIT),
    )(*args)
    return out.reshape(n, ATTN_WIDTH)


def _rope_tables(seq):
    half = HEAD_DIM // 2
    inv = jnp.power(ROPE_THETA, -jnp.arange(half, dtype=F32) * 2.0 / HEAD_DIM)
    ang = jnp.arange(seq, dtype=F32)[:, None] * inv[None, :]
    cos, sin = jnp.cos(ang), jnp.sin(ang)
    return jnp.concatenate([cos, cos], axis=-1), jnp.concatenate([-sin, sin], axis=-1)


def _dft_factors(seq):
    s1 = {2048: 32, 8192: 64}.get(seq)
    if s1 is None:
        s1 = 1
        while s1 * s1 < seq:
            s1 *= 2
        s1 = seq // s1 if (seq // s1) * s1 == seq else s1
    return s1, seq // s1


@functools.lru_cache(maxsize=None)
def _dft_tables(seq):
    s1, s2 = _dft_factors(seq)
    c = np.arange(FOURIER_GROUP)
    ang_c = 2.0 * np.pi * ((c[:, None] * c[None, :]) % FOURIER_GROUP) / FOURIER_GROUP
    norm = 1.0 / np.sqrt(float(seq) * FOURIER_GROUP)
    w_chan = np.concatenate([np.cos(ang_c), -np.sin(ang_c)], axis=1) * norm
    k1 = np.arange(s1)[None, :, None]
    n1 = np.arange(s1)[None, None, :]
    n2 = np.arange(s2)[:, None, None]
    ang1 = 2.0 * np.pi * ((k1 * (n2 + s2 * n1)) % seq) / seq
    stage1 = np.concatenate([np.cos(ang1), np.sin(ang1)], axis=1)
    j = np.arange(s2)
    ang2 = 2.0 * np.pi * ((j[:, None] * j[None, :]) % s2) / s2
    return (np.asarray(w_chan, np.float32), np.asarray(stage1, np.float32),
            np.asarray(np.cos(ang2), np.float32), np.asarray(np.sin(ang2), np.float32))


def _fourier_kernel(f_ref, wc_ref, st1_ref, c2_ref, s2_ref, wf_ref, o_ref,
                    zr_scr, zi_scr, tr_scr, ti_scr, y_scr, *, s1, s2):
    g = FOURIER_GROUP
    z = jnp.dot(f_ref[...].astype(BF16), wc_ref[...], preferred_element_type=F32)
    zr_scr[...] = z[:, :g]
    zi_scr[...] = z[:, g:]

    def stage1(n2, carry):
        rows = pl.ds(n2, s1, stride=s2)
        zs = jnp.concatenate([zr_scr[rows, :], zi_scr[rows, :]], axis=1).astype(BF16)
        pr = jnp.dot(st1_ref[n2], zs, preferred_element_type=F32)
        dst = pl.ds(pl.multiple_of(n2 * s1, s1), s1)
        tr_scr[dst, :] = pr[:s1, :g] + pr[s1:, g:]
        ti_scr[dst, :] = pr[:s1, g:] - pr[s1:, :g]
        return carry

    lax.fori_loop(0, s2, stage1, 0)

    def stage2(k1, carry):
        rows = pl.ds(k1, s2, stride=s1)
        y = (jnp.dot(c2_ref[...], tr_scr[rows, :].astype(BF16), preferred_element_type=F32)
             + jnp.dot(s2_ref[...], ti_scr[rows, :].astype(BF16), preferred_element_type=F32))
        y_scr[rows, :] = y
        return carry

    lax.fori_loop(0, s1, stage2, 0)
    o_ref[...] = jnp.dot(y_scr[...].astype(BF16), wf_ref[...],
                         preferred_element_type=F32).astype(o_ref.dtype)


def _fourier(z, o_prev, w_fourier, layer, seq, batch_lo, batch_n):
    n = z.shape[0]
    s1, s2 = _dft_factors(seq)
    w_chan, stage1, c2, sn2 = (jnp.asarray(t, BF16) for t in _dft_tables(seq))
    zb = z.reshape(n // seq, seq, IN_WIDTH)
    f_col = 3 * N_ATTN_HEADS
    in_specs = [
        pl.BlockSpec((None, seq, FOURIER_GROUP), lambda b, g: (batch_lo + b, 0, f_col + g)),
        pl.BlockSpec((FOURIER_GROUP, 2 * FOURIER_GROUP), lambda b, g: (0, 0)),
        pl.BlockSpec((s2, 2 * s1, s1), lambda b, g: (0, 0, 0)),
        pl.BlockSpec((s2, s2), lambda b, g: (0, 0)),
        pl.BlockSpec((s2, s2), lambda b, g: (0, 0)),
        pl.BlockSpec((None, None, FOURIER_GROUP, FOURIER_GROUP), lambda b, g: (layer, g, 0, 0)),
    ]
    args = [zb, w_chan, stage1, c2, sn2, w_fourier]
    aliases = {}
    if o_prev is not None:
        in_specs.append(pl.BlockSpec(memory_space=pl.ANY))
        args.append(o_prev.reshape(n // seq, seq, FOURIER_WIDTH))
        aliases = {6: 0}

    def kern(*refs):
        if o_prev is not None:
            refs = refs[:6] + refs[7:]
        _fourier_kernel(*refs, s1=s1, s2=s2)

    out = pl.pallas_call(
        kern,
        grid=(batch_n, N_FOURIER_GROUPS),
        in_specs=in_specs,
        out_specs=pl.BlockSpec((None, seq, FOURIER_GROUP), lambda b, g: (batch_lo + b, 0, g)),
        out_shape=jax.ShapeDtypeStruct((n // seq, seq, FOURIER_WIDTH), BF16),
        scratch_shapes=[pltpu.VMEM((seq, FOURIER_GROUP), F32)] * 5,
        input_output_aliases=aliases,
        compiler_params=_cparams(("parallel", "parallel")),
    )(*args)
    return out.reshape(n, FOURIER_WIDTH)


def _out_proj_kernel(oa_ref, of_ref, wa_ref, wf_ref, x_ref, g_ref, o_ref):
    y = jnp.dot(oa_ref[...], wa_ref[...], preferred_element_type=F32)
    y = y + jnp.dot(of_ref[...], wf_ref[...], preferred_element_type=F32)
    o_ref[...] = x_ref[...] + g_ref[...] * y


def _out_proj(o_attn, o_f, x, mod, w_out, layer, row_batch, tm=512, tn=1024):
    n = x.shape[0]
    fblk = ATTN_WIDTH // FOURIER_WIDTH

    def gate_index(i, j):
        return (row_batch(i * tm) * 6 + 2, 0, j)

    return pl.pallas_call(
        _out_proj_kernel,
        grid=(n // tm, D_MODEL // tn),
        in_specs=[
            pl.BlockSpec((tm, ATTN_WIDTH), lambda i, j: (i, 0)),
            pl.BlockSpec((tm, FOURIER_WIDTH), lambda i, j: (i, 0)),
            pl.BlockSpec((None, ATTN_WIDTH, tn), lambda i, j: (layer, 0, j)),
            pl.BlockSpec((None, FOURIER_WIDTH, tn), lambda i, j: (layer, fblk, j)),
            pl.BlockSpec((tm, tn), lambda i, j: (i, j)),
            pl.BlockSpec((None, 1, tn), gate_index),
        ],
        out_specs=pl.BlockSpec((tm, tn), lambda i, j: (i, j)),
        out_shape=jax.ShapeDtypeStruct((n, D_MODEL), F32),
        compiler_params=_cparams(("parallel", "parallel")),
    )(o_attn, o_f, w_out, w_out, x, mod)


def _ffn_dense_kernel(x_ref, g_ref, sh_ref, sc_ref, gate_ref, w1_ref, w3_ref, w2_ref, o_ref,
                      h_scr, acc):
    j = pl.program_id(1)

    @pl.when(j == 0)
    def _():
        h_scr[...] = _norm_mod(x_ref[...], g_ref[...], sh_ref[...], sc_ref[...]).astype(BF16)
        acc[...] = jnp.zeros_like(acc)

    h = h_scr[...]
    a = jnp.dot(h, w1_ref[...], preferred_element_type=F32)
    b = jnp.dot(h, w3_ref[...], preferred_element_type=F32)
    acc[...] += jnp.dot((_silu(a) * b).astype(BF16), w2_ref[...], preferred_element_type=F32)

    @pl.when(j == pl.num_programs(1) - 1)
    def _():
        o_ref[...] = x_ref[...] + gate_ref[...] * acc[...]


def _ffn_dense(x, gain, mod, w1, w3, w2, layer, idx, row_batch, tm=512, tf=512):
    n = x.shape[0]
    d_ff = w1.shape[-1]
    return pl.pallas_call(
        _ffn_dense_kernel,
        grid=(n // tm, d_ff // tf),
        in_specs=[
            pl.BlockSpec((tm, D_MODEL), lambda i, j: (i, 0)),
            pl.BlockSpec((None, 1, D_MODEL), lambda i, j: (layer, 0, 0)),
            _mod_spec(3, 6, row_batch, tm),
            _mod_spec(4, 6, row_batch, tm),
            _mod_spec(5, 6, row_batch, tm),
            pl.BlockSpec((None, D_MODEL, tf), lambda i, j: (idx, 0, j)),
            pl.BlockSpec((None, D_MODEL, tf), lambda i, j: (idx, 0, j)),
            pl.BlockSpec((None, tf, D_MODEL), lambda i, j: (idx, j, 0)),
        ],
        out_specs=pl.BlockSpec((tm, D_MODEL), lambda i, j: (i, 0)),
        out_shape=jax.ShapeDtypeStruct((n, D_MODEL), F32),
        scratch_shapes=[pltpu.VMEM((tm, D_MODEL), BF16), pltpu.VMEM((tm, D_MODEL), F32)],
        compiler_params=_cparams(("parallel", "arbitrary")),
    )(x, gain, mod, mod, mod, w1, w3, w2)


def _router_kernel(x_ref, g_ref, sh_ref, sc_ref, rw_ref, h_ref, idx_ref, gate_ref, rank_ref,
                   cnt_ref, cnt_scr, *, tm):
    @pl.when(pl.program_id(0) == 0)
    def _():
        cnt_scr[...] = jnp.zeros_like(cnt_scr)

    h = _norm_mod(x_ref[...], g_ref[...], sh_ref[...], sc_ref[...])
    h_ref[...] = h
    logits = jnp.dot(h, rw_ref[...], preferred_element_type=F32,
                     precision=lax.Precision.HIGHEST)
    lane = lax.broadcasted_iota(jnp.int32, (tm, LANES), 1).astype(F32)
    logits = jnp.where(lane < N_EXPERTS, logits, -jnp.inf)
    v1 = jnp.max(logits, axis=-1, keepdims=True)
    i1 = jnp.min(jnp.where(logits == v1, lane, float(LANES)), axis=-1, keepdims=True)
    rest = jnp.where(lane == i1, -jnp.inf, logits)
    v2 = jnp.max(rest, axis=-1, keepdims=True)
    i2 = jnp.min(jnp.where(rest == v2, lane, float(LANES)), axis=-1, keepdims=True)
    e2 = jnp.exp(v2 - v1)
    gate1 = 1.0 / (1.0 + e2)
    gate2 = e2 / (1.0 + e2)

    hot1 = (lane == i1).astype(BF16)
    hot2 = (lane == i2).astype(BF16)
    r_i = lax.broadcasted_iota(jnp.int32, (tm, tm), 0)
    c_i = lax.broadcasted_iota(jnp.int32, (tm, tm), 1)
    before = (c_i < r_i).astype(BF16)
    pre1 = jnp.dot(before, hot1, preferred_element_type=F32)
    pre2 = jnp.dot(before, hot2, preferred_element_type=F32)
    tot1 = jnp.sum(hot1.astype(F32), axis=0, keepdims=True)
    tot2 = jnp.sum(hot2.astype(F32), axis=0, keepdims=True)
    cnt = cnt_scr[...]
    rank1 = jnp.sum(jnp.where(lane == i1, pre1 + cnt, 0.0), axis=-1, keepdims=True)
    rank2 = jnp.sum(jnp.where(lane == i2, pre2 + cnt + tot1, 0.0), axis=-1, keepdims=True)
    cnt = cnt + tot1 + tot2
    cnt_scr[...] = cnt
    cnt_ref[...] = jnp.broadcast_to(cnt, cnt_ref.shape).astype(jnp.int32)

    idx_ref[...] = jnp.where(lane == 0, i1, jnp.where(lane == 1, i2, 0.0)).astype(jnp.int32)
    gate_ref[...] = jnp.where(lane == 0, gate1, jnp.where(lane == 1, gate2, 0.0))
    rank_ref[...] = jnp.where(lane == 0, rank1, jnp.where(lane == 1, rank2, 0.0)).astype(jnp.int32)


def _router(x, gain, mod, router_w, layer, row_batch, tm=512):
    n = x.shape[0]
    rw = jnp.zeros((D_MODEL, LANES), F32).at[:, :N_EXPERTS].set(router_w)
    tile = lambda i: (i, 0)
    return pl.pallas_call(
        functools.partial(_router_kernel, tm=tm),
        grid=(n // tm,),
        in_specs=[
            pl.BlockSpec((tm, D_MODEL), tile),
            pl.BlockSpec((None, 1, D_MODEL), lambda i: (layer, 0, 0)),
            _mod_spec(3, 6, row_batch, tm),
            _mod_spec(4, 6, row_batch, tm),
            pl.BlockSpec((D_MODEL, LANES), lambda i: (0, 0)),
        ],
        out_specs=[
            pl.BlockSpec((tm, D_MODEL), tile),
            pl.BlockSpec((tm, LANES), tile),
            pl.BlockSpec((tm, LANES), tile),
            pl.BlockSpec((tm, LANES), tile),
            pl.BlockSpec((8, LANES), lambda i: (0, 0)),
        ],
        out_shape=[
            jax.ShapeDtypeStruct((n, D_MODEL), F32),
            jax.ShapeDtypeStruct((n, LANES), jnp.int32),
            jax.ShapeDtypeStruct((n, LANES), F32),
            jax.ShapeDtypeStruct((n, LANES), jnp.int32),
            jax.ShapeDtypeStruct((8, LANES), jnp.int32),
        ],
        scratch_shapes=[pltpu.VMEM((1, LANES), F32)],
        compiler_params=_cparams(("arbitrary",)),
    )(x, gain, mod, mod, rw)


def _dispatch_kernel(pos_ref, h_ref, xs_in_ref, xs_ref, sem, *, tt):
    del xs_in_ref
    base = pl.program_id(0) * tt

    def copy(t, slot):
        return pltpu.make_async_copy(h_ref.at[pl.ds(t, 1), :],
                                     xs_ref.at[pl.ds(pos_ref[2 * (base + t) + slot], 1), :],
                                     sem)

    def issue(t, c):
        copy(t, 0).start()
        copy(t, 1).start()
        return c

    def drain(t, c):
        copy(t, 0).wait()
        copy(t, 1).wait()
        return c

    lax.fori_loop(0, tt, issue, 0)
    lax.fori_loop(0, tt, drain, 0)


def _dispatch(h, pos, xs_init, tt=256):
    n = h.shape[0]
    return pl.pallas_call(
        functools.partial(_dispatch_kernel, tt=tt),
        grid_spec=pltpu.PrefetchScalarGridSpec(
            num_scalar_prefetch=1,
            grid=(n // tt,),
            in_specs=[
                pl.BlockSpec((tt, D_MODEL), lambda i, pos: (i, 0)),
                pl.BlockSpec(memory_space=pl.ANY),
            ],
            out_specs=pl.BlockSpec(memory_space=pl.ANY),
            scratch_shapes=[pltpu.SemaphoreType.DMA(())],
        ),
        out_shape=jax.ShapeDtypeStruct(xs_init.shape, xs_init.dtype),
        input_output_aliases={2: 0},
        compiler_params=_cparams(("arbitrary",), has_side_effects=True),
    )(pos, h, xs_init)


def _moe_ffn_kernel(te_ref, nv_ref, x_ref, w1_ref, w3_ref, w2_ref, o_ref, h_scr, acc):
    i = pl.program_id(0)
    j = pl.program_id(1)

    @pl.when(i < nv_ref[0])
    def _():
        @pl.when(j == 0)
        def _():
            h_scr[...] = x_ref[...].astype(BF16)
            acc[...] = jnp.zeros_like(acc)

        h = h_scr[...]
        a = jnp.dot(h, w1_ref[...], preferred_element_type=F32)
        b = jnp.dot(h, w3_ref[...], preferred_element_type=F32)
        acc[...] += jnp.dot((_silu(a) * b).astype(BF16), w2_ref[...],
                            preferred_element_type=F32)

        @pl.when(j == pl.num_programs(1) - 1)
        def _():
            o_ref[...] = acc[...]


def _moe_ffn(xs, tile_expert, n_valid, w1, w3, w2, idx, tm, tf=512):
    rows = xs.shape[0]
    d_ff = w1.shape[-1]
    nj = d_ff // tf

    def row_index(i, j, te, nv):
        return (jnp.minimum(i, nv[0] - 1), 0)

    def up_index(i, j, te, nv):
        return (idx, te[i], 0, jnp.where(i < nv[0], j, nj - 1))

    def down_index(i, j, te, nv):
        return (idx, te[i], jnp.where(i < nv[0], j, nj - 1), 0)

    return pl.pallas_call(
        _moe_ffn_kernel,
        grid_spec=pltpu.PrefetchScalarGridSpec(
            num_scalar_prefetch=2,
            grid=(rows // tm, nj),
            in_specs=[
                pl.BlockSpec((tm, D_MODEL), row_index),
                pl.BlockSpec((None, None, D_MODEL, tf), up_index),
                pl.BlockSpec((None, None, D_MODEL, tf), up_index),
                pl.BlockSpec((None, None, tf, D_MODEL), down_index),
            ],
            out_specs=pl.BlockSpec((tm, D_MODEL), row_index),
            scratch_shapes=[pltpu.VMEM((tm, D_MODEL), BF16), pltpu.VMEM((tm, D_MODEL), F32)],
        ),
        out_shape=jax.ShapeDtypeStruct((rows, D_MODEL), F32),
        compiler_params=_cparams(("arbitrary", "arbitrary")),
    )(tile_expert, n_valid, xs, w1, w3, w2)


def _combine_kernel(pos_ref, ys_ref, x_ref, gate_ref, g2_ref, o_ref, buf, sem, *, tt):
    base = pl.program_id(0) * tt

    def copy(t, slot):
        return pltpu.make_async_copy(ys_ref.at[pl.ds(pos_ref[2 * (base + t) + slot], 1), :],
                                     buf.at[slot, pl.ds(t, 1), :], sem)

    def issue(t, c):
        copy(t, 0).start()
        copy(t, 1).start()
        return c

    def drain(t, c):
        copy(t, 0).wait()
        copy(t, 1).wait()
        return c

    lax.fori_loop(0, tt, issue, 0)
    lax.fori_loop(0, tt, drain, 0)
    gates = gate_ref[...]
    f = gates[:, 0:1] * buf[0] + gates[:, 1:2] * buf[1]
    o_ref[...] = x_ref[...] + g2_ref[...] * f


def _combine(ys, pos, x, gates, mod, row_batch, tt=256):
    n = x.shape[0]
    return pl.pallas_call(
        functools.partial(_combine_kernel, tt=tt),
        grid_spec=pltpu.PrefetchScalarGridSpec(
            num_scalar_prefetch=1,
            grid=(n // tt,),
            in_specs=[
                pl.BlockSpec(memory_space=pl.ANY),
                pl.BlockSpec((tt, D_MODEL), lambda i, pos: (i, 0)),
                pl.BlockSpec((tt, LANES), lambda i, pos: (i, 0)),
                _mod_spec(5, 6, row_batch, tt),
            ],
            out_specs=pl.BlockSpec((tt, D_MODEL), lambda i, pos: (i, 0)),
            scratch_shapes=[pltpu.VMEM((2, tt, D_MODEL), F32), pltpu.SemaphoreType.DMA(())],
        ),
        out_shape=jax.ShapeDtypeStruct((n, D_MODEL), F32),
        compiler_params=_cparams(("arbitrary",)),
    )(pos, ys, x, gates, mod)


def _moe_layer(x, gain, mod, router_w, w1, w3, w2, layer, idx, row_batch, tm=512):
    n = x.shape[0]
    h, top_idx, gates, rank, counts = _router(x, gain, mod, router_w[idx], layer, row_batch)
    counts = counts[0, :N_EXPERTS]
    padded = ((counts + tm - 1) // tm) * tm
    ends = jnp.cumsum(padded)
    starts = ends - padded
    experts = top_idx[:, :2]
    pos = (starts[experts] + rank[:, :2]).reshape(-1).astype(jnp.int32)
    n_tiles = (2 * n) // tm + N_EXPERTS
    tile_start = jnp.arange(n_tiles, dtype=jnp.int32) * tm
    tile_expert = jnp.minimum(jnp.searchsorted(ends, tile_start, side="right"),
                              N_EXPERTS - 1).astype(jnp.int32)
    n_valid = (ends[-1:] // tm).astype(jnp.int32)
    xs = _dispatch(h, pos, jnp.zeros((n_tiles * tm, D_MODEL), F32))
    ys = _moe_ffn(xs, tile_expert, n_valid, w1, w3, w2, idx, tm)
    return _combine(ys, pos, x, gates, mod, row_batch)


def _final_kernel(x_ref, g_ref, sh_ref, sc_ref, o_ref):
    o_ref[...] = _norm_mod(x_ref[...], g_ref[...], sh_ref[...], sc_ref[...])


def _final(x, gain, fmod, row_lo, rows, row_batch, tm=512):
    off = row_lo // tm

    def mod_spec(which):
        return pl.BlockSpec((None, 1, D_MODEL),
                            lambda i: (row_batch((i + off) * tm) * 2 + which, 0, 0))

    return pl.pallas_call(
        _final_kernel,
        grid=(rows // tm,),
        in_specs=[
            pl.BlockSpec((tm, D_MODEL), lambda i: (i + off, 0)),
            pl.BlockSpec((1, D_MODEL), lambda i: (0, 0)),
            mod_spec(0),
            mod_spec(1),
        ],
        out_specs=pl.BlockSpec((tm, D_MODEL), lambda i: (i, 0)),
        out_shape=jax.ShapeDtypeStruct((rows, D_MODEL), F32),
        compiler_params=_cparams(("parallel",)),
    )(x, gain, fmod, fmod)


def kernel(x_prompt, x_sample, c_prompt, c_sample, w_mod, b_mod, norm_mix, w_in, w_fourier, w_out,
           norm_ffn, dense_w1, dense_w3, dense_w2, router_w, moe_w1, moe_w3, moe_w2,
           w_final_mod, b_final_mod, norm_final):
    batch, seq_p, d = x_prompt.shape
    batch_s, seq_s, _ = x_sample.shape
    n_p, n_s = batch * seq_p, batch_s * seq_s
    n = n_p + n_s
    depth = w_mod.shape[0]
    assert d == D_MODEL and batch_s == 1 and n_p % seq_s == 0 and batch + batch_s <= MOD_ROWS

    def row_batch(row):
        return jnp.minimum(row // seq_p, batch)

    x = jnp.concatenate([x_prompt.reshape(n_p, d), x_sample.reshape(n_s, d)], axis=0)
    c_all = jnp.zeros((MOD_ROWS, d), F32).at[:batch].set(c_prompt).at[batch:batch + 1].set(c_sample)

    mod = _modulation(c_all, w_mod, b_mod)
    fmod = _modulation(c_all, w_final_mod[None], b_final_mod[None])[0]
    fmod = fmod.reshape(MOD_ROWS * 2, 1, d)

    gain_mix = norm_mix.reshape(depth, 1, d)
    gain_ffn = norm_ffn.reshape(depth, 1, d)
    w_in_b = w_in.astype(BF16)
    w_out_b = w_out.astype(BF16)
    w_fourier_b = w_fourier.astype(BF16)
    dense_b = [w.astype(BF16) for w in (dense_w1, dense_w3, dense_w2)]
    moe_b = [w.astype(BF16) for w in (moe_w1, moe_w3, moe_w2)]
    rope_p = _rope_tables(seq_p)
    rope_s = _rope_tables(seq_s)

    for l in range(depth):
        mod_l = mod[l].reshape(MOD_ROWS * 6, 1, d)
        z = _in_proj(x, gain_mix, mod_l, w_in_b, l, row_batch)
        o_attn = _attention(z, None, *rope_p, seq_p, 0, batch)
        o_attn = _attention(z, o_attn, *rope_s, seq_s, n_p // seq_s, batch_s)
        o_f = _fourier(z, None, w_fourier_b, l, seq_p, 0, batch)
        o_f = _fourier(z, o_f, w_fourier_b, l, seq_s, n_p // seq_s, batch_s)
        x = _out_proj(o_attn, o_f, x, mod_l, w_out_b, l, row_batch)
        if l % 2 == 0:
            x = _ffn_dense(x, gain_ffn, mod_l, *dense_b, l, l // 2, row_batch)
        else:
            x = _moe_layer(x, gain_ffn, mod_l, router_w, *moe_b, l, l // 2, row_batch)

    y_p = _final(x, norm_final.reshape(1, d), fmod, 0, n_p, row_batch)
    y_s = _final(x, norm_final.reshape(1, d), fmod, n_p, n_s, row_batch)
    return (y_p.reshape(batch, seq_p, d), y_s.reshape(batch_s, seq_s, d))
```

```python
import functools

import numpy as np
import jax
import jax.numpy as jnp
from jax import lax
from jax.experimental import pallas as pl
from jax.experimental.pallas import tpu as pltpu

F32 = jnp.float32
BF16 = jnp.bfloat16

D_MODEL = 2048
HEAD_DIM = 128
N_ATTN_HEADS = 12
ATTN_WIDTH = N_ATTN_HEADS * HEAD_DIM
N_FOURIER_GROUPS = 4
FOURIER_GROUP = 128
FOURIER_WIDTH = N_FOURIER_GROUPS * FOURIER_GROUP
IN_WIDTH = 3 * ATTN_WIDTH + FOURIER_WIDTH
DILATIONS = (1, 4, 16)
HALF_NEIGHBOURS = 64
ROPE_THETA = 10000.0
N_EXPERTS = 8
EPS = 1e-6
NEG_INF = -1e30
LANES = 128
MOD_ROWS = 16

VMEM_LIMIT = 48 * 1024 * 1024
ATTN_VMEM_LIMIT = 58 * 1024 * 1024
BLOCK_GROUP = 8
DFT_GROUP = 8


def _cparams(sem, vmem=VMEM_LIMIT, **kw):
    return pltpu.CompilerParams(dimension_semantics=sem, vmem_limit_bytes=vmem, **kw)


def _norm_mod(x, g, sh, sc):
    ms = jnp.mean(x * x, axis=-1, keepdims=True)
    return (x * lax.rsqrt(ms + EPS)) * g * (1.0 + sc) + sh


def _silu(a):
    return a / (1.0 + jnp.exp(-a))


def _mod_kernel(c_ref, w_ref, b_ref, o_ref):
    cs = _silu(c_ref[...]).astype(BF16)
    o_ref[...] = jnp.dot(cs, w_ref[...].astype(BF16), preferred_element_type=F32) + b_ref[...]


def _modulation(c_all, w, b, tn=1024):
    nl, d, n = w.shape
    return pl.pallas_call(
        _mod_kernel,
        grid=(nl, n // tn),
        in_specs=[
            pl.BlockSpec((MOD_ROWS, d), lambda l, j: (0, 0)),
            pl.BlockSpec((None, d, tn), lambda l, j: (l, 0, j)),
            pl.BlockSpec((None, 1, tn), lambda l, j: (l, 0, j)),
        ],
        out_specs=pl.BlockSpec((None, MOD_ROWS, tn), lambda l, j: (l, 0, j)),
        out_shape=jax.ShapeDtypeStruct((nl, MOD_ROWS, n), F32),
        compiler_params=_cparams(("parallel", "parallel")),
    )(c_all, w, b.reshape(nl, 1, n))


def _mod_spec(which, n_vec, row_batch, tm):
    def index(i, *_):
        return (row_batch(i * tm) * n_vec + which, 0, 0)
    return pl.BlockSpec((None, 1, D_MODEL), index)


def _in_proj_kernel(x_ref, g_ref, sh_ref, sc_ref, w_ref, o_ref, h_scr):
    @pl.when(pl.program_id(1) == 0)
    def _():
        h_scr[...] = _norm_mod(x_ref[...], g_ref[...], sh_ref[...], sc_ref[...]).astype(BF16)
    o_ref[...] = jnp.dot(h_scr[...], w_ref[...], preferred_element_type=F32).astype(o_ref.dtype)


def _in_proj(x, gain, mod, w_in, layer, row_batch, tm=1024, tn=1280):
    n = x.shape[0]
    return pl.pallas_call(
        _in_proj_kernel,
        grid=(n // tm, IN_WIDTH // tn),
        in_specs=[
            pl.BlockSpec((tm, D_MODEL), lambda i, j: (i, 0)),
            pl.BlockSpec((None, 1, D_MODEL), lambda i, j: (layer, 0, 0)),
            _mod_spec(0, 6, row_batch, tm),
            _mod_spec(1, 6, row_batch, tm),
            pl.BlockSpec((None, D_MODEL, tn), lambda i, j: (layer, 0, j)),
        ],
        out_specs=pl.BlockSpec((tm, tn), lambda i, j: (i, j)),
        out_shape=jax.ShapeDtypeStruct((n, IN_WIDTH), BF16),
        scratch_shapes=[pltpu.VMEM((tm, D_MODEL), BF16)],
        compiler_params=_cparams(("parallel", "arbitrary")),
    )(x, gain, mod, mod, w_in)


def _attn_kernel(q_ref, k_ref, v_ref, cos_ref, sin_ref, o_ref,
                 qr, kr, vr, acc, mx, den, *, seq, bq):
    chunk = bq * DILATIONS[-1]
    cos = cos_ref[...]
    sin = sin_ref[...]
    q = q_ref[...].astype(F32)
    qr[...] = (q * cos + pltpu.roll(q, HEAD_DIM // 2, 1) * sin) * (1.0 / np.sqrt(HEAD_DIM))
    k = k_ref[...].astype(F32)
    kr[...] = k * cos + pltpu.roll(k, HEAD_DIM // 2, 1) * sin
    vr[...] = v_ref[...].astype(F32)

    def do_chunk(c, carry):
        base = c * chunk
        for p, r in enumerate(DILATIONS):
            sub_len = seq // r
            kw = min(bq + 2 * HALF_NEIGHBOURS, sub_len)
            nb = chunk // (r * bq)
            row = lax.broadcasted_iota(jnp.int32, (bq, kw), 0)
            col = lax.broadcasted_iota(jnp.int32, (bq, kw), 1)
            rel = col - row

            def do_blocks(t, carry2, p=p, r=r, sub_len=sub_len, kw=kw, nb=nb, rel=rel):
                work = []
                for g in range(BLOCK_GROUP):
                    tg = t * BLOCK_GROUP + g
                    m = tg // nb
                    i = tg % nb
                    q0 = (c * nb + i) * bq
                    k0 = jnp.clip(q0 - HALF_NEIGHBOURS, 0, sub_len - kw)
                    qb = qr[pl.ds(m + r * q0, bq, stride=r), :].astype(BF16)
                    kb = kr[pl.ds(m + r * k0, kw, stride=r), :].astype(BF16)
                    vb = vr[pl.ds(m + r * k0, kw, stride=r), :].astype(BF16)
                    dst = pl.ds(m + r * (i * bq), bq, stride=r)
                    work.append((qb, kb, vb, k0 - q0, dst))
                scores = [lax.dot_general(qb, kb, (((1,), (1,)), ((), ())),
                                          preferred_element_type=F32)
                          for qb, kb, _, _, _ in work]
                weights = []
                for s, (_, _, _, off, dst) in zip(scores, work):
                    s = jnp.where(jnp.abs(rel + off) <= HALF_NEIGHBOURS, s, NEG_INF)
                    smax = jnp.max(s, axis=-1, keepdims=True)
                    e = jnp.exp(s - smax)
                    mx[p, dst, :] = jnp.broadcast_to(smax, (bq, HEAD_DIM))
                    den[p, dst, :] = jnp.broadcast_to(jnp.sum(e, axis=-1, keepdims=True),
                                                      (bq, HEAD_DIM))
                    weights.append(e.astype(BF16))
                for e, (_, _, vb, _, dst) in zip(weights, work):
                    acc[p, dst, :] = jnp.dot(e, vb, preferred_element_type=F32)
                return carry2

            lax.fori_loop(0, (r * nb) // BLOCK_GROUP, do_blocks, 0)

        top = jnp.maximum(jnp.maximum(mx[0], mx[1]), mx[2])
        num = jnp.zeros((chunk, HEAD_DIM), F32)
        tot = jnp.zeros((chunk, HEAD_DIM), F32)
        for p in range(len(DILATIONS)):
            w = jnp.exp(mx[p] - top)
            num = num + w * acc[p]
            tot = tot + w * den[p]
        o_ref[pl.ds(pl.multiple_of(base, chunk), chunk), :] = (num / tot).astype(o_ref.dtype)
        return carry

    lax.fori_loop(0, seq // chunk, do_chunk, 0)


def _attention(z, o_prev, rope_cos, rope_sin, seq, batch_lo, batch_n, bq=64):
    n = z.shape[0]
    chunk = bq * DILATIONS[-1]
    assert seq % chunk == 0 and n % seq == 0
    zb = z.reshape(n // seq, seq, IN_WIDTH)

    def col_spec(off):
        return pl.BlockSpec((None, seq, HEAD_DIM),
                            lambda b, h: (batch_lo + b, 0, off + h))

    in_specs = [
        col_spec(0), col_spec(N_ATTN_HEADS), col_spec(2 * N_ATTN_HEADS),
        pl.BlockSpec((seq, HEAD_DIM), lambda b, h: (0, 0), pipeline_mode=pl.Buffered(1)),
        pl.BlockSpec((seq, HEAD_DIM), lambda b, h: (0, 0), pipeline_mode=pl.Buffered(1)),
    ]
    args = [zb, zb, zb, rope_cos, rope_sin]
    aliases = {}
    if o_prev is not None:
        in_specs.append(pl.BlockSpec(memory_space=pl.ANY))
        args.append(o_prev.reshape(n // seq, seq, ATTN_WIDTH))
        aliases = {5: 0}

    def kern(*refs):
        if o_prev is not None:
            refs = refs[:5] + refs[6:]
        _attn_kernel(*refs, seq=seq, bq=bq)

    out = pl.pallas_call(
        kern,
        grid=(batch_n, N_ATTN_HEADS),
        in_specs=in_specs,
        out_specs=pl.BlockSpec((None, seq, HEAD_DIM), lambda b, h: (batch_lo + b, 0, h)),
        out_shape=jax.ShapeDtypeStruct((n // seq, seq, ATTN_WIDTH), BF16),
        scratch_shapes=[
            pltpu.VMEM((seq, HEAD_DIM), F32),
            pltpu.VMEM((seq, HEAD_DIM), F32),
            pltpu.VMEM((seq, HEAD_DIM), F32),
            pltpu.VMEM((len(DILATIONS), chunk, HEAD_DIM), F32),
            pltpu.VMEM((len(DILATIONS), chunk, HEAD_DIM), F32),
            pltpu.VMEM((len(DILATIONS), chunk, HEAD_DIM), F32),
        ],
        input_output_aliases=aliases,
        compiler_params=_cparams(("parallel", "parallel"), vmem=ATTN_VMEM_LIMIT),
    )(*args)
    return out.reshape(n, ATTN_WIDTH)


def _rope_tables(seq):
    half = HEAD_DIM // 2
    inv = jnp.power(ROPE_THETA, -jnp.arange(half, dtype=F32) * 2.0 / HEAD_DIM)
    ang = jnp.arange(seq, dtype=F32)[:, None] * inv[None, :]
    cos, sin = jnp.cos(ang), jnp.sin(ang)
    return jnp.concatenate([cos, cos], axis=-1), jnp.concatenate([-sin, sin], axis=-1)


def _dft_factors(seq):
    s1 = {2048: 32, 8192: 64}.get(seq)
    if s1 is None:
        s1 = 1
        while s1 * s1 < seq:
            s1 *= 2
        s1 = seq // s1 if (seq // s1) * s1 == seq else s1
    return s1, seq // s1


@functools.lru_cache(maxsize=None)
def _dft_tables(seq):
    s1, s2 = _dft_factors(seq)
    c = np.arange(FOURIER_GROUP)
    ang_c = 2.0 * np.pi * ((c[:, None] * c[None, :]) % FOURIER_GROUP) / FOURIER_GROUP
    norm = 1.0 / np.sqrt(float(seq) * FOURIER_GROUP)
    w_chan = np.concatenate([np.cos(ang_c), -np.sin(ang_c)], axis=1) * norm
    k1 = np.arange(s1)[None, :, None]
    n1 = np.arange(s1)[None, None, :]
    n2 = np.arange(s2)[:, None, None]
    ang1 = 2.0 * np.pi * ((k1 * (n2 + s2 * n1)) % seq) / seq
    stage1 = np.concatenate([np.cos(ang1), np.sin(ang1)], axis=1)
    j = np.arange(s2)
    ang2 = 2.0 * np.pi * ((j[:, None] * j[None, :]) % s2) / s2
    return (np.asarray(w_chan, np.float32), np.asarray(stage1, np.float32),
            np.asarray(np.cos(ang2), np.float32), np.asarray(np.sin(ang2), np.float32))


def _fourier_kernel(f_ref, wc_ref, st1_ref, c2_ref, s2_ref, wf_ref, o_ref,
                    zr_scr, zi_scr, tr_scr, ti_scr, y_scr, *, s1, s2):
    g = FOURIER_GROUP
    z = jnp.dot(f_ref[...], wc_ref[...], preferred_element_type=F32)
    zr_scr[...] = z[:, :g]
    zi_scr[...] = z[:, g:]

    def stage1(t, carry):
        n2s = [t * DFT_GROUP + u for u in range(DFT_GROUP)]
        zs = []
        for n2 in n2s:
            rows = pl.ds(n2, s1, stride=s2)
            zs.append(jnp.concatenate([zr_scr[rows, :], zi_scr[rows, :]], axis=1).astype(BF16))
        prs = [jnp.dot(st1_ref[n2], z, preferred_element_type=F32)
               for n2, z in zip(n2s, zs)]
        for n2, pr in zip(n2s, prs):
            dst = pl.ds(pl.multiple_of(n2 * s1, s1), s1)
            tr_scr[dst, :] = pr[:s1, :g] + pr[s1:, g:]
            ti_scr[dst, :] = pr[:s1, g:] - pr[s1:, :g]
        return carry

    lax.fori_loop(0, s2 // DFT_GROUP, stage1, 0)

    def stage2(t, carry):
        rows = [pl.ds(t * DFT_GROUP + u, s2, stride=s1) for u in range(DFT_GROUP)]
        ts = [(tr_scr[r, :].astype(BF16), ti_scr[r, :].astype(BF16)) for r in rows]
        ys = [jnp.dot(c2_ref[...], tr, preferred_element_type=F32)
              + jnp.dot(s2_ref[...], ti, preferred_element_type=F32) for tr, ti in ts]
        for r, y in zip(rows, ys):
            y_scr[r, :] = y
        return carry

    lax.fori_loop(0, s1 // DFT_GROUP, stage2, 0)
    o_ref[...] = jnp.dot(y_scr[...].astype(BF16), wf_ref[...],
                         preferred_element_type=F32).astype(o_ref.dtype)


def _fourier(z, o_prev, w_fourier, layer, seq, batch_lo, batch_n):
    n = z.shape[0]
    s1, s2 = _dft_factors(seq)
    w_chan, stage1, c2, sn2 = (jnp.asarray(t, BF16) for t in _dft_tables(seq))
    zb = z.reshape(n // seq, seq, IN_WIDTH)
    f_col = 3 * N_ATTN_HEADS
    in_specs = [
        pl.BlockSpec((None, seq, FOURIER_GROUP), lambda b, g: (batch_lo + b, 0, f_col + g)),
        pl.BlockSpec((FOURIER_GROUP, 2 * FOURIER_GROUP), lambda b, g: (0, 0)),
        pl.BlockSpec((s2, 2 * s1, s1), lambda b, g: (0, 0, 0)),
        pl.BlockSpec((s2, s2), lambda b, g: (0, 0)),
        pl.BlockSpec((s2, s2), lambda b, g: (0, 0)),
        pl.BlockSpec((None, None, FOURIER_GROUP, FOURIER_GROUP), lambda b, g: (layer, g, 0, 0)),
    ]
    args = [zb, w_chan, stage1, c2, sn2, w_fourier]
    aliases = {}
    if o_prev is not None:
        in_specs.append(pl.BlockSpec(memory_space=pl.ANY))
        args.append(o_prev.reshape(n // seq, seq, FOURIER_WIDTH))
        aliases = {6: 0}

    def kern(*refs):
        if o_prev is not None:
            refs = refs[:6] + refs[7:]
        _fourier_kernel(*refs, s1=s1, s2=s2)

    out = pl.pallas_call(
        kern,
        grid=(batch_n, N_FOURIER_GROUPS),
        in_specs=in_specs,
        out_specs=pl.BlockSpec((None, seq, FOURIER_GROUP), lambda b, g: (batch_lo + b, 0, g)),
        out_shape=jax.ShapeDtypeStruct((n // seq, seq, FOURIER_WIDTH), BF16),
        scratch_shapes=[pltpu.VMEM((seq, FOURIER_GROUP), F32)] * 5,
        input_output_aliases=aliases,
        compiler_params=_cparams(("parallel", "parallel")),
    )(*args)
    return out.reshape(n, FOURIER_WIDTH)


def _out_proj_kernel(oa_ref, of_ref, wa_ref, wf_ref, x_ref, g_ref, o_ref):
    y = jnp.dot(oa_ref[...], wa_ref[...], preferred_element_type=F32)
    y = y + jnp.dot(of_ref[...], wf_ref[...], preferred_element_type=F32)
    o_ref[...] = x_ref[...] + g_ref[...] * y


def _out_proj(o_attn, o_f, x, mod, w_out, layer, row_batch, tm=512, tn=1024):
    n = x.shape[0]
    fblk = ATTN_WIDTH // FOURIER_WIDTH

    def gate_index(i, j):
        return (row_batch(i * tm) * 6 + 2, 0, j)

    return pl.pallas_call(
        _out_proj_kernel,
        grid=(n // tm, D_MODEL // tn),
        in_specs=[
            pl.BlockSpec((tm, ATTN_WIDTH), lambda i, j: (i, 0)),
            pl.BlockSpec((tm, FOURIER_WIDTH), lambda i, j: (i, 0)),
            pl.BlockSpec((None, ATTN_WIDTH, tn), lambda i, j: (layer, 0, j)),
            pl.BlockSpec((None, FOURIER_WIDTH, tn), lambda i, j: (layer, fblk, j)),
            pl.BlockSpec((tm, tn), lambda i, j: (i, j)),
            pl.BlockSpec((None, 1, tn), gate_index),
        ],
        out_specs=pl.BlockSpec((tm, tn), lambda i, j: (i, j)),
        out_shape=jax.ShapeDtypeStruct((n, D_MODEL), F32),
        compiler_params=_cparams(("parallel", "parallel")),
    )(o_attn, o_f, w_out, w_out, x, mod)


def _ffn_dense_kernel(x_ref, g_ref, sh_ref, sc_ref, gate_ref, w1_ref, w3_ref, w2_ref, o_ref,
                      h_scr, acc):
    j = pl.program_id(1)

    @pl.when(j == 0)
    def _():
        h_scr[...] = _norm_mod(x_ref[...], g_ref[...], sh_ref[...], sc_ref[...]).astype(BF16)
        acc[...] = jnp.zeros_like(acc)

    h = h_scr[...]
    a = jnp.dot(h, w1_ref[...], preferred_element_type=F32)
    b = jnp.dot(h, w3_ref[...], preferred_element_type=F32)
    acc[...] += jnp.dot((_silu(a) * b).astype(BF16), w2_ref[...], preferred_element_type=F32)

    @pl.when(j == pl.num_programs(1) - 1)
    def _():
        o_ref[...] = x_ref[...] + gate_ref[...] * acc[...]


def _ffn_dense(x, gain, mod, w1, w3, w2, layer, idx, row_batch, tm=512, tf=512):
    n = x.shape[0]
    d_ff = w1.shape[-1]
    return pl.pallas_call(
        _ffn_dense_kernel,
        grid=(n // tm, d_ff // tf),
        in_specs=[
            pl.BlockSpec((tm, D_MODEL), lambda i, j: (i, 0)),
            pl.BlockSpec((None, 1, D_MODEL), lambda i, j: (layer, 0, 0)),
            _mod_spec(3, 6, row_batch, tm),
            _mod_spec(4, 6, row_batch, tm),
            _mod_spec(5, 6, row_batch, tm),
            pl.BlockSpec((None, D_MODEL, tf), lambda i, j: (idx, 0, j)),
            pl.BlockSpec((None, D_MODEL, tf), lambda i, j: (idx, 0, j)),
            pl.BlockSpec((None, tf, D_MODEL), lambda i, j: (idx, j, 0)),
        ],
        out_specs=pl.BlockSpec((tm, D_MODEL), lambda i, j: (i, 0)),
        out_shape=jax.ShapeDtypeStruct((n, D_MODEL), F32),
        scratch_shapes=[pltpu.VMEM((tm, D_MODEL), BF16), pltpu.VMEM((tm, D_MODEL), F32)],
        compiler_params=_cparams(("parallel", "arbitrary")),
    )(x, gain, mod, mod, mod, w1, w3, w2)


def _router_kernel(x_ref, g_ref, sh_ref, sc_ref, rw_ref, h_ref, idx_ref, gate_ref, rank_ref,
                   cnt_ref, cnt_scr, *, tm):
    @pl.when(pl.program_id(0) == 0)
    def _():
        cnt_scr[...] = jnp.zeros_like(cnt_scr)

    h = _norm_mod(x_ref[...], g_ref[...], sh_ref[...], sc_ref[...])
    h_ref[...] = h
    logits = jnp.dot(h, rw_ref[...], preferred_element_type=F32,
                     precision=lax.Precision.HIGHEST)
    lane = lax.broadcasted_iota(jnp.int32, (tm, LANES), 1).astype(F32)
    logits = jnp.where(lane < N_EXPERTS, logits, -jnp.inf)
    v1 = jnp.max(logits, axis=-1, keepdims=True)
    i1 = jnp.min(jnp.where(logits == v1, lane, float(LANES)), axis=-1, keepdims=True)
    rest = jnp.where(lane == i1, -jnp.inf, logits)
    v2 = jnp.max(rest, axis=-1, keepdims=True)
    i2 = jnp.min(jnp.where(rest == v2, lane, float(LANES)), axis=-1, keepdims=True)
    e2 = jnp.exp(v2 - v1)
    gate1 = 1.0 / (1.0 + e2)
    gate2 = e2 / (1.0 + e2)

    hot1 = (lane == i1).astype(BF16)
    hot2 = (lane == i2).astype(BF16)
    r_i = lax.broadcasted_iota(jnp.int32, (tm, tm), 0)
    c_i = lax.broadcasted_iota(jnp.int32, (tm, tm), 1)
    before = (c_i < r_i).astype(BF16)
    pre1 = jnp.dot(before, hot1, preferred_element_type=F32)
    pre2 = jnp.dot(before, hot2, preferred_element_type=F32)
    tot1 = jnp.sum(hot1.astype(F32), axis=0, keepdims=True)
    tot2 = jnp.sum(hot2.astype(F32), axis=0, keepdims=True)
    cnt = cnt_scr[...]
    rank1 = jnp.sum(jnp.where(lane == i1, pre1 + cnt, 0.0), axis=-1, keepdims=True)
    rank2 = jnp.sum(jnp.where(lane == i2, pre2 + cnt + tot1, 0.0), axis=-1, keepdims=True)
    cnt = cnt + tot1 + tot2
    cnt_scr[...] = cnt
    cnt_ref[...] = jnp.broadcast_to(cnt, cnt_ref.shape).astype(jnp.int32)

    idx_ref[...] = jnp.where(lane == 0, i1, jnp.where(lane == 1, i2, 0.0)).astype(jnp.int32)
    gate_ref[...] = jnp.where(lane == 0, gate1, jnp.where(lane == 1, gate2, 0.0))
    rank_ref[...] = jnp.where(lane == 0, rank1, jnp.where(lane == 1, rank2, 0.0)).astype(jnp.int32)


def _router(x, gain, mod, router_w, layer, row_batch, tm=512):
    n = x.shape[0]
    rw = jnp.zeros((D_MODEL, LANES), F32).at[:, :N_EXPERTS].set(router_w)
    tile = lambda i: (i, 0)
    return pl.pallas_call(
        functools.partial(_router_kernel, tm=tm),
        grid=(n // tm,),
        in_specs=[
            pl.BlockSpec((tm, D_MODEL), tile),
            pl.BlockSpec((None, 1, D_MODEL), lambda i: (layer, 0, 0)),
            _mod_spec(3, 6, row_batch, tm),
            _mod_spec(4, 6, row_batch, tm),
            pl.BlockSpec((D_MODEL, LANES), lambda i: (0, 0)),
        ],
        out_specs=[
            pl.BlockSpec((tm, D_MODEL), tile),
            pl.BlockSpec((tm, LANES), tile),
            pl.BlockSpec((tm, LANES), tile),
            pl.BlockSpec((tm, LANES), tile),
            pl.BlockSpec((8, LANES), lambda i: (0, 0)),
        ],
        out_shape=[
            jax.ShapeDtypeStruct((n, D_MODEL), F32),
            jax.ShapeDtypeStruct((n, LANES), jnp.int32),
            jax.ShapeDtypeStruct((n, LANES), F32),
            jax.ShapeDtypeStruct((n, LANES), jnp.int32),
            jax.ShapeDtypeStruct((8, LANES), jnp.int32),
        ],
        scratch_shapes=[pltpu.VMEM((1, LANES), F32)],
        compiler_params=_cparams(("arbitrary",)),
    )(x, gain, mod, mod, rw)


def _dispatch_kernel(pos_ref, h_ref, xs_in_ref, xs_ref, sem, *, tt):
    del xs_in_ref
    base = pl.program_id(0) * tt

    def copy(t, slot):
        return pltpu.make_async_copy(h_ref.at[pl.ds(t, 1), :],
                                     xs_ref.at[pl.ds(pos_ref[2 * (base + t) + slot], 1), :],
                                     sem)

    def issue(t, c):
        copy(t, 0).start()
        copy(t, 1).start()
        return c

    def drain(t, c):
        copy(t, 0).wait()
        copy(t, 1).wait()
        return c

    lax.fori_loop(0, tt, issue, 0)
    lax.fori_loop(0, tt, drain, 0)


def _dispatch(h, pos, xs_init, tt=256):
    n = h.shape[0]
    return pl.pallas_call(
        functools.partial(_dispatch_kernel, tt=tt),
        grid_spec=pltpu.PrefetchScalarGridSpec(
            num_scalar_prefetch=1,
            grid=(n // tt,),
            in_specs=[
                pl.BlockSpec((tt, D_MODEL), lambda i, pos: (i, 0)),
                pl.BlockSpec(memory_space=pl.ANY),
            ],
            out_specs=pl.BlockSpec(memory_space=pl.ANY),
            scratch_shapes=[pltpu.SemaphoreType.DMA(())],
        ),
        out_shape=jax.ShapeDtypeStruct(xs_init.shape, xs_init.dtype),
        input_output_aliases={2: 0},
        compiler_params=_cparams(("arbitrary",), has_side_effects=True),
    )(pos, h, xs_init)


def _moe_ffn_kernel(te_ref, nv_ref, x_ref, w1_ref, w3_ref, w2_ref, o_ref, h_scr, acc):
    i = pl.program_id(0)
    j = pl.program_id(1)

    @pl.when(i < nv_ref[0])
    def _():
        @pl.when(j == 0)
        def _():
            h_scr[...] = x_ref[...].astype(BF16)
            acc[...] = jnp.zeros_like(acc)

        h = h_scr[...]
        a = jnp.dot(h, w1_ref[...], preferred_element_type=F32)
        b = jnp.dot(h, w3_ref[...], preferred_element_type=F32)
        acc[...] += jnp.dot((_silu(a) * b).astype(BF16), w2_ref[...],
                            preferred_element_type=F32)

        @pl.when(j == pl.num_programs(1) - 1)
        def _():
            o_ref[...] = acc[...]


def _moe_ffn(xs, tile_expert, n_valid, w1, w3, w2, idx, tm, tf=512):
    rows = xs.shape[0]
    d_ff = w1.shape[-1]
    nj = d_ff // tf

    def row_index(i, j, te, nv):
        return (jnp.minimum(i, nv[0] - 1), 0)

    def up_index(i, j, te, nv):
        return (idx, te[i], 0, jnp.where(i < nv[0], j, nj - 1))

    def down_index(i, j, te, nv):
        return (idx, te[i], jnp.where(i < nv[0], j, nj - 1), 0)

    return pl.pallas_call(
        _moe_ffn_kernel,
        grid_spec=pltpu.PrefetchScalarGridSpec(
            num_scalar_prefetch=2,
            grid=(rows // tm, nj),
            in_specs=[
                pl.BlockSpec((tm, D_MODEL), row_index),
                pl.BlockSpec((None, None, D_MODEL, tf), up_index),
                pl.BlockSpec((None, None, D_MODEL, tf), up_index),
                pl.BlockSpec((None, None, tf, D_MODEL), down_index),
            ],
            out_specs=pl.BlockSpec((tm, D_MODEL), row_index),
            scratch_shapes=[pltpu.VMEM((tm, D_MODEL), BF16), pltpu.VMEM((tm, D_MODEL), F32)],
        ),
        out_shape=jax.ShapeDtypeStruct((rows, D_MODEL), F32),
        compiler_params=_cparams(("arbitrary", "arbitrary")),
    )(tile_expert, n_valid, xs, w1, w3, w2)


def _combine_kernel(pos_ref, ys_ref, x_ref, gate_ref, g2_ref, o_ref, buf, sem, *, tt):
    base = pl.program_id(0) * tt

    def copy(t, slot):
        return pltpu.make_async_copy(ys_ref.at[pl.ds(pos_ref[2 * (base + t) + slot], 1), :],
                                     buf.at[slot, pl.ds(t, 1), :], sem)

    def issue(t, c):
        copy(t, 0).start()
        copy(t, 1).start()
        return c

    def drain(t, c):
        copy(t, 0).wait()
        copy(t, 1).wait()
        return c

    lax.fori_loop(0, tt, issue, 0)
    lax.fori_loop(0, tt, drain, 0)
    gates = gate_ref[...]
    f = gates[:, 0:1] * buf[0] + gates[:, 1:2] * buf[1]
    o_ref[...] = x_ref[...] + g2_ref[...] * f


def _combine(ys, pos, x, gates, mod, row_batch, tt=256):
    n = x.shape[0]
    return pl.pallas_call(
        functools.partial(_combine_kernel, tt=tt),
        grid_spec=pltpu.PrefetchScalarGridSpec(
            num_scalar_prefetch=1,
            grid=(n // tt,),
            in_specs=[
                pl.BlockSpec(memory_space=pl.ANY),
                pl.BlockSpec((tt, D_MODEL), lambda i, pos: (i, 0)),
                pl.BlockSpec((tt, LANES), lambda i, pos: (i, 0)),
                _mod_spec(5, 6, row_batch, tt),
            ],
            out_specs=pl.BlockSpec((tt, D_MODEL), lambda i, pos: (i, 0)),
            scratch_shapes=[pltpu.VMEM((2, tt, D_MODEL), F32), pltpu.SemaphoreType.DMA(())],
        ),
        out_shape=jax.ShapeDtypeStruct((n, D_MODEL), F32),
        compiler_params=_cparams(("arbitrary",)),
    )(pos, ys, x, gates, mod)


def _moe_layer(x, gain, mod, router_w, w1, w3, w2, layer, idx, row_batch, tm=512):
    n = x.shape[0]
    h, top_idx, gates, rank, counts = _router(x, gain, mod, router_w[idx], layer, row_batch)
    counts = counts[0, :N_EXPERTS]
    padded = ((counts + tm - 1) // tm) * tm
    ends = jnp.cumsum(padded)
    starts = ends - padded
    experts = top_idx[:, :2]
    pos = (starts[experts] + rank[:, :2]).reshape(-1).astype(jnp.int32)
    n_tiles = (2 * n) // tm + N_EXPERTS
    tile_start = jnp.arange(n_tiles, dtype=jnp.int32) * tm
    tile_expert = jnp.minimum(jnp.sum(tile_start[:, None] >= ends[None, :], axis=1),
                              N_EXPERTS - 1).astype(jnp.int32)
    n_valid = (ends[-1:] // tm).astype(jnp.int32)
    xs = _dispatch(h, pos, jnp.zeros((n_tiles * tm, D_MODEL), F32))
    ys = _moe_ffn(xs, tile_expert, n_valid, w1, w3, w2, idx, tm)
    return _combine(ys, pos, x, gates, mod, row_batch)


def _final_kernel(x_ref, g_ref, sh_ref, sc_ref, o_ref):
    o_ref[...] = _norm_mod(x_ref[...], g_ref[...], sh_ref[...], sc_ref[...])


def _final(x, gain, fmod, row_lo, rows, row_batch, tm=512):
    off = row_lo // tm

    def mod_spec(which):
        return pl.BlockSpec((None, 1, D_MODEL),
                            lambda i: (row_batch((i + off) * tm) * 2 + which, 0, 0))

    return pl.pallas_call(
        _final_kernel,
        grid=(rows // tm,),
        in_specs=[
            pl.BlockSpec((tm, D_MODEL), lambda i: (i + off, 0)),
            pl.BlockSpec((1, D_MODEL), lambda i: (0, 0)),
            mod_spec(0),
            mod_spec(1),
        ],
        out_specs=pl.BlockSpec((tm, D_MODEL), lambda i: (i, 0)),
        out_shape=jax.ShapeDtypeStruct((rows, D_MODEL), F32),
        compiler_params=_cparams(("parallel",)),
    )(x, gain, fmod, fmod)


def kernel(x_prompt, x_sample, c_prompt, c_sample, w_mod, b_mod, norm_mix, w_in, w_fourier, w_out,
           norm_ffn, dense_w1, dense_w3, dense_w2, router_w, moe_w1, moe_w3, moe_w2,
           w_final_mod, b_final_mod, norm_final):
    batch, seq_p, d = x_prompt.shape
    batch_s, seq_s, _ = x_sample.shape
    n_p, n_s = batch * seq_p, batch_s * seq_s
    n = n_p + n_s
    depth = w_mod.shape[0]
    assert d == D_MODEL and batch_s == 1 and n_p % seq_s == 0 and batch + batch_s <= MOD_ROWS

    def row_batch(row):
        return jnp.minimum(row // seq_p, batch)

    x = jnp.concatenate([x_prompt.reshape(n_p, d), x_sample.reshape(n_s, d)], axis=0)
    c_all = jnp.zeros((MOD_ROWS, d), F32).at[:batch].set(c_prompt).at[batch:batch + 1].set(c_sample)

    mod = _modulation(c_all, w_mod, b_mod)
    fmod = _modulation(c_all, w_final_mod[None], b_final_mod[None])[0]
    fmod = fmod.reshape(MOD_ROWS * 2, 1, d)

    gain_mix = norm_mix.reshape(depth, 1, d)
    gain_ffn = norm_ffn.reshape(depth, 1, d)
    w_in_b = w_in.astype(BF16)
    w_out_b = w_out.astype(BF16)
    w_fourier_b = w_fourier.astype(BF16)
    dense_b = [w.astype(BF16) for w in (dense_w1, dense_w3, dense_w2)]
    moe_b = [w.astype(BF16) for w in (moe_w1, moe_w3, moe_w2)]
    rope_p = _rope_tables(seq_p)
    rope_s = _rope_tables(seq_s)

    for l in range(depth):
        mod_l = mod[l].reshape(MOD_ROWS * 6, 1, d)
        z = _in_proj(x, gain_mix, mod_l, w_in_b, l, row_batch)
        o_attn = _attention(z, None, *rope_p, seq_p, 0, batch)
        o_attn = _attention(z, o_attn, *rope_s, seq_s, n_p // seq_s, batch_s)
        o_f = _fourier(z, None, w_fourier_b, l, seq_p, 0, batch)
        o_f = _fourier(z, o_f, w_fourier_b, l, seq_s, n_p // seq_s, batch_s)
        x = _out_proj(o_attn, o_f, x, mod_l, w_out_b, l, row_batch)
        if l % 2 == 0:
            x = _ffn_dense(x, gain_ffn, mod_l, *dense_b, l, l // 2, row_batch)
        else:
            x = _moe_layer(x, gain_ffn, mod_l, router_w, *moe_b, l, l // 2, row_batch)

    y_p = _final(x, norm_final.reshape(1, d), fmod, 0, n_p, row_batch)
    y_s = _final(x, norm_final.reshape(1, d), fmod, n_p, n_s, row_batch)
    return (y_p.reshape(batch, seq_p, d), y_s.reshape(batch_s, seq_s, d))
```

```python
import functools

import numpy as np
import jax
import jax.numpy as jnp
from jax import lax
from jax.experimental import pallas as pl
from jax.experimental.pallas import tpu as pltpu

F32 = jnp.float32
BF16 = jnp.bfloat16

D_MODEL = 2048
HEAD_DIM = 128
N_ATTN_HEADS = 12
ATTN_WIDTH = N_ATTN_HEADS * HEAD_DIM
N_FOURIER_GROUPS = 4
FOURIER_GROUP = 128
FOURIER_WIDTH = N_FOURIER_GROUPS * FOURIER_GROUP
IN_WIDTH = 3 * ATTN_WIDTH + FOURIER_WIDTH
DILATIONS = (1, 4, 16)
HALF_NEIGHBOURS = 64
ROPE_THETA = 10000.0
N_EXPERTS = 8
EPS = 1e-6
NEG_INF = -1e30
LANES = 128
MOD_ROWS = 16

VMEM_LIMIT = 48 * 1024 * 1024
ATTN_VMEM_LIMIT = 58 * 1024 * 1024
BLOCK_GROUP = 8
DFT_GROUP = 8
DOWN_CHUNK = 512
LARGE_VMEM_LIMIT = 56 * 1024 * 1024


def _cparams(sem, vmem=VMEM_LIMIT, **kw):
    return pltpu.CompilerParams(dimension_semantics=sem, vmem_limit_bytes=vmem, **kw)


def _norm_mod(x, g, sh, sc):
    ms = jnp.mean(x * x, axis=-1, keepdims=True)
    return (x * lax.rsqrt(ms + EPS)) * g * (1.0 + sc) + sh


def _silu(a):
    return a / (1.0 + jnp.exp(-a))


def _mod_kernel(c_ref, w_ref, b_ref, o_ref):
    cs = _silu(c_ref[...]).astype(BF16)
    o_ref[...] = jnp.dot(cs, w_ref[...].astype(BF16), preferred_element_type=F32) + b_ref[...]


def _modulation(c_all, w, b, tn=1024):
    nl, d, n = w.shape
    return pl.pallas_call(
        _mod_kernel,
        grid=(nl, n // tn),
        in_specs=[
            pl.BlockSpec((MOD_ROWS, d), lambda l, j: (0, 0)),
            pl.BlockSpec((None, d, tn), lambda l, j: (l, 0, j)),
            pl.BlockSpec((None, 1, tn), lambda l, j: (l, 0, j)),
        ],
        out_specs=pl.BlockSpec((None, MOD_ROWS, tn), lambda l, j: (l, 0, j)),
        out_shape=jax.ShapeDtypeStruct((nl, MOD_ROWS, n), F32),
        compiler_params=_cparams(("parallel", "parallel")),
    )(c_all, w, b.reshape(nl, 1, n))


def _mod_spec(which, n_vec, row_batch, tm):
    def index(i, *_):
        return (row_batch(i * tm) * n_vec + which, 0, 0)
    return pl.BlockSpec((None, 1, D_MODEL), index)


def _token_specs(xs, tm, width, column_tiled=False):
    specs, lo = [], 0
    for x in xs:
        nt = x.shape[0] // tm

        def index(i, j=0, *_, lo=lo, nt=nt):
            return (jnp.clip(i - lo, 0, nt - 1), j if column_tiled else 0)

        specs.append(pl.BlockSpec((tm, width), index))
        lo += nt
    return specs


def _with_token_tile(x_refs, src_rows, tm, fn):
    if len(x_refs) == 1:
        fn(x_refs[0][...])
        return
    i, lo = pl.program_id(0), 0
    for x_ref, rows in zip(x_refs, src_rows):
        nt = rows // tm
        pl.when((i >= lo) & (i < lo + nt))(functools.partial(lambda r: fn(r[...]), x_ref))
        lo += nt


def _in_proj_kernel(*refs, src_rows, tm):
    ns = len(src_rows)
    g_ref, sh_ref, sc_ref, w_ref, o_ref, h_scr = refs[ns:]

    def prepare(x):
        h_scr[...] = _norm_mod(x, g_ref[...], sh_ref[...], sc_ref[...]).astype(BF16)

    @pl.when(pl.program_id(1) == 0)
    def _():
        _with_token_tile(refs[:ns], src_rows, tm, prepare)

    o_ref[...] = jnp.dot(h_scr[...], w_ref[...], preferred_element_type=F32).astype(o_ref.dtype)


def _in_proj(xs, gain, mod, w_in, layer, row_batch, tn=1280):
    tm = 1024 if len(xs) == 1 else 512
    n = sum(x.shape[0] for x in xs)
    src_rows = tuple(x.shape[0] for x in xs)
    return pl.pallas_call(
        functools.partial(_in_proj_kernel, src_rows=src_rows, tm=tm),
        grid=(n // tm, IN_WIDTH // tn),
        in_specs=_token_specs(xs, tm, D_MODEL) + [
            pl.BlockSpec((None, 1, D_MODEL), lambda i, j: (layer, 0, 0)),
            _mod_spec(0, 6, row_batch, tm),
            _mod_spec(1, 6, row_batch, tm),
            pl.BlockSpec((None, D_MODEL, tn), lambda i, j: (layer, 0, j)),
        ],
        out_specs=pl.BlockSpec((tm, tn), lambda i, j: (i, j)),
        out_shape=jax.ShapeDtypeStruct((n, IN_WIDTH), BF16),
        scratch_shapes=[pltpu.VMEM((tm, D_MODEL), BF16)],
        compiler_params=_cparams(("parallel", "arbitrary")),
    )(*xs, gain, mod, mod, w_in)


def _attn_kernel(q_ref, k_ref, v_ref, cos_ref, sin_ref, o_ref,
                 qr, kr, vr, acc, mx, den, bias, *, seq, bq):
    chunk = bq * DILATIONS[-1]
    for p, r in enumerate(DILATIONS):
        kw = min(bq + 2 * HALF_NEIGHBOURS, seq // r)
        rel = (lax.broadcasted_iota(jnp.int32, (bq, kw), 1)
               - lax.broadcasted_iota(jnp.int32, (bq, kw), 0))
        for case, off in enumerate((0, -HALF_NEIGHBOURS, bq - kw)):
            bias[p, case, :, :kw] = jnp.where(jnp.abs(rel + off) <= HALF_NEIGHBOURS, 0.0, NEG_INF)
    cos = cos_ref[...]
    sin = sin_ref[...]
    q = q_ref[...].astype(F32)
    qr[...] = (q * cos + pltpu.roll(q, HEAD_DIM // 2, 1) * sin) * (1.0 / np.sqrt(HEAD_DIM))
    k = k_ref[...].astype(F32)
    kr[...] = k * cos + pltpu.roll(k, HEAD_DIM // 2, 1) * sin
    vr[...] = v_ref[...].astype(F32)

    def do_chunk(c, carry):
        base = c * chunk
        for p, r in enumerate(DILATIONS):
            sub_len = seq // r
            kw = min(bq + 2 * HALF_NEIGHBOURS, sub_len)
            nb = chunk // (r * bq)

            def do_blocks(t, carry2, p=p, r=r, sub_len=sub_len, kw=kw, nb=nb):
                work = []
                for g in range(BLOCK_GROUP):
                    tg = t * BLOCK_GROUP + g
                    m = tg // nb
                    i = tg % nb
                    q0 = (c * nb + i) * bq
                    k0 = jnp.clip(q0 - HALF_NEIGHBOURS, 0, sub_len - kw)
                    qb = qr[pl.ds(m + r * q0, bq, stride=r), :].astype(BF16)
                    kb = kr[pl.ds(m + r * k0, kw, stride=r), :].astype(BF16)
                    vb = vr[pl.ds(m + r * k0, kw, stride=r), :].astype(BF16)
                    dst = pl.ds(m + r * (i * bq), bq, stride=r)
                    work.append((qb, kb, vb, k0 - q0, dst))
                scores = [lax.dot_general(qb, kb, (((1,), (1,)), ((), ())),
                                          preferred_element_type=F32)
                          for qb, kb, _, _, _ in work]
                weights = []
                for s, (_, _, _, off, dst) in zip(scores, work):
                    case = jnp.where(off == 0, 0, jnp.where(off == -HALF_NEIGHBOURS, 1, 2))
                    s = s + bias[p, case, :, :kw]
                    smax = jnp.max(s, axis=-1, keepdims=True)
                    e = jnp.exp(s - smax)
                    mx[p, dst, :] = jnp.broadcast_to(smax, (bq, HEAD_DIM))
                    den[p, dst, :] = jnp.broadcast_to(jnp.sum(e, axis=-1, keepdims=True),
                                                      (bq, HEAD_DIM))
                    weights.append(e.astype(BF16))
                for e, (_, _, vb, _, dst) in zip(weights, work):
                    acc[p, dst, :] = jnp.dot(e, vb, preferred_element_type=F32)
                return carry2

            lax.fori_loop(0, (r * nb) // BLOCK_GROUP, do_blocks, 0)

        top = jnp.maximum(jnp.maximum(mx[0], mx[1]), mx[2])
        num = jnp.zeros((chunk, HEAD_DIM), F32)
        tot = jnp.zeros((chunk, HEAD_DIM), F32)
        for p in range(len(DILATIONS)):
            w = jnp.exp(mx[p] - top)
            num = num + w * acc[p]
            tot = tot + w * den[p]
        o_ref[pl.ds(pl.multiple_of(base, chunk), chunk), :] = (num / tot).astype(o_ref.dtype)
        return carry

    lax.fori_loop(0, seq // chunk, do_chunk, 0)


def _attention(z, o_prev, rope_cos, rope_sin, seq, batch_lo, batch_n, bq=64):
    n = z.shape[0]
    chunk = bq * DILATIONS[-1]
    assert seq % chunk == 0 and n % seq == 0
    zb = z.reshape(n // seq, seq, IN_WIDTH)

    def col_spec(off):
        return pl.BlockSpec((None, seq, HEAD_DIM),
                            lambda b, h: (batch_lo + b, 0, off + h))

    in_specs = [
        col_spec(0), col_spec(N_ATTN_HEADS), col_spec(2 * N_ATTN_HEADS),
        pl.BlockSpec((seq, HEAD_DIM), lambda b, h: (0, 0), pipeline_mode=pl.Buffered(1)),
        pl.BlockSpec((seq, HEAD_DIM), lambda b, h: (0, 0), pipeline_mode=pl.Buffered(1)),
    ]
    args = [zb, zb, zb, rope_cos, rope_sin]
    aliases = {}
    if o_prev is not None:
        in_specs.append(pl.BlockSpec(memory_space=pl.ANY))
        args.append(o_prev.reshape(n // seq, seq, ATTN_WIDTH))
        aliases = {5: 0}

    def kern(*refs):
        if o_prev is not None:
            refs = refs[:5] + refs[6:]
        _attn_kernel(*refs, seq=seq, bq=bq)

    out = pl.pallas_call(
        kern,
        grid=(batch_n, N_ATTN_HEADS),
        in_specs=in_specs,
        out_specs=pl.BlockSpec((None, seq, HEAD_DIM), lambda b, h: (batch_lo + b, 0, h)),
        out_shape=jax.ShapeDtypeStruct((n // seq, seq, ATTN_WIDTH), BF16),
        scratch_shapes=[
            pltpu.VMEM((seq, HEAD_DIM), F32),
            pltpu.VMEM((seq, HEAD_DIM), F32),
            pltpu.VMEM((seq, HEAD_DIM), F32),
            pltpu.VMEM((len(DILATIONS), chunk, HEAD_DIM), F32),
            pltpu.VMEM((len(DILATIONS), chunk, HEAD_DIM), F32),
            pltpu.VMEM((len(DILATIONS), chunk, HEAD_DIM), F32),
            pltpu.VMEM((len(DILATIONS), 3, bq, bq + 2 * HALF_NEIGHBOURS), F32),
        ],
        input_output_aliases=aliases,
        compiler_params=_cparams(("parallel", "parallel"), vmem=ATTN_VMEM_LIMIT),
    )(*args)
    return out.reshape(n, ATTN_WIDTH)


def _rope_tables(seq):
    half = HEAD_DIM // 2
    inv = jnp.power(ROPE_THETA, -jnp.arange(half, dtype=F32) * 2.0 / HEAD_DIM)
    ang = jnp.arange(seq, dtype=F32)[:, None] * inv[None, :]
    cos, sin = jnp.cos(ang), jnp.sin(ang)
    return jnp.concatenate([cos, cos], axis=-1), jnp.concatenate([-sin, sin], axis=-1)


def _dft_factors(seq):
    s1 = {2048: 32, 8192: 64}.get(seq)
    if s1 is None:
        s1 = 1
        while s1 * s1 < seq:
            s1 *= 2
        s1 = seq // s1 if (seq // s1) * s1 == seq else s1
    return s1, seq // s1


@functools.lru_cache(maxsize=None)
def _dft_tables(seq):
    s1, s2 = _dft_factors(seq)
    c = np.arange(FOURIER_GROUP)
    ang_c = 2.0 * np.pi * ((c[:, None] * c[None, :]) % FOURIER_GROUP) / FOURIER_GROUP
    norm = 1.0 / np.sqrt(float(seq) * FOURIER_GROUP)
    w_chan = np.concatenate([np.cos(ang_c), -np.sin(ang_c)], axis=1) * norm
    k1 = np.arange(s1)[None, :, None]
    n1 = np.arange(s1)[None, None, :]
    n2 = np.arange(s2)[:, None, None]
    ang1 = 2.0 * np.pi * ((k1 * (n2 + s2 * n1)) % seq) / seq
    stage1 = np.concatenate([np.cos(ang1), np.sin(ang1)], axis=1)
    j = np.arange(s2)
    ang2 = 2.0 * np.pi * ((j[:, None] * j[None, :]) % s2) / s2
    return (np.asarray(w_chan, np.float32), np.asarray(stage1, np.float32),
            np.asarray(np.cos(ang2), np.float32), np.asarray(np.sin(ang2), np.float32))


def _fourier_kernel(f_ref, wc_ref, st1_ref, c2_ref, s2_ref, wf_ref, o_ref,
                    zr_scr, zi_scr, tr_scr, ti_scr, y_scr, *, s1, s2):
    g = FOURIER_GROUP
    z = jnp.dot(f_ref[...], wc_ref[...], preferred_element_type=F32)
    zr_scr[...] = z[:, :g]
    zi_scr[...] = z[:, g:]

    def stage1(t, carry):
        n2s = [t * DFT_GROUP + u for u in range(DFT_GROUP)]
        zs = []
        for n2 in n2s:
            rows = pl.ds(n2, s1, stride=s2)
            zs.append(jnp.concatenate([zr_scr[rows, :], zi_scr[rows, :]], axis=1).astype(BF16))
        prs = [jnp.dot(st1_ref[n2], z, preferred_element_type=F32)
               for n2, z in zip(n2s, zs)]
        for n2, pr in zip(n2s, prs):
            dst = pl.ds(pl.multiple_of(n2 * s1, s1), s1)
            tr_scr[dst, :] = pr[:s1, :g] + pr[s1:, g:]
            ti_scr[dst, :] = pr[:s1, g:] - pr[s1:, :g]
        return carry

    lax.fori_loop(0, s2 // DFT_GROUP, stage1, 0)

    def stage2(t, carry):
        rows = [pl.ds(t * DFT_GROUP + u, s2, stride=s1) for u in range(DFT_GROUP)]
        ts = [(tr_scr[r, :].astype(BF16), ti_scr[r, :].astype(BF16)) for r in rows]
        ys = [jnp.dot(c2_ref[...], tr, preferred_element_type=F32)
              + jnp.dot(s2_ref[...], ti, preferred_element_type=F32) for tr, ti in ts]
        for r, y in zip(rows, ys):
            y_scr[r, :] = y
        return carry

    lax.fori_loop(0, s1 // DFT_GROUP, stage2, 0)
    o_ref[...] = jnp.dot(y_scr[...].astype(BF16), wf_ref[...],
                         preferred_element_type=F32).astype(o_ref.dtype)


def _fourier(z, o_prev, w_fourier, layer, seq, batch_lo, batch_n):
    n = z.shape[0]
    s1, s2 = _dft_factors(seq)
    w_chan, stage1, c2, sn2 = (jnp.asarray(t, BF16) for t in _dft_tables(seq))
    zb = z.reshape(n // seq, seq, IN_WIDTH)
    f_col = 3 * N_ATTN_HEADS
    in_specs = [
        pl.BlockSpec((None, seq, FOURIER_GROUP), lambda b, g: (batch_lo + b, 0, f_col + g)),
        pl.BlockSpec((FOURIER_GROUP, 2 * FOURIER_GROUP), lambda b, g: (0, 0)),
        pl.BlockSpec((s2, 2 * s1, s1), lambda b, g: (0, 0, 0)),
        pl.BlockSpec((s2, s2), lambda b, g: (0, 0)),
        pl.BlockSpec((s2, s2), lambda b, g: (0, 0)),
        pl.BlockSpec((None, None, FOURIER_GROUP, FOURIER_GROUP), lambda b, g: (layer, g, 0, 0)),
    ]
    args = [zb, w_chan, stage1, c2, sn2, w_fourier]
    aliases = {}
    if o_prev is not None:
        in_specs.append(pl.BlockSpec(memory_space=pl.ANY))
        args.append(o_prev.reshape(n // seq, seq, FOURIER_WIDTH))
        aliases = {6: 0}

    def kern(*refs):
        if o_prev is not None:
            refs = refs[:6] + refs[7:]
        _fourier_kernel(*refs, s1=s1, s2=s2)

    out = pl.pallas_call(
        kern,
        grid=(batch_n, N_FOURIER_GROUPS),
        in_specs=in_specs,
        out_specs=pl.BlockSpec((None, seq, FOURIER_GROUP), lambda b, g: (batch_lo + b, 0, g)),
        out_shape=jax.ShapeDtypeStruct((n // seq, seq, FOURIER_WIDTH), BF16),
        scratch_shapes=[pltpu.VMEM((seq, FOURIER_GROUP), F32)] * 5,
        input_output_aliases=aliases,
        compiler_params=_cparams(("parallel", "parallel")),
    )(*args)
    return out.reshape(n, FOURIER_WIDTH)


def _out_proj_kernel(oa_ref, of_ref, wa_ref, wf_ref, g_ref, *refs, src_rows, tm):
    ns = len(src_rows)
    o_ref = refs[ns]
    y = jnp.dot(oa_ref[...], wa_ref[...], preferred_element_type=F32)
    y = y + jnp.dot(of_ref[...], wf_ref[...], preferred_element_type=F32)
    gated = g_ref[...] * y

    def residual(x):
        o_ref[...] = x + gated

    _with_token_tile(refs[:ns], src_rows, tm, residual)


def _out_proj(o_attn, o_f, xs, mod, w_out, layer, row_batch, tm=512, tn=1024):
    n = sum(x.shape[0] for x in xs)
    src_rows = tuple(x.shape[0] for x in xs)
    fblk = ATTN_WIDTH // FOURIER_WIDTH

    def gate_index(i, j):
        return (row_batch(i * tm) * 6 + 2, 0, j)

    return pl.pallas_call(
        functools.partial(_out_proj_kernel, src_rows=src_rows, tm=tm),
        grid=(n // tm, D_MODEL // tn),
        in_specs=[
            pl.BlockSpec((tm, ATTN_WIDTH), lambda i, j: (i, 0)),
            pl.BlockSpec((tm, FOURIER_WIDTH), lambda i, j: (i, 0)),
            pl.BlockSpec((None, ATTN_WIDTH, tn), lambda i, j: (layer, 0, j)),
            pl.BlockSpec((None, FOURIER_WIDTH, tn), lambda i, j: (layer, fblk, j)),
            pl.BlockSpec((None, 1, tn), gate_index),
        ] + _token_specs(xs, tm, tn, column_tiled=True),
        out_specs=pl.BlockSpec((tm, tn), lambda i, j: (i, j)),
        out_shape=jax.ShapeDtypeStruct((n, D_MODEL), F32),
        compiler_params=_cparams(("parallel", "parallel")),
    )(o_attn, o_f, w_out, w_out, mod, *xs)


def _swiglu_accumulate(h_scr, w1_ref, w3_ref, w2_ref, o_ref):
    h = h_scr[...]
    a = jnp.dot(h, w1_ref[...], preferred_element_type=F32)
    b = jnp.dot(h, w3_ref[...], preferred_element_type=F32)
    g = (_silu(a) * b).astype(BF16)
    for c in range(0, D_MODEL, DOWN_CHUNK):
        o_ref[:, c:c + DOWN_CHUNK] += jnp.dot(g, w2_ref[:, c:c + DOWN_CHUNK],
                                              preferred_element_type=F32)


def _ffn_dense_kernel(x_ref, g_ref, sh_ref, sc_ref, gate_ref, w1_ref, w3_ref, w2_ref, o_ref,
                      h_scr):
    j = pl.program_id(1)

    @pl.when(j == 0)
    def _():
        h_scr[...] = _norm_mod(x_ref[...], g_ref[...], sh_ref[...], sc_ref[...]).astype(BF16)
        o_ref[...] = jnp.zeros_like(o_ref)

    _swiglu_accumulate(h_scr, w1_ref, w3_ref, w2_ref, o_ref)

    @pl.when(j == pl.num_programs(1) - 1)
    def _():
        o_ref[...] = x_ref[...] + gate_ref[...] * o_ref[...]


def _ffn_dense(x, gain, mod, w1, w3, w2, layer, idx, row_batch, tm=512, tf=512):
    n = x.shape[0]
    d_ff = w1.shape[-1]
    return pl.pallas_call(
        _ffn_dense_kernel,
        grid=(n // tm, d_ff // tf),
        in_specs=[
            pl.BlockSpec((tm, D_MODEL), lambda i, j: (i, 0)),
            pl.BlockSpec((None, 1, D_MODEL), lambda i, j: (layer, 0, 0)),
            _mod_spec(3, 6, row_batch, tm),
            _mod_spec(4, 6, row_batch, tm),
            _mod_spec(5, 6, row_batch, tm),
            pl.BlockSpec((None, D_MODEL, tf), lambda i, j: (idx, 0, j)),
            pl.BlockSpec((None, D_MODEL, tf), lambda i, j: (idx, 0, j)),
            pl.BlockSpec((None, tf, D_MODEL), lambda i, j: (idx, j, 0)),
        ],
        out_specs=pl.BlockSpec((tm, D_MODEL), lambda i, j: (i, 0)),
        out_shape=jax.ShapeDtypeStruct((n, D_MODEL), F32),
        scratch_shapes=[pltpu.VMEM((tm, D_MODEL), BF16)],
        compiler_params=_cparams(("parallel", "arbitrary")),
    )(x, gain, mod, mod, mod, w1, w3, w2)


def _router_kernel(x_ref, g_ref, sh_ref, sc_ref, rw_ref, h_ref, idx_ref, gate_ref, rank_ref,
                   cnt_ref, cnt_scr, *, tm):
    @pl.when(pl.program_id(0) == 0)
    def _():
        cnt_scr[...] = jnp.zeros_like(cnt_scr)

    h = _norm_mod(x_ref[...], g_ref[...], sh_ref[...], sc_ref[...])
    h_ref[...] = h
    logits = jnp.dot(h, rw_ref[...], preferred_element_type=F32,
                     precision=lax.Precision.HIGHEST)
    lane = lax.broadcasted_iota(jnp.int32, (tm, LANES), 1).astype(F32)
    logits = jnp.where(lane < N_EXPERTS, logits, -jnp.inf)
    v1 = jnp.max(logits, axis=-1, keepdims=True)
    i1 = jnp.min(jnp.where(logits == v1, lane, float(LANES)), axis=-1, keepdims=True)
    rest = jnp.where(lane == i1, -jnp.inf, logits)
    v2 = jnp.max(rest, axis=-1, keepdims=True)
    i2 = jnp.min(jnp.where(rest == v2, lane, float(LANES)), axis=-1, keepdims=True)
    e2 = jnp.exp(v2 - v1)
    gate1 = 1.0 / (1.0 + e2)
    gate2 = e2 / (1.0 + e2)

    hot1 = (lane == i1).astype(BF16)
    hot2 = (lane == i2).astype(BF16)
    r_i = lax.broadcasted_iota(jnp.int32, (tm, tm), 0)
    c_i = lax.broadcasted_iota(jnp.int32, (tm, tm), 1)
    before = (c_i < r_i).astype(BF16)
    pre1 = jnp.dot(before, hot1, preferred_element_type=F32)
    pre2 = jnp.dot(before, hot2, preferred_element_type=F32)
    tot1 = jnp.sum(hot1.astype(F32), axis=0, keepdims=True)
    tot2 = jnp.sum(hot2.astype(F32), axis=0, keepdims=True)
    cnt = cnt_scr[...]
    rank1 = jnp.sum(jnp.where(lane == i1, pre1 + cnt, 0.0), axis=-1, keepdims=True)
    rank2 = jnp.sum(jnp.where(lane == i2, pre2 + cnt + tot1, 0.0), axis=-1, keepdims=True)
    cnt = cnt + tot1 + tot2
    cnt_scr[...] = cnt
    cnt_ref[...] = jnp.broadcast_to(cnt, cnt_ref.shape).astype(jnp.int32)

    idx_ref[...] = jnp.where(lane == 0, i1, jnp.where(lane == 1, i2, 0.0)).astype(jnp.int32)
    gate_ref[...] = jnp.where(lane == 0, gate1, jnp.where(lane == 1, gate2, 0.0))
    rank_ref[...] = jnp.where(lane == 0, rank1, jnp.where(lane == 1, rank2, 0.0)).astype(jnp.int32)


def _router(x, gain, mod, router_w, layer, row_batch, tm=512):
    n = x.shape[0]
    rw = jnp.zeros((D_MODEL, LANES), F32).at[:, :N_EXPERTS].set(router_w)
    tile = lambda i: (i, 0)
    return pl.pallas_call(
        functools.partial(_router_kernel, tm=tm),
        grid=(n // tm,),
        in_specs=[
            pl.BlockSpec((tm, D_MODEL), tile),
            pl.BlockSpec((None, 1, D_MODEL), lambda i: (layer, 0, 0)),
            _mod_spec(3, 6, row_batch, tm),
            _mod_spec(4, 6, row_batch, tm),
            pl.BlockSpec((D_MODEL, LANES), lambda i: (0, 0)),
        ],
        out_specs=[
            pl.BlockSpec((tm, D_MODEL), tile),
            pl.BlockSpec((tm, LANES), tile),
            pl.BlockSpec((tm, LANES), tile),
            pl.BlockSpec((tm, LANES), tile),
            pl.BlockSpec((8, LANES), lambda i: (0, 0)),
        ],
        out_shape=[
            jax.ShapeDtypeStruct((n, D_MODEL), F32),
            jax.ShapeDtypeStruct((n, LANES), jnp.int32),
            jax.ShapeDtypeStruct((n, LANES), F32),
            jax.ShapeDtypeStruct((n, LANES), jnp.int32),
            jax.ShapeDtypeStruct((8, LANES), jnp.int32),
        ],
        scratch_shapes=[pltpu.VMEM((1, LANES), F32)],
        compiler_params=_cparams(("arbitrary",)),
    )(x, gain, mod, mod, rw)


def _dispatch_kernel(pos_ref, h_ref, xs_in_ref, xs_ref, sem, *, tt):
    del xs_in_ref
    base = pl.program_id(0) * tt

    def copy(t, slot):
        return pltpu.make_async_copy(h_ref.at[pl.ds(t, 1), :],
                                     xs_ref.at[pl.ds(pos_ref[2 * (base + t) + slot], 1), :],
                                     sem)

    def issue(t, c):
        copy(t, 0).start()
        copy(t, 1).start()
        return c

    def drain(t, c):
        copy(t, 0).wait()
        copy(t, 1).wait()
        return c

    lax.fori_loop(0, tt, issue, 0)
    lax.fori_loop(0, tt, drain, 0)


def _dispatch(h, pos, xs_init, tt=256):
    n = h.shape[0]
    return pl.pallas_call(
        functools.partial(_dispatch_kernel, tt=tt),
        grid_spec=pltpu.PrefetchScalarGridSpec(
            num_scalar_prefetch=1,
            grid=(n // tt,),
            in_specs=[
                pl.BlockSpec((tt, D_MODEL), lambda i, pos: (i, 0)),
                pl.BlockSpec(memory_space=pl.ANY),
            ],
            out_specs=pl.BlockSpec(memory_space=pl.ANY),
            scratch_shapes=[pltpu.SemaphoreType.DMA(())],
        ),
        out_shape=jax.ShapeDtypeStruct(xs_init.shape, xs_init.dtype),
        input_output_aliases={2: 0},
        compiler_params=_cparams(("arbitrary",), has_side_effects=True),
    )(pos, h, xs_init)


def _moe_ffn_kernel(te_ref, nv_ref, x_ref, w1_ref, w3_ref, w2_ref, o_ref, h_scr):
    i = pl.program_id(0)
    j = pl.program_id(1)

    @pl.when(i < nv_ref[0])
    def _():
        @pl.when(j == 0)
        def _():
            h_scr[...] = x_ref[...].astype(BF16)
            o_ref[...] = jnp.zeros_like(o_ref)

        _swiglu_accumulate(h_scr, w1_ref, w3_ref, w2_ref, o_ref)


def _moe_ffn(xs, tile_expert, n_valid, w1, w3, w2, idx, tm, tf=1024):
    rows = xs.shape[0]
    d_ff = w1.shape[-1]
    nj = d_ff // tf

    def row_index(i, j, te, nv):
        return (jnp.maximum(jnp.minimum(i, nv[0] - 1), 0), 0)

    def up_index(i, j, te, nv):
        return (idx, te[i], 0, jnp.where(i < nv[0], j, nj - 1))

    def down_index(i, j, te, nv):
        return (idx, te[i], jnp.where(i < nv[0], j, nj - 1), 0)

    return pl.pallas_call(
        _moe_ffn_kernel,
        grid_spec=pltpu.PrefetchScalarGridSpec(
            num_scalar_prefetch=2,
            grid=(rows // tm, nj),
            in_specs=[
                pl.BlockSpec((tm, D_MODEL), row_index),
                pl.BlockSpec((None, None, D_MODEL, tf), up_index),
                pl.BlockSpec((None, None, D_MODEL, tf), up_index),
                pl.BlockSpec((None, None, tf, D_MODEL), down_index),
            ],
            out_specs=pl.BlockSpec((tm, D_MODEL), row_index),
            scratch_shapes=[pltpu.VMEM((tm, D_MODEL), BF16)],
        ),
        out_shape=jax.ShapeDtypeStruct((rows, D_MODEL), F32),
        compiler_params=_cparams(("arbitrary", "arbitrary"), vmem=LARGE_VMEM_LIMIT),
    )(tile_expert, n_valid, xs, w1, w3, w2)


def _combine_kernel(pos_ref, ys_ref, x_ref, gate_ref, g2_ref, o_ref, buf, sem, *, tt):
    base = pl.program_id(0) * tt

    def copy(t, slot):
        return pltpu.make_async_copy(ys_ref.at[pl.ds(pos_ref[2 * (base + t) + slot], 1), :],
                                     buf.at[slot, pl.ds(t, 1), :], sem)

    def issue(t, c):
        copy(t, 0).start()
        copy(t, 1).start()
        return c

    def drain(t, c):
        copy(t, 0).wait()
        copy(t, 1).wait()
        return c

    lax.fori_loop(0, tt, issue, 0)
    lax.fori_loop(0, tt, drain, 0)
    gates = gate_ref[...]
    f = gates[:, 0:1] * buf[0] + gates[:, 1:2] * buf[1]
    o_ref[...] = x_ref[...] + g2_ref[...] * f


def _combine_final_kernel(pos_ref, ys_ref, x_ref, gate_ref, g2_ref, gf_ref, shf_ref, scf_ref,
                          *refs, out_rows, tt):
    out_refs, (x_new, buf, sem) = refs[:len(out_rows)], refs[len(out_rows):]
    _combine_kernel(pos_ref, ys_ref, x_ref, gate_ref, g2_ref, x_new, buf, sem, tt=tt)
    y = _norm_mod(x_new[...], gf_ref[...], shf_ref[...], scf_ref[...])
    i, lo = pl.program_id(0), 0
    for o_ref, rows in zip(out_refs, out_rows):
        nt = rows // tt

        @pl.when((i >= lo) & (i < lo + nt))
        def _(o_ref=o_ref):
            o_ref[...] = y

        lo += nt


def _combine(ys, pos, x, gates, mod, row_batch, final=None, tt=256):
    n = x.shape[0]
    in_specs = [
        pl.BlockSpec(memory_space=pl.ANY),
        pl.BlockSpec((tt, D_MODEL), lambda i, pos: (i, 0)),
        pl.BlockSpec((tt, LANES), lambda i, pos: (i, 0)),
        _mod_spec(5, 6, row_batch, tt),
    ]
    args = [pos, ys, x, gates, mod]
    scratch = [pltpu.VMEM((2, tt, D_MODEL), F32), pltpu.SemaphoreType.DMA(())]
    if final is None:
        body = functools.partial(_combine_kernel, tt=tt)
        out_specs = pl.BlockSpec((tt, D_MODEL), lambda i, pos: (i, 0))
        out_shape = jax.ShapeDtypeStruct((n, D_MODEL), F32)
    else:
        gain, fmod, out_rows = final
        body = functools.partial(_combine_final_kernel, out_rows=out_rows, tt=tt)
        in_specs += [pl.BlockSpec((1, D_MODEL), lambda i, pos: (0, 0)),
                     _mod_spec(0, 2, row_batch, tt), _mod_spec(1, 2, row_batch, tt)]
        args += [gain, fmod, fmod]
        out_specs, out_shape, lo = [], [], 0
        for rows in out_rows:
            nt = rows // tt

            def index(i, pos, lo=lo, nt=nt):
                return (jnp.clip(i - lo, 0, nt - 1), 0)

            out_specs.append(pl.BlockSpec((tt, D_MODEL), index))
            out_shape.append(jax.ShapeDtypeStruct((rows, D_MODEL), F32))
            lo += nt
        scratch = [pltpu.VMEM((tt, D_MODEL), F32)] + scratch
    return pl.pallas_call(
        body,
        grid_spec=pltpu.PrefetchScalarGridSpec(
            num_scalar_prefetch=1, grid=(n // tt,), in_specs=in_specs, out_specs=out_specs,
            scratch_shapes=scratch),
        out_shape=out_shape,
        compiler_params=_cparams(("arbitrary",)),
    )(*args)


def _moe_layer(x, gain, mod, router_w, w1, w3, w2, layer, idx, row_batch, final=None, tm=512):
    n = x.shape[0]
    h, top_idx, gates, rank, counts = _router(x, gain, mod, router_w[idx], layer, row_batch)
    counts = counts[0, :N_EXPERTS]
    padded = ((counts + tm - 1) // tm) * tm
    ends = jnp.cumsum(padded)
    starts = ends - padded
    experts = top_idx[:, :2]
    pos = (starts[experts] + rank[:, :2]).reshape(-1).astype(jnp.int32)
    n_tiles = (2 * n) // tm + N_EXPERTS
    tile_start = jnp.arange(n_tiles, dtype=jnp.int32) * tm
    tile_expert = jnp.minimum(jnp.sum(tile_start[:, None] >= ends[None, :], axis=1),
                              N_EXPERTS - 1).astype(jnp.int32)
    n_valid = (ends[-1:] // tm).astype(jnp.int32)
    xs = _dispatch(h, pos, jnp.zeros((n_tiles * tm, D_MODEL), F32))
    ys = _moe_ffn(xs, tile_expert, n_valid, w1, w3, w2, idx, tm)
    return _combine(ys, pos, x, gates, mod, row_batch, final)


def _final_kernel(x_ref, g_ref, sh_ref, sc_ref, o_ref):
    o_ref[...] = _norm_mod(x_ref[...], g_ref[...], sh_ref[...], sc_ref[...])


def _final(x, gain, fmod, row_lo, rows, row_batch, tm=512):
    off = row_lo // tm

    def mod_spec(which):
        return pl.BlockSpec((None, 1, D_MODEL),
                            lambda i: (row_batch((i + off) * tm) * 2 + which, 0, 0))

    return pl.pallas_call(
        _final_kernel,
        grid=(rows // tm,),
        in_specs=[
            pl.BlockSpec((tm, D_MODEL), lambda i: (i + off, 0)),
            pl.BlockSpec((1, D_MODEL), lambda i: (0, 0)),
            mod_spec(0),
            mod_spec(1),
        ],
        out_specs=pl.BlockSpec((tm, D_MODEL), lambda i: (i, 0)),
        out_shape=jax.ShapeDtypeStruct((rows, D_MODEL), F32),
        compiler_params=_cparams(("parallel",)),
    )(x, gain, fmod, fmod)


def kernel(x_prompt, x_sample, c_prompt, c_sample, w_mod, b_mod, norm_mix, w_in, w_fourier, w_out,
           norm_ffn, dense_w1, dense_w3, dense_w2, router_w, moe_w1, moe_w3, moe_w2,
           w_final_mod, b_final_mod, norm_final):
    batch, seq_p, d = x_prompt.shape
    batch_s, seq_s, _ = x_sample.shape
    n_p, n_s = batch * seq_p, batch_s * seq_s
    n = n_p + n_s
    depth = w_mod.shape[0]
    assert d == D_MODEL and batch_s == 1 and n_p % seq_s == 0 and batch + batch_s <= MOD_ROWS

    def row_batch(row):
        return jnp.minimum(row // seq_p, batch)

    x = (x_prompt.reshape(n_p, d), x_sample.reshape(n_s, d))
    c_all = jnp.zeros((MOD_ROWS, d), F32).at[:batch].set(c_prompt).at[batch:batch + 1].set(c_sample)

    mod = _modulation(c_all, w_mod, b_mod)
    fmod = _modulation(c_all, w_final_mod[None], b_final_mod[None])[0]
    fmod = fmod.reshape(MOD_ROWS * 2, 1, d)

    gain_mix = norm_mix.reshape(depth, 1, d)
    gain_ffn = norm_ffn.reshape(depth, 1, d)
    gain_final = norm_final.reshape(1, d)
    w_in_b = w_in.astype(BF16)
    w_out_b = w_out.astype(BF16)
    w_fourier_b = w_fourier.astype(BF16)
    dense_b = [w.astype(BF16) for w in (dense_w1, dense_w3, dense_w2)]
    moe_b = [w.astype(BF16) for w in (moe_w1, moe_w3, moe_w2)]
    rope_p = _rope_tables(seq_p)
    rope_s = _rope_tables(seq_s)

    for l in range(depth):
        mod_l = mod[l].reshape(MOD_ROWS * 6, 1, d)
        z = _in_proj(x, gain_mix, mod_l, w_in_b, l, row_batch)
        o_attn = _attention(z, None, *rope_p, seq_p, 0, batch)
        o_attn = _attention(z, o_attn, *rope_s, seq_s, n_p // seq_s, batch_s)
        o_f = _fourier(z, None, w_fourier_b, l, seq_p, 0, batch)
        o_f = _fourier(z, o_f, w_fourier_b, l, seq_s, n_p // seq_s, batch_s)
        x = _out_proj(o_attn, o_f, x, mod_l, w_out_b, l, row_batch)
        if l % 2 == 0:
            x = (_ffn_dense(x, gain_ffn, mod_l, *dense_b, l, l // 2, row_batch),)
        else:
            final = (gain_final, fmod, (n_p, n_s)) if l == depth - 1 else None
            x = _moe_layer(x, gain_ffn, mod_l, router_w, *moe_b, l, l // 2, row_batch, final)
            x = tuple(x) if final else (x,)

    if depth % 2 == 0:
        y_p, y_s = x
    else:
        y_p = _final(x[0], gain_final, fmod, 0, n_p, row_batch)
        y_s = _final(x[0], gain_final, fmod, n_p, n_s, row_batch)
    return (y_p.reshape(batch, seq_p, d), y_s.reshape(batch_s, seq_s, d))
```

```python
import functools

import numpy as np
import jax
import jax.numpy as jnp
from jax import lax
from jax.experimental import pallas as pl
from jax.experimental.pallas import tpu as pltpu

F32 = jnp.float32
BF16 = jnp.bfloat16

D_MODEL = 2048
HEAD_DIM = 128
N_ATTN_HEADS = 12
ATTN_WIDTH = N_ATTN_HEADS * HEAD_DIM
N_FOURIER_GROUPS = 4
FOURIER_GROUP = 128
FOURIER_WIDTH = N_FOURIER_GROUPS * FOURIER_GROUP
IN_WIDTH = 3 * ATTN_WIDTH + FOURIER_WIDTH
DILATIONS = (1, 4, 16)
HALF_NEIGHBOURS = 64
ROPE_THETA = 10000.0
N_EXPERTS = 8
EPS = 1e-6
NEG_INF = -1e30
LANES = 128
SUBLANES = 8
MOD_ROWS = 16

VMEM_LIMIT = 48 * 1024 * 1024
ATTN_VMEM_LIMIT = 58 * 1024 * 1024
BLOCK_GROUP = 4
DFT_GROUP = 8
NORM_ROWS = 16
ROW_DMA_UNROLL = 8
DOWN_CHUNK = 512
LARGE_VMEM_LIMIT = 56 * 1024 * 1024


def _cparams(sem, vmem=VMEM_LIMIT, **kw):
    return pltpu.CompilerParams(dimension_semantics=sem, vmem_limit_bytes=vmem, **kw)


def _norm_mod(x, g, sh, sc):
    ms = jnp.mean(x * x, axis=-1, keepdims=True)
    return (x * lax.rsqrt(ms + EPS)) * g * (1.0 + sc) + sh


def _norm_mod_rows(x_ref, g_ref, sh_ref, sc_ref, out_ref):
    g, sh, sc = g_ref[...], sh_ref[...], sc_ref[...]

    def body(c, carry):
        rows = pl.ds(pl.multiple_of(c * NORM_ROWS, NORM_ROWS), NORM_ROWS)
        out_ref[rows, :] = _norm_mod(x_ref[rows, :], g, sh, sc).astype(out_ref.dtype)
        return carry

    lax.fori_loop(0, x_ref.shape[0] // NORM_ROWS, body, 0, unroll=8)


def _silu(a):
    return a / (1.0 + jnp.exp(-a))


def _mod_kernel(c_ref, w_ref, b_ref, o_ref):
    cs = _silu(c_ref[...]).astype(BF16)
    o_ref[...] = jnp.dot(cs, w_ref[...].astype(BF16), preferred_element_type=F32) + b_ref[...]


def _modulation(c_all, w, b, tn=1024):
    nl, d, n = w.shape
    return pl.pallas_call(
        _mod_kernel,
        grid=(nl, n // tn),
        in_specs=[
            pl.BlockSpec((MOD_ROWS, d), lambda l, j: (0, 0)),
            pl.BlockSpec((None, d, tn), lambda l, j: (l, 0, j)),
            pl.BlockSpec((None, 1, tn), lambda l, j: (l, 0, j)),
        ],
        out_specs=pl.BlockSpec((None, MOD_ROWS, tn), lambda l, j: (l, 0, j)),
        out_shape=jax.ShapeDtypeStruct((nl, MOD_ROWS, n), F32),
        compiler_params=_cparams(("parallel", "parallel")),
    )(c_all, w, b.reshape(nl, 1, n))


def _mod_spec(which, n_vec, row_batch, tm):
    def index(i, *_):
        return (row_batch(i * tm) * n_vec + which, 0, 0)
    return pl.BlockSpec((None, 1, D_MODEL), index)


def _token_specs(xs, tm, width, column_tiled=False):
    specs, lo = [], 0
    for x in xs:
        nt = x.shape[0] // tm

        def index(i, j=0, *_, lo=lo, nt=nt):
            return (jnp.clip(i - lo, 0, nt - 1), j if column_tiled else 0)

        specs.append(pl.BlockSpec((tm, width), index))
        lo += nt
    return specs


def _with_token_tile(x_refs, src_rows, tm, fn):
    if len(x_refs) == 1:
        fn(x_refs[0])
        return
    i, lo = pl.program_id(0), 0
    for x_ref, rows in zip(x_refs, src_rows):
        nt = rows // tm
        pl.when((i >= lo) & (i < lo + nt))(functools.partial(fn, x_ref))
        lo += nt


def _in_proj_kernel(*refs, src_rows, tm):
    ns = len(src_rows)
    g_ref, sh_ref, sc_ref, w_ref, o_ref, h_scr = refs[ns:]

    def prepare(x_ref):
        _norm_mod_rows(x_ref, g_ref, sh_ref, sc_ref, h_scr)

    @pl.when(pl.program_id(1) == 0)
    def _():
        _with_token_tile(refs[:ns], src_rows, tm, prepare)

    o_ref[...] = jnp.dot(h_scr[...], w_ref[...], preferred_element_type=F32).astype(o_ref.dtype)


def _in_proj(xs, gain, mod, w_in, layer, row_batch, tn=1280):
    tm = 1024 if len(xs) == 1 else 512
    n = sum(x.shape[0] for x in xs)
    src_rows = tuple(x.shape[0] for x in xs)
    return pl.pallas_call(
        functools.partial(_in_proj_kernel, src_rows=src_rows, tm=tm),
        grid=(n // tm, IN_WIDTH // tn),
        in_specs=_token_specs(xs, tm, D_MODEL) + [
            pl.BlockSpec((None, 1, D_MODEL), lambda i, j: (layer, 0, 0)),
            _mod_spec(0, 6, row_batch, tm),
            _mod_spec(1, 6, row_batch, tm),
            pl.BlockSpec((None, D_MODEL, tn), lambda i, j: (layer, 0, j)),
        ],
        out_specs=pl.BlockSpec((tm, tn), lambda i, j: (i, j)),
        out_shape=jax.ShapeDtypeStruct((n, IN_WIDTH), BF16),
        scratch_shapes=[pltpu.VMEM((tm, D_MODEL), BF16)],
        compiler_params=_cparams(("parallel", "arbitrary")),
    )(*xs, gain, mod, mod, w_in)


def _attn_kernel(q_ref, k_ref, v_ref, cos_ref, sin_ref, o_ref,
                 qr, kr, vr, acc, mx, den, bias, *, seq, bq):
    chunk = bq * DILATIONS[-1]
    for p, r in enumerate(DILATIONS):
        kw = min(bq + 2 * HALF_NEIGHBOURS, seq // r)
        rel = (lax.broadcasted_iota(jnp.int32, (bq, kw), 1)
               - lax.broadcasted_iota(jnp.int32, (bq, kw), 0))
        for case, off in enumerate((0, -HALF_NEIGHBOURS, bq - kw)):
            bias[p, case, :, :kw] = jnp.where(jnp.abs(rel + off) <= HALF_NEIGHBOURS, 0.0, NEG_INF)
    cos = cos_ref[...]
    sin = sin_ref[...]
    q = q_ref[...].astype(F32)
    qr[...] = (q * cos + pltpu.roll(q, HEAD_DIM // 2, 1) * sin) * (1.0 / np.sqrt(HEAD_DIM))
    k = k_ref[...].astype(F32)
    kr[...] = k * cos + pltpu.roll(k, HEAD_DIM // 2, 1) * sin
    vr[...] = v_ref[...].astype(F32)

    def do_chunk(c, carry):
        base = c * chunk
        for p, r in enumerate(DILATIONS):
            sub_len = seq // r
            kw = min(bq + 2 * HALF_NEIGHBOURS, sub_len)
            nb = chunk // (r * bq)

            def do_blocks(t, carry2, p=p, r=r, sub_len=sub_len, kw=kw, nb=nb):
                work = []
                for g in range(BLOCK_GROUP):
                    tg = t * BLOCK_GROUP + g
                    m = tg // nb
                    i = tg % nb
                    q0 = (c * nb + i) * bq
                    k0 = jnp.clip(q0 - HALF_NEIGHBOURS, 0, sub_len - kw)
                    qb = qr[pl.ds(m + r * q0, bq, stride=r), :].astype(BF16)
                    kb = kr[pl.ds(m + r * k0, kw, stride=r), :].astype(BF16)
                    vb = vr[pl.ds(m + r * k0, kw, stride=r), :].astype(BF16)
                    dst = pl.ds(m + r * (i * bq), bq, stride=r)
                    work.append((qb, kb, vb, k0 - q0, dst))
                scores = [lax.dot_general(qb, kb, (((1,), (1,)), ((), ())),
                                          preferred_element_type=F32)
                          for qb, kb, _, _, _ in work]
                weights = []
                for s, (_, _, _, off, dst) in zip(scores, work):
                    case = jnp.where(off == 0, 0, jnp.where(off == -HALF_NEIGHBOURS, 1, 2))
                    s = s + bias[p, case, :, :kw]
                    smax = jnp.max(s, axis=-1, keepdims=True)
                    e = jnp.exp(s - smax)
                    mx[p, dst, :] = jnp.broadcast_to(smax, (bq, HEAD_DIM))
                    den[p, dst, :] = jnp.broadcast_to(jnp.sum(e, axis=-1, keepdims=True),
                                                      (bq, HEAD_DIM))
                    weights.append(e.astype(BF16))
                for e, (_, _, vb, _, dst) in zip(weights, work):
                    acc[p, dst, :] = jnp.dot(e, vb, preferred_element_type=F32)
                return carry2

            lax.fori_loop(0, (r * nb) // BLOCK_GROUP, do_blocks, 0)

        top = jnp.maximum(jnp.maximum(mx[0], mx[1]), mx[2])
        num = jnp.zeros((chunk, HEAD_DIM), F32)
        tot = jnp.zeros((chunk, HEAD_DIM), F32)
        for p in range(len(DILATIONS)):
            w = jnp.exp(mx[p] - top)
            num = num + w * acc[p]
            tot = tot + w * den[p]
        o_ref[pl.ds(pl.multiple_of(base, chunk), chunk), :] = (num / tot).astype(o_ref.dtype)
        return carry

    lax.fori_loop(0, seq // chunk, do_chunk, 0)


def _attention(z, o_prev, rope_cos, rope_sin, seq, batch_lo, batch_n, bq=128):
    n = z.shape[0]
    chunk = bq * DILATIONS[-1]
    assert seq % chunk == 0 and n % seq == 0
    zb = z.reshape(n // seq, seq, IN_WIDTH)

    def col_spec(off):
        return pl.BlockSpec((None, seq, HEAD_DIM),
                            lambda b, h: (batch_lo + b, 0, off + h))

    in_specs = [
        col_spec(0), col_spec(N_ATTN_HEADS), col_spec(2 * N_ATTN_HEADS),
        pl.BlockSpec((seq, HEAD_DIM), lambda b, h: (0, 0), pipeline_mode=pl.Buffered(1)),
        pl.BlockSpec((seq, HEAD_DIM), lambda b, h: (0, 0), pipeline_mode=pl.Buffered(1)),
    ]
    args = [zb, zb, zb, rope_cos, rope_sin]
    aliases = {}
    if o_prev is not None:
        in_specs.append(pl.BlockSpec(memory_space=pl.ANY))
        args.append(o_prev.reshape(n // seq, seq, ATTN_WIDTH))
        aliases = {5: 0}

    def kern(*refs):
        if o_prev is not None:
            refs = refs[:5] + refs[6:]
        _attn_kernel(*refs, seq=seq, bq=bq)

    out = pl.pallas_call(
        kern,
        grid=(batch_n, N_ATTN_HEADS),
        in_specs=in_specs,
        out_specs=pl.BlockSpec((None, seq, HEAD_DIM), lambda b, h: (batch_lo + b, 0, h)),
        out_shape=jax.ShapeDtypeStruct((n // seq, seq, ATTN_WIDTH), BF16),
        scratch_shapes=[
            pltpu.VMEM((seq, HEAD_DIM), F32),
            pltpu.VMEM((seq, HEAD_DIM), F32),
            pltpu.VMEM((seq, HEAD_DIM), F32),
            pltpu.VMEM((len(DILATIONS), chunk, HEAD_DIM), F32),
            pltpu.VMEM((len(DILATIONS), chunk, HEAD_DIM), F32),
            pltpu.VMEM((len(DILATIONS), chunk, HEAD_DIM), F32),
            pltpu.VMEM((len(DILATIONS), 3, bq, bq + 2 * HALF_NEIGHBOURS), F32),
        ],
        input_output_aliases=aliases,
        compiler_params=_cparams(("parallel", "parallel"), vmem=ATTN_VMEM_LIMIT),
    )(*args)
    return out.reshape(n, ATTN_WIDTH)


def _rope_tables(seq):
    half = HEAD_DIM // 2
    inv = jnp.power(ROPE_THETA, -jnp.arange(half, dtype=F32) * 2.0 / HEAD_DIM)
    ang = jnp.arange(seq, dtype=F32)[:, None] * inv[None, :]
    cos, sin = jnp.cos(ang), jnp.sin(ang)
    return jnp.concatenate([cos, cos], axis=-1), jnp.concatenate([-sin, sin], axis=-1)


def _dft_factors(seq):
    s1 = {2048: 32, 8192: 64}.get(seq)
    if s1 is None:
        s1 = 1
        while s1 * s1 < seq:
            s1 *= 2
        s1 = seq // s1 if (seq // s1) * s1 == seq else s1
    return s1, seq // s1


@functools.lru_cache(maxsize=None)
def _dft_tables(seq):
    s1, s2 = _dft_factors(seq)
    c = np.arange(FOURIER_GROUP)
    ang_c = 2.0 * np.pi * ((c[:, None] * c[None, :]) % FOURIER_GROUP) / FOURIER_GROUP
    norm = 1.0 / np.sqrt(float(seq) * FOURIER_GROUP)
    w_chan = np.concatenate([np.cos(ang_c), -np.sin(ang_c)], axis=1) * norm
    k1 = np.arange(s1)[None, :, None]
    n1 = np.arange(s1)[None, None, :]
    n2 = np.arange(s2)[:, None, None]
    ang1 = 2.0 * np.pi * ((k1 * (n2 + s2 * n1)) % seq) / seq
    stage1 = np.concatenate([np.cos(ang1), np.sin(ang1)], axis=1)
    j = np.arange(s2)
    ang2 = 2.0 * np.pi * ((j[:, None] * j[None, :]) % s2) / s2
    return (np.asarray(w_chan, np.float32), np.asarray(stage1, np.float32),
            np.asarray(np.cos(ang2), np.float32), np.asarray(np.sin(ang2), np.float32))


def _fourier_kernel(f_ref, wc_ref, st1_ref, c2_ref, s2_ref, wf_ref, o_ref,
                    zr_scr, zi_scr, tr_scr, ti_scr, y_scr, *, s1, s2):
    g = FOURIER_GROUP
    z = jnp.dot(f_ref[...], wc_ref[...], preferred_element_type=F32)
    zr_scr[...] = z[:, :g]
    zi_scr[...] = z[:, g:]

    def stage1(t, carry):
        n2s = [t * DFT_GROUP + u for u in range(DFT_GROUP)]
        zs = []
        for n2 in n2s:
            rows = pl.ds(n2, s1, stride=s2)
            zs.append(jnp.concatenate([zr_scr[rows, :], zi_scr[rows, :]], axis=1).astype(BF16))
        prs = [jnp.dot(st1_ref[n2], z, preferred_element_type=F32)
               for n2, z in zip(n2s, zs)]
        for n2, pr in zip(n2s, prs):
            dst = pl.ds(pl.multiple_of(n2 * s1, s1), s1)
            tr_scr[dst, :] = pr[:s1, :g] + pr[s1:, g:]
            ti_scr[dst, :] = pr[:s1, g:] - pr[s1:, :g]
        return carry

    lax.fori_loop(0, s2 // DFT_GROUP, stage1, 0)

    def stage2(t, carry):
        rows = [pl.ds(t * DFT_GROUP + u, s2, stride=s1) for u in range(DFT_GROUP)]
        ts = [(tr_scr[r, :].astype(BF16), ti_scr[r, :].astype(BF16)) for r in rows]
        ys = [jnp.dot(c2_ref[...], tr, preferred_element_type=F32)
              + jnp.dot(s2_ref[...], ti, preferred_element_type=F32) for tr, ti in ts]
        for r, y in zip(rows, ys):
            y_scr[r, :] = y
        return carry

    lax.fori_loop(0, s1 // DFT_GROUP, stage2, 0)
    o_ref[...] = jnp.dot(y_scr[...].astype(BF16), wf_ref[...],
                         preferred_element_type=F32).astype(o_ref.dtype)


def _fourier(z, o_prev, w_fourier, layer, seq, batch_lo, batch_n):
    n = z.shape[0]
    s1, s2 = _dft_factors(seq)
    w_chan, stage1, c2, sn2 = (jnp.asarray(t, BF16) for t in _dft_tables(seq))
    zb = z.reshape(n // seq, seq, IN_WIDTH)
    f_col = 3 * N_ATTN_HEADS
    in_specs = [
        pl.BlockSpec((None, seq, FOURIER_GROUP), lambda b, g: (batch_lo + b, 0, f_col + g)),
        pl.BlockSpec((FOURIER_GROUP, 2 * FOURIER_GROUP), lambda b, g: (0, 0)),
        pl.BlockSpec((s2, 2 * s1, s1), lambda b, g: (0, 0, 0)),
        pl.BlockSpec((s2, s2), lambda b, g: (0, 0)),
        pl.BlockSpec((s2, s2), lambda b, g: (0, 0)),
        pl.BlockSpec((None, None, FOURIER_GROUP, FOURIER_GROUP), lambda b, g: (layer, g, 0, 0)),
    ]
    args = [zb, w_chan, stage1, c2, sn2, w_fourier]
    aliases = {}
    if o_prev is not None:
        in_specs.append(pl.BlockSpec(memory_space=pl.ANY))
        args.append(o_prev.reshape(n // seq, seq, FOURIER_WIDTH))
        aliases = {6: 0}

    def kern(*refs):
        if o_prev is not None:
            refs = refs[:6] + refs[7:]
        _fourier_kernel(*refs, s1=s1, s2=s2)

    out = pl.pallas_call(
        kern,
        grid=(batch_n, N_FOURIER_GROUPS),
        in_specs=in_specs,
        out_specs=pl.BlockSpec((None, seq, FOURIER_GROUP), lambda b, g: (batch_lo + b, 0, g)),
        out_shape=jax.ShapeDtypeStruct((n // seq, seq, FOURIER_WIDTH), BF16),
        scratch_shapes=[pltpu.VMEM((seq, FOURIER_GROUP), F32)] * 5,
        input_output_aliases=aliases,
        compiler_params=_cparams(("parallel", "parallel")),
    )(*args)
    return out.reshape(n, FOURIER_WIDTH)


def _out_proj_kernel(oa_ref, of_ref, wa_ref, wf_ref, g_ref, *refs, src_rows, tm):
    ns = len(src_rows)
    o_ref = refs[ns]
    y = jnp.dot(oa_ref[...], wa_ref[...], preferred_element_type=F32)
    y = y + jnp.dot(of_ref[...], wf_ref[...], preferred_element_type=F32)
    gated = g_ref[...] * y

    def residual(x_ref):
        o_ref[...] = x_ref[...] + gated

    _with_token_tile(refs[:ns], src_rows, tm, residual)


def _out_proj(o_attn, o_f, xs, mod, w_out, layer, row_batch, tm=512, tn=1024):
    n = sum(x.shape[0] for x in xs)
    src_rows = tuple(x.shape[0] for x in xs)
    fblk = ATTN_WIDTH // FOURIER_WIDTH

    def gate_index(i, j):
        return (row_batch(i * tm) * 6 + 2, 0, j)

    return pl.pallas_call(
        functools.partial(_out_proj_kernel, src_rows=src_rows, tm=tm),
        grid=(n // tm, D_MODEL // tn),
        in_specs=[
            pl.BlockSpec((tm, ATTN_WIDTH), lambda i, j: (i, 0)),
            pl.BlockSpec((tm, FOURIER_WIDTH), lambda i, j: (i, 0)),
            pl.BlockSpec((None, ATTN_WIDTH, tn), lambda i, j: (layer, 0, j)),
            pl.BlockSpec((None, FOURIER_WIDTH, tn), lambda i, j: (layer, fblk, j)),
            pl.BlockSpec((None, 1, tn), gate_index),
        ] + _token_specs(xs, tm, tn, column_tiled=True),
        out_specs=pl.BlockSpec((tm, tn), lambda i, j: (i, j)),
        out_shape=jax.ShapeDtypeStruct((n, D_MODEL), F32),
        compiler_params=_cparams(("parallel", "parallel")),
    )(o_attn, o_f, w_out, w_out, mod, *xs)


def _swiglu_accumulate(h_scr, w1_ref, w3_ref, w2_ref, o_ref):
    h = h_scr[...]
    a = jnp.dot(h, w1_ref[...], preferred_element_type=F32)
    b = jnp.dot(h, w3_ref[...], preferred_element_type=F32)
    g = (_silu(a) * b).astype(BF16)
    for c in range(0, D_MODEL, DOWN_CHUNK):
        o_ref[:, c:c + DOWN_CHUNK] += jnp.dot(g, w2_ref[:, c:c + DOWN_CHUNK],
                                              preferred_element_type=F32)


def _ffn_dense_kernel(x_ref, g_ref, sh_ref, sc_ref, gate_ref, w1_ref, w3_ref, w2_ref, o_ref,
                      h_scr):
    j = pl.program_id(1)

    @pl.when(j == 0)
    def _():
        _norm_mod_rows(x_ref, g_ref, sh_ref, sc_ref, h_scr)
        o_ref[...] = jnp.zeros_like(o_ref)

    _swiglu_accumulate(h_scr, w1_ref, w3_ref, w2_ref, o_ref)

    @pl.when(j == pl.num_programs(1) - 1)
    def _():
        o_ref[...] = x_ref[...] + gate_ref[...] * o_ref[...]


def _ffn_dense(x, gain, mod, w1, w3, w2, layer, idx, row_batch, tm=512, tf=512):
    n = x.shape[0]
    d_ff = w1.shape[-1]
    return pl.pallas_call(
        _ffn_dense_kernel,
        grid=(n // tm, d_ff // tf),
        in_specs=[
            pl.BlockSpec((tm, D_MODEL), lambda i, j: (i, 0)),
            pl.BlockSpec((None, 1, D_MODEL), lambda i, j: (layer, 0, 0)),
            _mod_spec(3, 6, row_batch, tm),
            _mod_spec(4, 6, row_batch, tm),
            _mod_spec(5, 6, row_batch, tm),
            pl.BlockSpec((None, D_MODEL, tf), lambda i, j: (idx, 0, j)),
            pl.BlockSpec((None, D_MODEL, tf), lambda i, j: (idx, 0, j)),
            pl.BlockSpec((None, tf, D_MODEL), lambda i, j: (idx, j, 0)),
        ],
        out_specs=pl.BlockSpec((tm, D_MODEL), lambda i, j: (i, 0)),
        out_shape=jax.ShapeDtypeStruct((n, D_MODEL), F32),
        scratch_shapes=[pltpu.VMEM((tm, D_MODEL), BF16)],
        compiler_params=_cparams(("parallel", "arbitrary")),
    )(x, gain, mod, mod, mod, w1, w3, w2)


def _router_kernel(x_ref, g_ref, sh_ref, sc_ref, rw_ref, h_ref, idx_ref, gate_ref, rank_ref,
                   cnt_ref, cnt_scr, *, tm):
    @pl.when(pl.program_id(0) == 0)
    def _():
        cnt_scr[...] = jnp.zeros_like(cnt_scr)

    h = _norm_mod(x_ref[...], g_ref[...], sh_ref[...], sc_ref[...])
    h_ref[...] = h
    logits = jnp.dot(h, rw_ref[...], preferred_element_type=F32,
                     precision=lax.Precision.HIGHEST)
    lane = lax.broadcasted_iota(jnp.int32, (tm, LANES), 1).astype(F32)
    logits = jnp.where(lane < N_EXPERTS, logits, -jnp.inf)
    v1 = jnp.max(logits, axis=-1, keepdims=True)
    i1 = jnp.min(jnp.where(logits == v1, lane, float(LANES)), axis=-1, keepdims=True)
    rest = jnp.where(lane == i1, -jnp.inf, logits)
    v2 = jnp.max(rest, axis=-1, keepdims=True)
    i2 = jnp.min(jnp.where(rest == v2, lane, float(LANES)), axis=-1, keepdims=True)
    e2 = jnp.exp(v2 - v1)
    gate1 = 1.0 / (1.0 + e2)
    gate2 = e2 / (1.0 + e2)

    hot1 = (lane == i1).astype(BF16)
    hot2 = (lane == i2).astype(BF16)
    r_i = lax.broadcasted_iota(jnp.int32, (tm, tm), 0)
    c_i = lax.broadcasted_iota(jnp.int32, (tm, tm), 1)
    before = (c_i < r_i).astype(BF16)
    pre1 = jnp.dot(before, hot1, preferred_element_type=F32)
    pre2 = jnp.dot(before, hot2, preferred_element_type=F32)
    tot1 = jnp.sum(hot1.astype(F32), axis=0, keepdims=True)
    tot2 = jnp.sum(hot2.astype(F32), axis=0, keepdims=True)
    cnt = cnt_scr[...]
    rank1 = jnp.sum(jnp.where(lane == i1, pre1 + cnt, 0.0), axis=-1, keepdims=True)
    rank2 = jnp.sum(jnp.where(lane == i2, pre2 + cnt + tot1, 0.0), axis=-1, keepdims=True)
    cnt = cnt + tot1 + tot2
    cnt_scr[...] = cnt
    cnt_ref[...] = jnp.broadcast_to(cnt, cnt_ref.shape).astype(jnp.int32)

    idx_ref[...] = jnp.where(lane == 0, i1, jnp.where(lane == 1, i2, 0.0)).astype(jnp.int32)
    gate_ref[...] = jnp.where(lane == 0, gate1, jnp.where(lane == 1, gate2, 0.0))
    rank_ref[...] = jnp.where(lane == 0, rank1, jnp.where(lane == 1, rank2, 0.0)).astype(jnp.int32)


def _router(x, gain, mod, router_w, layer, row_batch, tm=512):
    n = x.shape[0]
    rw = jnp.zeros((D_MODEL, LANES), F32).at[:, :N_EXPERTS].set(router_w)
    tile = lambda i: (i, 0)
    return pl.pallas_call(
        functools.partial(_router_kernel, tm=tm),
        grid=(n // tm,),
        in_specs=[
            pl.BlockSpec((tm, D_MODEL), tile),
            pl.BlockSpec((None, 1, D_MODEL), lambda i: (layer, 0, 0)),
            _mod_spec(3, 6, row_batch, tm),
            _mod_spec(4, 6, row_batch, tm),
            pl.BlockSpec((D_MODEL, LANES), lambda i: (0, 0)),
        ],
        out_specs=[
            pl.BlockSpec((tm, D_MODEL), tile),
            pl.BlockSpec((tm, LANES), tile),
            pl.BlockSpec((tm, LANES), tile),
            pl.BlockSpec((tm, LANES), tile),
            pl.BlockSpec((8, LANES), lambda i: (0, 0)),
        ],
        out_shape=[
            jax.ShapeDtypeStruct((n, D_MODEL), F32),
            jax.ShapeDtypeStruct((n, LANES), jnp.int32),
            jax.ShapeDtypeStruct((n, LANES), F32),
            jax.ShapeDtypeStruct((n, LANES), jnp.int32),
            jax.ShapeDtypeStruct((8, LANES), jnp.int32),
        ],
        scratch_shapes=[pltpu.VMEM((1, LANES), F32)],
        compiler_params=_cparams(("arbitrary",)),
    )(x, gain, mod, mod, rw)


def _dispatch_kernel(pos_ref, pad_ref, h_ref, xs_ref, zeros, sem, zero_sem, *, tt, pad_rows):
    base = pl.program_id(0) * tt

    @pl.when(pl.program_id(0) == 0)
    def _():
        zeros[...] = jnp.zeros_like(zeros)

        def fill(e):
            start = pl.multiple_of(pad_ref[e], SUBLANES)
            return pltpu.make_async_copy(zeros, xs_ref.at[pl.ds(start, pad_rows), :], zero_sem)

        for e in range(N_EXPERTS):
            fill(e).start()
        for e in range(N_EXPERTS):
            fill(e).wait()

    def issue(t, c):
        for slot in range(2):
            pltpu.make_async_copy(h_ref.at[pl.ds(t, 1), :],
                                  xs_ref.at[pl.ds(pos_ref[2 * (base + t) + slot], 1), :],
                                  sem).start()
        return c

    lax.fori_loop(0, tt, issue, 0, unroll=ROW_DMA_UNROLL)
    for slot in range(2):
        pltpu.make_async_copy(h_ref, xs_ref.at[pl.ds(0, tt), :], sem).wait()


def _dispatch(h, pos, pad_start, rows, pad_rows, tt=256):
    n = h.shape[0]
    return pl.pallas_call(
        functools.partial(_dispatch_kernel, tt=tt, pad_rows=pad_rows),
        grid_spec=pltpu.PrefetchScalarGridSpec(
            num_scalar_prefetch=2,
            grid=(n // tt,),
            in_specs=[pl.BlockSpec((tt, D_MODEL), lambda i, pos, pad: (i, 0))],
            out_specs=pl.BlockSpec(memory_space=pl.ANY),
            scratch_shapes=[pltpu.VMEM((pad_rows, D_MODEL), F32),
                            pltpu.SemaphoreType.DMA(()), pltpu.SemaphoreType.DMA(())],
        ),
        out_shape=jax.ShapeDtypeStruct((rows, D_MODEL), F32),
        compiler_params=_cparams(("arbitrary",), has_side_effects=True),
    )(pos, pad_start, h)


def _moe_ffn_kernel(te_ref, nv_ref, x_ref, w1_ref, w3_ref, w2_ref, o_ref, h_scr):
    i = pl.program_id(0)
    j = pl.program_id(1)

    @pl.when(i < nv_ref[0])
    def _():
        @pl.when(j == 0)
        def _():
            h_scr[...] = x_ref[...].astype(BF16)
            o_ref[...] = jnp.zeros_like(o_ref)

        _swiglu_accumulate(h_scr, w1_ref, w3_ref, w2_ref, o_ref)


def _moe_ffn(xs, tile_expert, n_valid, w1, w3, w2, idx, tm, tf=1024):
    rows = tile_expert.shape[0] * tm
    d_ff = w1.shape[-1]
    assert d_ff % tf == 0 and rows <= xs.shape[0]
    nj = d_ff // tf

    def row_index(i, j, te, nv):
        return (jnp.maximum(jnp.minimum(i, nv[0] - 1), 0), 0)

    def up_index(i, j, te, nv):
        return (idx, te[i], 0, jnp.where(i < nv[0], j, nj - 1))

    def down_index(i, j, te, nv):
        return (idx, te[i], jnp.where(i < nv[0], j, nj - 1), 0)

    return pl.pallas_call(
        _moe_ffn_kernel,
        grid_spec=pltpu.PrefetchScalarGridSpec(
            num_scalar_prefetch=2,
            grid=(rows // tm, nj),
            in_specs=[
                pl.BlockSpec((tm, D_MODEL), row_index),
                pl.BlockSpec((None, None, D_MODEL, tf), up_index),
                pl.BlockSpec((None, None, D_MODEL, tf), up_index),
                pl.BlockSpec((None, None, tf, D_MODEL), down_index),
            ],
            out_specs=pl.BlockSpec((tm, D_MODEL), row_index),
            scratch_shapes=[pltpu.VMEM((tm, D_MODEL), BF16)],
        ),
        out_shape=jax.ShapeDtypeStruct((rows, D_MODEL), F32),
        compiler_params=_cparams(("arbitrary", "arbitrary"), vmem=LARGE_VMEM_LIMIT),
    )(tile_expert, n_valid, xs, w1, w3, w2)


def _combine_kernel(pos_ref, ys_ref, x_ref, gate_ref, g2_ref, o_ref, buf, sem, *, tt):
    base = pl.program_id(0) * tt

    def issue(t, c):
        for slot in range(2):
            pltpu.make_async_copy(ys_ref.at[pl.ds(pos_ref[2 * (base + t) + slot], 1), :],
                                  buf.at[slot, pl.ds(t, 1), :], sem).start()
        return c

    lax.fori_loop(0, tt, issue, 0, unroll=ROW_DMA_UNROLL)
    for slot in range(2):
        pltpu.make_async_copy(ys_ref.at[pl.ds(0, tt), :], buf.at[slot], sem).wait()
    gates = gate_ref[...]
    f = gates[:, 0:1] * buf[0] + gates[:, 1:2] * buf[1]
    o_ref[...] = x_ref[...] + g2_ref[...] * f


def _combine_final_kernel(pos_ref, ys_ref, x_ref, gate_ref, g2_ref, gf_ref, shf_ref, scf_ref,
                          *refs, out_rows, tt):
    out_refs, (x_new, buf, sem) = refs[:len(out_rows)], refs[len(out_rows):]
    _combine_kernel(pos_ref, ys_ref, x_ref, gate_ref, g2_ref, x_new, buf, sem, tt=tt)
    y = _norm_mod(x_new[...], gf_ref[...], shf_ref[...], scf_ref[...])
    i, lo = pl.program_id(0), 0
    for o_ref, rows in zip(out_refs, out_rows):
        nt = rows // tt

        @pl.when((i >= lo) & (i < lo + nt))
        def _(o_ref=o_ref):
            o_ref[...] = y

        lo += nt


def _combine(ys, pos, x, gates, mod, row_batch, final=None, tt=256):
    n = x.shape[0]
    in_specs = [
        pl.BlockSpec(memory_space=pl.ANY),
        pl.BlockSpec((tt, D_MODEL), lambda i, pos: (i, 0)),
        pl.BlockSpec((tt, LANES), lambda i, pos: (i, 0)),
        _mod_spec(5, 6, row_batch, tt),
    ]
    args = [pos, ys, x, gates, mod]
    scratch = [pltpu.VMEM((2, tt, D_MODEL), F32), pltpu.SemaphoreType.DMA(())]
    if final is None:
        body = functools.partial(_combine_kernel, tt=tt)
        out_specs = pl.BlockSpec((tt, D_MODEL), lambda i, pos: (i, 0))
        out_shape = jax.ShapeDtypeStruct((n, D_MODEL), F32)
    else:
        gain, fmod, out_rows = final
        body = functools.partial(_combine_final_kernel, out_rows=out_rows, tt=tt)
        in_specs += [pl.BlockSpec((1, D_MODEL), lambda i, pos: (0, 0)),
                     _mod_spec(0, 2, row_batch, tt), _mod_spec(1, 2, row_batch, tt)]
        args += [gain, fmod, fmod]
        out_specs, out_shape, lo = [], [], 0
        for rows in out_rows:
            nt = rows // tt

            def index(i, pos, lo=lo, nt=nt):
                return (jnp.clip(i - lo, 0, nt - 1), 0)

            out_specs.append(pl.BlockSpec((tt, D_MODEL), index))
            out_shape.append(jax.ShapeDtypeStruct((rows, D_MODEL), F32))
            lo += nt
        scratch = [pltpu.VMEM((tt, D_MODEL), F32)] + scratch
    return pl.pallas_call(
        body,
        grid_spec=pltpu.PrefetchScalarGridSpec(
            num_scalar_prefetch=1, grid=(n // tt,), in_specs=in_specs, out_specs=out_specs,
            scratch_shapes=scratch),
        out_shape=out_shape,
        compiler_params=_cparams(("arbitrary",)),
    )(*args)


def _moe_layer(x, gain, mod, router_w, w1, w3, w2, layer, idx, row_batch, final=None, tm=512):
    n = x.shape[0]
    h, top_idx, gates, rank, counts = _router(x, gain, mod, router_w[idx], layer, row_batch)
    counts = counts[0, :N_EXPERTS]
    padded = ((counts + tm - 1) // tm) * tm
    ends = jnp.cumsum(padded)
    starts = ends - padded
    experts = top_idx[:, :2]
    pos = (starts[experts] + rank[:, :2]).reshape(-1).astype(jnp.int32)
    n_tiles = (2 * n) // tm + N_EXPERTS
    tile_start = jnp.arange(n_tiles, dtype=jnp.int32) * tm
    tile_expert = jnp.minimum(jnp.sum(tile_start[:, None] >= ends[None, :], axis=1),
                              N_EXPERTS - 1).astype(jnp.int32)
    n_valid = (ends[-1:] // tm).astype(jnp.int32)
    pad_start = ((starts + counts) // SUBLANES * SUBLANES).astype(jnp.int32)
    xs = _dispatch(h, pos, pad_start, (n_tiles + 2) * tm, tm + SUBLANES)
    ys = _moe_ffn(xs, tile_expert, n_valid, w1, w3, w2, idx, tm)
    return _combine(ys, pos, x, gates, mod, row_batch, final)


def _final_kernel(x_ref, g_ref, sh_ref, sc_ref, o_ref):
    o_ref[...] = _norm_mod(x_ref[...], g_ref[...], sh_ref[...], sc_ref[...])


def _final(x, gain, fmod, row_lo, rows, row_batch, tm=512):
    off = row_lo // tm

    def mod_spec(which):
        return pl.BlockSpec((None, 1, D_MODEL),
                            lambda i: (row_batch((i + off) * tm) * 2 + which, 0, 0))

    return pl.pallas_call(
        _final_kernel,
        grid=(rows // tm,),
        in_specs=[
            pl.BlockSpec((tm, D_MODEL), lambda i: (i + off, 0)),
            pl.BlockSpec((1, D_MODEL), lambda i: (0, 0)),
            mod_spec(0),
            mod_spec(1),
        ],
        out_specs=pl.BlockSpec((tm, D_MODEL), lambda i: (i, 0)),
        out_shape=jax.ShapeDtypeStruct((rows, D_MODEL), F32),
        compiler_params=_cparams(("parallel",)),
    )(x, gain, fmod, fmod)


def kernel(x_prompt, x_sample, c_prompt, c_sample, w_mod, b_mod, norm_mix, w_in, w_fourier, w_out,
           norm_ffn, dense_w1, dense_w3, dense_w2, router_w, moe_w1, moe_w3, moe_w2,
           w_final_mod, b_final_mod, norm_final):
    batch, seq_p, d = x_prompt.shape
    batch_s, seq_s, _ = x_sample.shape
    n_p, n_s = batch * seq_p, batch_s * seq_s
    n = n_p + n_s
    depth = w_mod.shape[0]
    assert d == D_MODEL and batch_s == 1 and n_p % seq_s == 0 and batch + batch_s <= MOD_ROWS

    def row_batch(row):
        return jnp.minimum(row // seq_p, batch)

    x = (x_prompt.reshape(n_p, d), x_sample.reshape(n_s, d))
    c_all = jnp.zeros((MOD_ROWS, d), F32).at[:batch].set(c_prompt).at[batch:batch + 1].set(c_sample)

    mod = _modulation(c_all, w_mod, b_mod)
    fmod = _modulation(c_all, w_final_mod[None], b_final_mod[None])[0]
    fmod = fmod.reshape(MOD_ROWS * 2, 1, d)

    gain_mix = norm_mix.reshape(depth, 1, d)
    gain_ffn = norm_ffn.reshape(depth, 1, d)
    gain_final = norm_final.reshape(1, d)
    w_in_b = w_in.astype(BF16)
    w_out_b = w_out.astype(BF16)
    w_fourier_b = w_fourier.astype(BF16)
    dense_b = [w.astype(BF16) for w in (dense_w1, dense_w3, dense_w2)]
    moe_b = [w.astype(BF16) for w in (moe_w1, moe_w3, moe_w2)]
    rope_p = _rope_tables(seq_p)
    rope_s = _rope_tables(seq_s)

    for l in range(depth):
        mod_l = mod[l].reshape(MOD_ROWS * 6, 1, d)
        z = _in_proj(x, gain_mix, mod_l, w_in_b, l, row_batch)
        o_attn = _attention(z, None, *rope_p, seq_p, 0, batch)
        o_attn = _attention(z, o_attn, *rope_s, seq_s, n_p // seq_s, batch_s)
        o_f = _fourier(z, None, w_fourier_b, l, seq_p, 0, batch)
        o_f = _fourier(z, o_f, w_fourier_b, l, seq_s, n_p // seq_s, batch_s)
        x = _out_proj(o_attn, o_f, x, mod_l, w_out_b, l, row_batch)
        if l % 2 == 0:
            x = (_ffn_dense(x, gain_ffn, mod_l, *dense_b, l, l // 2, row_batch),)
        else:
            final = (gain_final, fmod, (n_p, n_s)) if l == depth - 1 else None
            x = _moe_layer(x, gain_ffn, mod_l, router_w, *moe_b, l, l // 2, row_batch, final)
            x = tuple(x) if final else (x,)

    if depth % 2 == 0:
        y_p, y_s = x
    else:
        y_p = _final(x[0], gain_final, fmod, 0, n_p, row_batch)
        y_s = _final(x[0], gain_final, fmod, n_p, n_s, row_batch)
    return (y_p.reshape(batch, seq_p, d), y_s.reshape(batch_s, seq_s, d))
```

```python
import functools

import numpy as np
import jax
import jax.numpy as jnp
from jax import lax
from jax.experimental import pallas as pl
from jax.experimental.pallas import tpu as pltpu

F32 = jnp.float32
BF16 = jnp.bfloat16

D_MODEL = 2048
HEAD_DIM = 128
N_ATTN_HEADS = 12
ATTN_WIDTH = N_ATTN_HEADS * HEAD_DIM
N_FOURIER_GROUPS = 4
FOURIER_GROUP = 128
FOURIER_WIDTH = N_FOURIER_GROUPS * FOURIER_GROUP
IN_WIDTH = 3 * ATTN_WIDTH + FOURIER_WIDTH
DILATIONS = (1, 4, 16)
HALF_NEIGHBOURS = 64
ROPE_THETA = 10000.0
N_EXPERTS = 8
EPS = 1e-6
NEG_INF = -1e30
LANES = 128
SUBLANES = 8
MOD_ROWS = 16

VMEM_LIMIT = 48 * 1024 * 1024
ATTN_VMEM_LIMIT = 58 * 1024 * 1024
BLOCK_GROUP = 4
DFT_GROUP = 8
NORM_ROWS = 16
ROW_DMA_UNROLL = 8
DOWN_CHUNK = 512
LARGE_VMEM_LIMIT = 56 * 1024 * 1024


def _cparams(sem, vmem=VMEM_LIMIT, **kw):
    return pltpu.CompilerParams(dimension_semantics=sem, vmem_limit_bytes=vmem, **kw)


def _norm_mod(x, g, sh, sc):
    ms = jnp.mean(x * x, axis=-1, keepdims=True)
    return (x * lax.rsqrt(ms + EPS)) * g * (1.0 + sc) + sh


def _norm_mod_rows(x_ref, g_ref, sh_ref, sc_ref, out_ref):
    g, sh, sc = g_ref[...], sh_ref[...], sc_ref[...]

    def body(c, carry):
        rows = pl.ds(pl.multiple_of(c * NORM_ROWS, NORM_ROWS), NORM_ROWS)
        out_ref[rows, :] = _norm_mod(x_ref[rows, :], g, sh, sc).astype(out_ref.dtype)
        return carry

    lax.fori_loop(0, x_ref.shape[0] // NORM_ROWS, body, 0, unroll=8)


def _silu(a):
    return a / (1.0 + jnp.exp(-a))


def _mod_kernel(c_ref, w_ref, b_ref, o_ref):
    cs = _silu(c_ref[...]).astype(BF16)
    o_ref[...] = jnp.dot(cs, w_ref[...].astype(BF16), preferred_element_type=F32) + b_ref[...]


def _modulation(c_all, w, b, tn=1024):
    nl, d, n = w.shape
    return pl.pallas_call(
        _mod_kernel,
        grid=(nl, n // tn),
        in_specs=[
            pl.BlockSpec((MOD_ROWS, d), lambda l, j: (0, 0)),
            pl.BlockSpec((None, d, tn), lambda l, j: (l, 0, j)),
            pl.BlockSpec((None, 1, tn), lambda l, j: (l, 0, j)),
        ],
        out_specs=pl.BlockSpec((None, MOD_ROWS, tn), lambda l, j: (l, 0, j)),
        out_shape=jax.ShapeDtypeStruct((nl, MOD_ROWS, n), F32),
        compiler_params=_cparams(("parallel", "parallel")),
    )(c_all, w, b.reshape(nl, 1, n))


def _mod_spec(which, n_vec, row_batch, tm):
    def index(i, *_):
        return (row_batch(i * tm) * n_vec + which, 0, 0)
    return pl.BlockSpec((None, 1, D_MODEL), index)


def _token_specs(xs, tm, width, column_tiled=False):
    specs, lo = [], 0
    for x in xs:
        nt = x.shape[0] // tm

        def index(i, j=0, *_, lo=lo, nt=nt):
            return (jnp.clip(i - lo, 0, nt - 1), j if column_tiled else 0)

        specs.append(pl.BlockSpec((tm, width), index))
        lo += nt
    return specs


def _with_token_tile(x_refs, src_rows, tm, fn):
    if len(x_refs) == 1:
        fn(x_refs[0])
        return
    i, lo = pl.program_id(0), 0
    for x_ref, rows in zip(x_refs, src_rows):
        nt = rows // tm
        pl.when((i >= lo) & (i < lo + nt))(functools.partial(fn, x_ref))
        lo += nt


def _in_proj_kernel(*refs, src_rows, tm):
    ns = len(src_rows)
    g_ref, sh_ref, sc_ref, w_ref, o_ref, h_scr = refs[ns:]

    def prepare(x_ref):
        _norm_mod_rows(x_ref, g_ref, sh_ref, sc_ref, h_scr)

    @pl.when(pl.program_id(1) == 0)
    def _():
        _with_token_tile(refs[:ns], src_rows, tm, prepare)

    o_ref[...] = jnp.dot(h_scr[...], w_ref[...], preferred_element_type=F32).astype(o_ref.dtype)


def _in_proj(xs, gain, mod, w_in, layer, row_batch, tn=1280):
    tm = 1024 if len(xs) == 1 else 512
    n = sum(x.shape[0] for x in xs)
    src_rows = tuple(x.shape[0] for x in xs)
    return pl.pallas_call(
        functools.partial(_in_proj_kernel, src_rows=src_rows, tm=tm),
        grid=(n // tm, IN_WIDTH // tn),
        in_specs=_token_specs(xs, tm, D_MODEL) + [
            pl.BlockSpec((None, 1, D_MODEL), lambda i, j: (layer, 0, 0)),
            _mod_spec(0, 6, row_batch, tm),
            _mod_spec(1, 6, row_batch, tm),
            pl.BlockSpec((None, D_MODEL, tn), lambda i, j: (layer, 0, j)),
        ],
        out_specs=pl.BlockSpec((tm, tn), lambda i, j: (i, j)),
        out_shape=jax.ShapeDtypeStruct((n, IN_WIDTH), BF16),
        scratch_shapes=[pltpu.VMEM((tm, D_MODEL), BF16)],
        compiler_params=_cparams(("parallel", "arbitrary")),
    )(*xs, gain, mod, mod, w_in)


def _attn_kernel(q_ref, k_ref, v_ref, cos_ref, sin_ref, o_ref,
                 qr, kr, vr, acc, mx, den, bias, *, seq, bq):
    chunk = bq * DILATIONS[-1]

    @pl.when((pl.program_id(0) == 0) & (pl.program_id(1) == 0))
    def _():
        for p, r in enumerate(DILATIONS):
            kw = min(bq + 2 * HALF_NEIGHBOURS, seq // r)
            rel = (lax.broadcasted_iota(jnp.int32, (bq, kw), 1)
                   - lax.broadcasted_iota(jnp.int32, (bq, kw), 0))
            for case, off in enumerate((0, -HALF_NEIGHBOURS, bq - kw)):
                bias[p, case, :, :kw] = jnp.where(jnp.abs(rel + off) <= HALF_NEIGHBOURS,
                                                  0.0, NEG_INF)

    cos = cos_ref[...]
    sin = sin_ref[...]
    q = q_ref[...].astype(F32)
    qr[...] = (q * cos + pltpu.roll(q, HEAD_DIM // 2, 1) * sin) * (1.0 / np.sqrt(HEAD_DIM))
    k = k_ref[...].astype(F32)
    kr[...] = k * cos + pltpu.roll(k, HEAD_DIM // 2, 1) * sin
    vr[...] = v_ref[...].astype(F32)

    def do_chunk(c, carry):
        base = c * chunk
        for p, r in enumerate(DILATIONS):
            sub_len = seq // r
            kw = min(bq + 2 * HALF_NEIGHBOURS, sub_len)
            nb = chunk // (r * bq)
            group = BLOCK_GROUP * (bq + 2 * HALF_NEIGHBOURS) // kw

            def do_blocks(t, carry2, p=p, r=r, sub_len=sub_len, kw=kw, nb=nb, group=group):
                work = []
                for g in range(group):
                    tg = t * group + g
                    m = tg // nb
                    i = tg % nb
                    q0 = (c * nb + i) * bq
                    k0 = jnp.clip(q0 - HALF_NEIGHBOURS, 0, sub_len - kw)
                    qb = qr[pl.ds(m + r * q0, bq, stride=r), :].astype(BF16)
                    kb = kr[pl.ds(m + r * k0, kw, stride=r), :].astype(BF16)
                    vb = vr[pl.ds(m + r * k0, kw, stride=r), :].astype(BF16)
                    dst = pl.ds(m + r * (i * bq), bq, stride=r)
                    work.append((qb, kb, vb, k0 - q0, dst))
                scores = [lax.dot_general(qb, kb, (((1,), (1,)), ((), ())),
                                          preferred_element_type=F32)
                          for qb, kb, _, _, _ in work]
                weights = []
                for s, (_, _, _, off, dst) in zip(scores, work):
                    case = jnp.where(off == 0, 0, jnp.where(off == -HALF_NEIGHBOURS, 1, 2))
                    s = s + bias[p, case, :, :kw]
                    smax = jnp.max(s, axis=-1, keepdims=True)
                    e = jnp.exp(s - smax)
                    mx[p, dst, :] = jnp.broadcast_to(smax, (bq, HEAD_DIM))
                    den[p, dst, :] = jnp.broadcast_to(jnp.sum(e, axis=-1, keepdims=True),
                                                      (bq, HEAD_DIM))
                    weights.append(e.astype(BF16))
                for e, (_, _, vb, _, dst) in zip(weights, work):
                    acc[p, dst, :] = jnp.dot(e, vb, preferred_element_type=F32)
                return carry2

            lax.fori_loop(0, (r * nb) // group, do_blocks, 0)

        top = jnp.maximum(jnp.maximum(mx[0], mx[1]), mx[2])
        num = jnp.zeros((chunk, HEAD_DIM), F32)
        tot = jnp.zeros((chunk, HEAD_DIM), F32)
        for p in range(len(DILATIONS)):
            w = jnp.exp(mx[p] - top)
            num = num + w * acc[p]
            tot = tot + w * den[p]
        o_ref[pl.ds(pl.multiple_of(base, chunk), chunk), :] = (num / tot).astype(o_ref.dtype)
        return carry

    lax.fori_loop(0, seq // chunk, do_chunk, 0)


def _attention(z, o_prev, rope_cos, rope_sin, seq, batch_lo, batch_n, bq=128):
    n = z.shape[0]
    chunk = bq * DILATIONS[-1]
    assert seq % chunk == 0 and n % seq == 0
    zb = z.reshape(n // seq, seq, IN_WIDTH)

    def col_spec(off):
        return pl.BlockSpec((None, seq, HEAD_DIM),
                            lambda b, h: (batch_lo + b, 0, off + h))

    in_specs = [
        col_spec(0), col_spec(N_ATTN_HEADS), col_spec(2 * N_ATTN_HEADS),
        pl.BlockSpec((seq, HEAD_DIM), lambda b, h: (0, 0), pipeline_mode=pl.Buffered(1)),
        pl.BlockSpec((seq, HEAD_DIM), lambda b, h: (0, 0), pipeline_mode=pl.Buffered(1)),
    ]
    args = [zb, zb, zb, rope_cos, rope_sin]
    aliases = {}
    if o_prev is not None:
        in_specs.append(pl.BlockSpec(memory_space=pl.ANY))
        args.append(o_prev.reshape(n // seq, seq, ATTN_WIDTH))
        aliases = {5: 0}

    def kern(*refs):
        if o_prev is not None:
            refs = refs[:5] + refs[6:]
        _attn_kernel(*refs, seq=seq, bq=bq)

    out = pl.pallas_call(
        kern,
        grid=(batch_n, N_ATTN_HEADS),
        in_specs=in_specs,
        out_specs=pl.BlockSpec((None, seq, HEAD_DIM), lambda b, h: (batch_lo + b, 0, h)),
        out_shape=jax.ShapeDtypeStruct((n // seq, seq, ATTN_WIDTH), BF16),
        scratch_shapes=[
            pltpu.VMEM((seq, HEAD_DIM), F32),
            pltpu.VMEM((seq, HEAD_DIM), F32),
            pltpu.VMEM((seq, HEAD_DIM), F32),
            pltpu.VMEM((len(DILATIONS), chunk, HEAD_DIM), F32),
            pltpu.VMEM((len(DILATIONS), chunk, HEAD_DIM), F32),
            pltpu.VMEM((len(DILATIONS), chunk, HEAD_DIM), F32),
            pltpu.VMEM((len(DILATIONS), 3, bq, bq + 2 * HALF_NEIGHBOURS), F32),
        ],
        input_output_aliases=aliases,
        compiler_params=_cparams(("arbitrary", "arbitrary"), vmem=ATTN_VMEM_LIMIT),
    )(*args)
    return out.reshape(n, ATTN_WIDTH)


def _rope_tables(seq):
    half = HEAD_DIM // 2
    inv = jnp.power(ROPE_THETA, -jnp.arange(half, dtype=F32) * 2.0 / HEAD_DIM)
    ang = jnp.arange(seq, dtype=F32)[:, None] * inv[None, :]
    cos, sin = jnp.cos(ang), jnp.sin(ang)
    return jnp.concatenate([cos, cos], axis=-1), jnp.concatenate([-sin, sin], axis=-1)


def _dft_factors(seq):
    s1 = {2048: 32, 8192: 64}.get(seq)
    if s1 is None:
        s1 = 1
        while s1 * s1 < seq:
            s1 *= 2
        s1 = seq // s1 if (seq // s1) * s1 == seq else s1
    return s1, seq // s1


@functools.lru_cache(maxsize=None)
def _dft_tables(seq):
    s1, s2 = _dft_factors(seq)
    c = np.arange(FOURIER_GROUP)
    ang_c = 2.0 * np.pi * ((c[:, None] * c[None, :]) % FOURIER_GROUP) / FOURIER_GROUP
    norm = 1.0 / np.sqrt(float(seq) * FOURIER_GROUP)
    w_chan = np.concatenate([np.cos(ang_c), -np.sin(ang_c)], axis=1) * norm
    k1 = np.arange(s1)[None, :, None]
    n1 = np.arange(s1)[None, None, :]
    n2 = np.arange(s2)[:, None, None]
    ang1 = 2.0 * np.pi * ((k1 * (n2 + s2 * n1)) % seq) / seq
    stage1 = np.concatenate([np.cos(ang1), np.sin(ang1)], axis=1)
    j = np.arange(s2)
    ang2 = 2.0 * np.pi * ((j[:, None] * j[None, :]) % s2) / s2
    return (np.asarray(w_chan, np.float32), np.asarray(stage1, np.float32),
            np.asarray(np.cos(ang2), np.float32), np.asarray(np.sin(ang2), np.float32))


def _fourier_kernel(f_ref, wc_ref, st1_ref, c2_ref, s2_ref, wf_ref, o_ref,
                    zr_scr, zi_scr, tr_scr, ti_scr, y_scr, *, s1, s2):
    g = FOURIER_GROUP
    z = jnp.dot(f_ref[...], wc_ref[...], preferred_element_type=F32)
    zr_scr[...] = z[:, :g]
    zi_scr[...] = z[:, g:]

    def stage1(t, carry):
        n2s = [t * DFT_GROUP + u for u in range(DFT_GROUP)]
        zs = []
        for n2 in n2s:
            rows = pl.ds(n2, s1, stride=s2)
            zs.append(jnp.concatenate([zr_scr[rows, :], zi_scr[rows, :]], axis=1).astype(BF16))
        prs = [jnp.dot(st1_ref[n2], z, preferred_element_type=F32)
               for n2, z in zip(n2s, zs)]
        for n2, pr in zip(n2s, prs):
            dst = pl.ds(pl.multiple_of(n2 * s1, s1), s1)
            tr_scr[dst, :] = pr[:s1, :g] + pr[s1:, g:]
            ti_scr[dst, :] = pr[:s1, g:] - pr[s1:, :g]
        return carry

    lax.fori_loop(0, s2 // DFT_GROUP, stage1, 0)

    def stage2(t, carry):
        rows = [pl.ds(t * DFT_GROUP + u, s2, stride=s1) for u in range(DFT_GROUP)]
        ts = [(tr_scr[r, :].astype(BF16), ti_scr[r, :].astype(BF16)) for r in rows]
        ys = [jnp.dot(c2_ref[...], tr, preferred_element_type=F32)
              + jnp.dot(s2_ref[...], ti, preferred_element_type=F32) for tr, ti in ts]
        for r, y in zip(rows, ys):
            y_scr[r, :] = y
        return carry

    lax.fori_loop(0, s1 // DFT_GROUP, stage2, 0)
    o_ref[...] = jnp.dot(y_scr[...].astype(BF16), wf_ref[...],
                         preferred_element_type=F32).astype(o_ref.dtype)


def _fourier(z, o_prev, w_fourier, layer, seq, batch_lo, batch_n):
    n = z.shape[0]
    s1, s2 = _dft_factors(seq)
    w_chan, stage1, c2, sn2 = (jnp.asarray(t, BF16) for t in _dft_tables(seq))
    zb = z.reshape(n // seq, seq, IN_WIDTH)
    f_col = 3 * N_ATTN_HEADS
    in_specs = [
        pl.BlockSpec((None, seq, FOURIER_GROUP), lambda b, g: (batch_lo + b, 0, f_col + g)),
        pl.BlockSpec((FOURIER_GROUP, 2 * FOURIER_GROUP), lambda b, g: (0, 0)),
        pl.BlockSpec((s2, 2 * s1, s1), lambda b, g: (0, 0, 0)),
        pl.BlockSpec((s2, s2), lambda b, g: (0, 0)),
        pl.BlockSpec((s2, s2), lambda b, g: (0, 0)),
        pl.BlockSpec((None, None, FOURIER_GROUP, FOURIER_GROUP), lambda b, g: (layer, g, 0, 0)),
    ]
    args = [zb, w_chan, stage1, c2, sn2, w_fourier]
    aliases = {}
    if o_prev is not None:
        in_specs.append(pl.BlockSpec(memory_space=pl.ANY))
        args.append(o_prev.reshape(n // seq, seq, FOURIER_WIDTH))
        aliases = {6: 0}

    def kern(*refs):
        if o_prev is not None:
            refs = refs[:6] + refs[7:]
        _fourier_kernel(*refs, s1=s1, s2=s2)

    out = pl.pallas_call(
        kern,
        grid=(batch_n, N_FOURIER_GROUPS),
        in_specs=in_specs,
        out_specs=pl.BlockSpec((None, seq, FOURIER_GROUP), lambda b, g: (batch_lo + b, 0, g)),
        out_shape=jax.ShapeDtypeStruct((n // seq, seq, FOURIER_WIDTH), BF16),
        scratch_shapes=[pltpu.VMEM((seq, FOURIER_GROUP), F32)] * 5,
        input_output_aliases=aliases,
        compiler_params=_cparams(("parallel", "parallel")),
    )(*args)
    return out.reshape(n, FOURIER_WIDTH)


def _out_proj_kernel(oa_ref, of_ref, wa_ref, wf_ref, g_ref, *refs, src_rows, tm):
    ns = len(src_rows)
    o_ref = refs[ns]
    y = jnp.dot(oa_ref[...], wa_ref[...], preferred_element_type=F32)
    y = y + jnp.dot(of_ref[...], wf_ref[...], preferred_element_type=F32)
    gated = g_ref[...] * y

    def residual(x_ref):
        o_ref[...] = x_ref[...] + gated

    _with_token_tile(refs[:ns], src_rows, tm, residual)


def _out_proj(o_attn, o_f, xs, mod, w_out, layer, row_batch, tm=512, tn=D_MODEL):
    n = sum(x.shape[0] for x in xs)
    src_rows = tuple(x.shape[0] for x in xs)
    fblk = ATTN_WIDTH // FOURIER_WIDTH

    def gate_index(i, j):
        return (row_batch(i * tm) * 6 + 2, 0, j)

    return pl.pallas_call(
        functools.partial(_out_proj_kernel, src_rows=src_rows, tm=tm),
        grid=(n // tm, D_MODEL // tn),
        in_specs=[
            pl.BlockSpec((tm, ATTN_WIDTH), lambda i, j: (i, 0)),
            pl.BlockSpec((tm, FOURIER_WIDTH), lambda i, j: (i, 0)),
            pl.BlockSpec((None, ATTN_WIDTH, tn), lambda i, j: (layer, 0, j),
                         pipeline_mode=pl.Buffered(1)),
            pl.BlockSpec((None, FOURIER_WIDTH, tn), lambda i, j: (layer, fblk, j),
                         pipeline_mode=pl.Buffered(1)),
            pl.BlockSpec((None, 1, tn), gate_index),
        ] + _token_specs(xs, tm, tn, column_tiled=True),
        out_specs=pl.BlockSpec((tm, tn), lambda i, j: (i, j)),
        out_shape=jax.ShapeDtypeStruct((n, D_MODEL), F32),
        compiler_params=_cparams(("parallel", "parallel")),
    )(o_attn, o_f, w_out, w_out, mod, *xs)


def _swiglu_accumulate(h_scr, w1_ref, w3_ref, w2_ref, o_ref):
    h = h_scr[...]
    a = jnp.dot(h, w1_ref[...], preferred_element_type=F32)
    b = jnp.dot(h, w3_ref[...], preferred_element_type=F32)
    g = (_silu(a) * b).astype(BF16)
    for c in range(0, D_MODEL, DOWN_CHUNK):
        o_ref[:, c:c + DOWN_CHUNK] += jnp.dot(g, w2_ref[:, c:c + DOWN_CHUNK],
                                              preferred_element_type=F32)


def _ffn_dense_kernel(x_ref, g_ref, sh_ref, sc_ref, gate_ref, w1_ref, w3_ref, w2_ref, o_ref,
                      h_scr):
    j = pl.program_id(1)

    @pl.when(j == 0)
    def _():
        _norm_mod_rows(x_ref, g_ref, sh_ref, sc_ref, h_scr)
        o_ref[...] = jnp.zeros_like(o_ref)

    _swiglu_accumulate(h_scr, w1_ref, w3_ref, w2_ref, o_ref)

    @pl.when(j == pl.num_programs(1) - 1)
    def _():
        o_ref[...] = x_ref[...] + gate_ref[...] * o_ref[...]


def _ffn_dense(x, gain, mod, w1, w3, w2, layer, idx, row_batch, tm=512, tf=512):
    n = x.shape[0]
    d_ff = w1.shape[-1]
    return pl.pallas_call(
        _ffn_dense_kernel,
        grid=(n // tm, d_ff // tf),
        in_specs=[
            pl.BlockSpec((tm, D_MODEL), lambda i, j: (i, 0)),
            pl.BlockSpec((None, 1, D_MODEL), lambda i, j: (layer, 0, 0)),
            _mod_spec(3, 6, row_batch, tm),
            _mod_spec(4, 6, row_batch, tm),
            _mod_spec(5, 6, row_batch, tm),
            pl.BlockSpec((None, D_MODEL, tf), lambda i, j: (idx, 0, j)),
            pl.BlockSpec((None, D_MODEL, tf), lambda i, j: (idx, 0, j)),
            pl.BlockSpec((None, tf, D_MODEL), lambda i, j: (idx, j, 0)),
        ],
        out_specs=pl.BlockSpec((tm, D_MODEL), lambda i, j: (i, 0)),
        out_shape=jax.ShapeDtypeStruct((n, D_MODEL), F32),
        scratch_shapes=[pltpu.VMEM((tm, D_MODEL), BF16)],
        compiler_params=_cparams(("parallel", "arbitrary")),
    )(x, gain, mod, mod, mod, w1, w3, w2)


def _router_kernel(x_ref, g_ref, sh_ref, sc_ref, rw_ref, h_ref, idx_ref, gate_ref, rank_ref,
                   cnt_ref, cnt_scr, *, tm):
    @pl.when(pl.program_id(0) == 0)
    def _():
        cnt_scr[...] = jnp.zeros_like(cnt_scr)

    h = _norm_mod(x_ref[...], g_ref[...], sh_ref[...], sc_ref[...])
    h_ref[...] = h
    logits = jnp.dot(h, rw_ref[...], preferred_element_type=F32,
                     precision=lax.Precision.HIGHEST)
    lane = lax.broadcasted_iota(jnp.int32, (tm, LANES), 1).astype(F32)
    logits = jnp.where(lane < N_EXPERTS, logits, -jnp.inf)
    v1 = jnp.max(logits, axis=-1, keepdims=True)
    i1 = jnp.min(jnp.where(logits == v1, lane, float(LANES)), axis=-1, keepdims=True)
    rest = jnp.where(lane == i1, -jnp.inf, logits)
    v2 = jnp.max(rest, axis=-1, keepdims=True)
    i2 = jnp.min(jnp.where(rest == v2, lane, float(LANES)), axis=-1, keepdims=True)
    e2 = jnp.exp(v2 - v1)
    gate1 = 1.0 / (1.0 + e2)
    gate2 = e2 / (1.0 + e2)

    hot1 = (lane == i1).astype(BF16)
    hot2 = (lane == i2).astype(BF16)
    r_i = lax.broadcasted_iota(jnp.int32, (tm, tm), 0)
    c_i = lax.broadcasted_iota(jnp.int32, (tm, tm), 1)
    before = (c_i < r_i).astype(BF16)
    pre1 = jnp.dot(before, hot1, preferred_element_type=F32)
    pre2 = jnp.dot(before, hot2, preferred_element_type=F32)
    tot1 = jnp.sum(hot1.astype(F32), axis=0, keepdims=True)
    tot2 = jnp.sum(hot2.astype(F32), axis=0, keepdims=True)
    cnt = cnt_scr[...]
    rank1 = jnp.sum(jnp.where(lane == i1, pre1 + cnt, 0.0), axis=-1, keepdims=True)
    rank2 = jnp.sum(jnp.where(lane == i2, pre2 + cnt + tot1, 0.0), axis=-1, keepdims=True)
    cnt = cnt + tot1 + tot2
    cnt_scr[...] = cnt
    cnt_ref[...] = jnp.broadcast_to(cnt, cnt_ref.shape).astype(jnp.int32)

    idx_ref[...] = jnp.where(lane == 0, i1, jnp.where(lane == 1, i2, 0.0)).astype(jnp.int32)
    gate_ref[...] = jnp.where(lane == 0, gate1, jnp.where(lane == 1, gate2, 0.0))
    rank_ref[...] = jnp.where(lane == 0, rank1, jnp.where(lane == 1, rank2, 0.0)).astype(jnp.int32)


def _router(x, gain, mod, router_w, layer, row_batch, tm=512):
    n = x.shape[0]
    rw = jnp.zeros((D_MODEL, LANES), F32).at[:, :N_EXPERTS].set(router_w)
    tile = lambda i: (i, 0)
    return pl.pallas_call(
        functools.partial(_router_kernel, tm=tm),
        grid=(n // tm,),
        in_specs=[
            pl.BlockSpec((tm, D_MODEL), tile),
            pl.BlockSpec((None, 1, D_MODEL), lambda i: (layer, 0, 0)),
            _mod_spec(3, 6, row_batch, tm),
            _mod_spec(4, 6, row_batch, tm),
            pl.BlockSpec((D_MODEL, LANES), lambda i: (0, 0)),
        ],
        out_specs=[
            pl.BlockSpec((tm, D_MODEL), tile),
            pl.BlockSpec((tm, LANES), tile),
            pl.BlockSpec((tm, LANES), tile),
            pl.BlockSpec((tm, LANES), tile),
            pl.BlockSpec((8, LANES), lambda i: (0, 0)),
        ],
        out_shape=[
            jax.ShapeDtypeStruct((n, D_MODEL), F32),
            jax.ShapeDtypeStruct((n, LANES), jnp.int32),
            jax.ShapeDtypeStruct((n, LANES), F32),
            jax.ShapeDtypeStruct((n, LANES), jnp.int32),
            jax.ShapeDtypeStruct((8, LANES), jnp.int32),
        ],
        scratch_shapes=[pltpu.VMEM((1, LANES), F32)],
        compiler_params=_cparams(("arbitrary",)),
    )(x, gain, mod, mod, rw)


def _dispatch_kernel(pos_ref, pad_ref, h_ref, xs_ref, zeros, sem, zero_sem, *, tt, pad_rows):
    base = pl.program_id(0) * tt

    @pl.when(pl.program_id(0) == 0)
    def _():
        zeros[...] = jnp.zeros_like(zeros)

        def fill(e):
            start = pl.multiple_of(pad_ref[e], SUBLANES)
            return pltpu.make_async_copy(zeros, xs_ref.at[pl.ds(start, pad_rows), :], zero_sem)

        for e in range(N_EXPERTS):
            fill(e).start()
        for e in range(N_EXPERTS):
            fill(e).wait()

    def issue(t, c):
        for slot in range(2):
            pltpu.make_async_copy(h_ref.at[pl.ds(t, 1), :],
                                  xs_ref.at[pl.ds(pos_ref[2 * (base + t) + slot], 1), :],
                                  sem).start()
        return c

    lax.fori_loop(0, tt, issue, 0, unroll=ROW_DMA_UNROLL)
    for slot in range(2):
        pltpu.make_async_copy(h_ref, xs_ref.at[pl.ds(0, tt), :], sem).wait()


def _dispatch(h, pos, pad_start, rows, pad_rows, tt=256):
    n = h.shape[0]
    return pl.pallas_call(
        functools.partial(_dispatch_kernel, tt=tt, pad_rows=pad_rows),
        grid_spec=pltpu.PrefetchScalarGridSpec(
            num_scalar_prefetch=2,
            grid=(n // tt,),
            in_specs=[pl.BlockSpec((tt, D_MODEL), lambda i, pos, pad: (i, 0))],
            out_specs=pl.BlockSpec(memory_space=pl.ANY),
            scratch_shapes=[pltpu.VMEM((pad_rows, D_MODEL), F32),
                            pltpu.SemaphoreType.DMA(()), pltpu.SemaphoreType.DMA(())],
        ),
        out_shape=jax.ShapeDtypeStruct((rows, D_MODEL), F32),
        compiler_params=_cparams(("arbitrary",), has_side_effects=True),
    )(pos, pad_start, h)


def _moe_ffn_kernel(te_ref, nv_ref, x_ref, w1_ref, w3_ref, w2_ref, o_ref, h_scr):
    i = pl.program_id(0)
    j = pl.program_id(1)

    @pl.when(i < nv_ref[0])
    def _():
        @pl.when(j == 0)
        def _():
            h_scr[...] = x_ref[...].astype(BF16)
            o_ref[...] = jnp.zeros_like(o_ref)

        _swiglu_accumulate(h_scr, w1_ref, w3_ref, w2_ref, o_ref)


def _moe_ffn(xs, tile_expert, n_valid, w1, w3, w2, idx, tm, tf=1024):
    rows = tile_expert.shape[0] * tm
    d_ff = w1.shape[-1]
    assert d_ff % tf == 0 and rows <= xs.shape[0]
    nj = d_ff // tf

    def row_index(i, j, te, nv):
        return (jnp.maximum(jnp.minimum(i, nv[0] - 1), 0), 0)

    def up_index(i, j, te, nv):
        return (idx, te[i], 0, jnp.where(i < nv[0], j, nj - 1))

    def down_index(i, j, te, nv):
        return (idx, te[i], jnp.where(i < nv[0], j, nj - 1), 0)

    return pl.pallas_call(
        _moe_ffn_kernel,
        grid_spec=pltpu.PrefetchScalarGridSpec(
            num_scalar_prefetch=2,
            grid=(rows // tm, nj),
            in_specs=[
                pl.BlockSpec((tm, D_MODEL), row_index),
                pl.BlockSpec((None, None, D_MODEL, tf), up_index),
                pl.BlockSpec((None, None, D_MODEL, tf), up_index),
                pl.BlockSpec((None, None, tf, D_MODEL), down_index),
            ],
            out_specs=pl.BlockSpec((tm, D_MODEL), row_index),
            scratch_shapes=[pltpu.VMEM((tm, D_MODEL), BF16)],
        ),
        out_shape=jax.ShapeDtypeStruct((rows, D_MODEL), F32),
        compiler_params=_cparams(("arbitrary", "arbitrary"), vmem=LARGE_VMEM_LIMIT),
    )(tile_expert, n_valid, xs, w1, w3, w2)


def _combine_kernel(pos_ref, ys_ref, x_ref, gate_ref, g2_ref, o_ref, buf, sem, *, tt):
    base = pl.program_id(0) * tt

    def issue(t, c):
        for slot in range(2):
            pltpu.make_async_copy(ys_ref.at[pl.ds(pos_ref[2 * (base + t) + slot], 1), :],
                                  buf.at[slot, pl.ds(t, 1), :], sem).start()
        return c

    lax.fori_loop(0, tt, issue, 0, unroll=ROW_DMA_UNROLL)
    for slot in range(2):
        pltpu.make_async_copy(ys_ref.at[pl.ds(0, tt), :], buf.at[slot], sem).wait()
    gates = gate_ref[...]
    f = gates[:, 0:1] * buf[0] + gates[:, 1:2] * buf[1]
    o_ref[...] = x_ref[...] + g2_ref[...] * f


def _combine_final_kernel(pos_ref, ys_ref, x_ref, gate_ref, g2_ref, gf_ref, shf_ref, scf_ref,
                          *refs, out_rows, tt):
    out_refs, (x_new, buf, sem) = refs[:len(out_rows)], refs[len(out_rows):]
    _combine_kernel(pos_ref, ys_ref, x_ref, gate_ref, g2_ref, x_new, buf, sem, tt=tt)
    y = _norm_mod(x_new[...], gf_ref[...], shf_ref[...], scf_ref[...])
    i, lo = pl.program_id(0), 0
    for o_ref, rows in zip(out_refs, out_rows):
        nt = rows // tt

        @pl.when((i >= lo) & (i < lo + nt))
        def _(o_ref=o_ref):
            o_ref[...] = y

        lo += nt


def _combine(ys, pos, x, gates, mod, row_batch, final=None, tt=256):
    n = x.shape[0]
    in_specs = [
        pl.BlockSpec(memory_space=pl.ANY),
        pl.BlockSpec((tt, D_MODEL), lambda i, pos: (i, 0)),
        pl.BlockSpec((tt, LANES), lambda i, pos: (i, 0)),
        _mod_spec(5, 6, row_batch, tt),
    ]
    args = [pos, ys, x, gates, mod]
    scratch = [pltpu.VMEM((2, tt, D_MODEL), F32), pltpu.SemaphoreType.DMA(())]
    if final is None:
        body = functools.partial(_combine_kernel, tt=tt)
        out_specs = pl.BlockSpec((tt, D_MODEL), lambda i, pos: (i, 0))
        out_shape = jax.ShapeDtypeStruct((n, D_MODEL), F32)
    else:
        gain, fmod, out_rows = final
        body = functools.partial(_combine_final_kernel, out_rows=out_rows, tt=tt)
        in_specs += [pl.BlockSpec((1, D_MODEL), lambda i, pos: (0, 0)),
                     _mod_spec(0, 2, row_batch, tt), _mod_spec(1, 2, row_batch, tt)]
        args += [gain, fmod, fmod]
        out_specs, out_shape, lo = [], [], 0
        for rows in out_rows:
            nt = rows // tt

            def index(i, pos, lo=lo, nt=nt):
                return (jnp.clip(i - lo, 0, nt - 1), 0)

            out_specs.append(pl.BlockSpec((tt, D_MODEL), index))
            out_shape.append(jax.ShapeDtypeStruct((rows, D_MODEL), F32))
            lo += nt
        scratch = [pltpu.VMEM((tt, D_MODEL), F32)] + scratch
    return pl.pallas_call(
        body,
        grid_spec=pltpu.PrefetchScalarGridSpec(
            num_scalar_prefetch=1, grid=(n // tt,), in_specs=in_specs, out_specs=out_specs,
            scratch_shapes=scratch),
        out_shape=out_shape,
        compiler_params=_cparams(("arbitrary",)),
    )(*args)


def _moe_layer(x, gain, mod, router_w, w1, w3, w2, layer, idx, row_batch, final=None, tm=512):
    n = x.shape[0]
    h, top_idx, gates, rank, counts = _router(x, gain, mod, router_w[idx], layer, row_batch)
    counts = counts[0, :N_EXPERTS]
    padded = ((counts + tm - 1) // tm) * tm
    ends = jnp.cumsum(padded)
    starts = ends - padded
    experts = top_idx[:, :2]
    pos = (starts[experts] + rank[:, :2]).reshape(-1).astype(jnp.int32)
    n_tiles = (2 * n) // tm + N_EXPERTS
    tile_start = jnp.arange(n_tiles, dtype=jnp.int32) * tm
    tile_expert = jnp.minimum(jnp.sum(tile_start[:, None] >= ends[None, :], axis=1),
                              N_EXPERTS - 1).astype(jnp.int32)
    n_valid = (ends[-1:] // tm).astype(jnp.int32)
    pad_start = ((starts + counts) // SUBLANES * SUBLANES).astype(jnp.int32)
    xs = _dispatch(h, pos, pad_start, (n_tiles + 2) * tm, tm + SUBLANES)
    ys = _moe_ffn(xs, tile_expert, n_valid, w1, w3, w2, idx, tm)
    return _combine(ys, pos, x, gates, mod, row_batch, final)


def _final_kernel(x_ref, g_ref, sh_ref, sc_ref, o_ref):
    o_ref[...] = _norm_mod(x_ref[...], g_ref[...], sh_ref[...], sc_ref[...])


def _final(x, gain, fmod, row_lo, rows, row_batch, tm=512):
    off = row_lo // tm

    def mod_spec(which):
        return pl.BlockSpec((None, 1, D_MODEL),
                            lambda i: (row_batch((i + off) * tm) * 2 + which, 0, 0))

    return pl.pallas_call(
        _final_kernel,
        grid=(rows // tm,),
        in_specs=[
            pl.BlockSpec((tm, D_MODEL), lambda i: (i + off, 0)),
            pl.BlockSpec((1, D_MODEL), lambda i: (0, 0)),
            mod_spec(0),
            mod_spec(1),
        ],
        out_specs=pl.BlockSpec((tm, D_MODEL), lambda i: (i, 0)),
        out_shape=jax.ShapeDtypeStruct((rows, D_MODEL), F32),
        compiler_params=_cparams(("parallel",)),
    )(x, gain, fmod, fmod)


def kernel(x_prompt, x_sample, c_prompt, c_sample, w_mod, b_mod, norm_mix, w_in, w_fourier, w_out,
           norm_ffn, dense_w1, dense_w3, dense_w2, router_w, moe_w1, moe_w3, moe_w2,
           w_final_mod, b_final_mod, norm_final):
    batch, seq_p, d = x_prompt.shape
    batch_s, seq_s, _ = x_sample.shape
    n_p, n_s = batch * seq_p, batch_s * seq_s
    n = n_p + n_s
    depth = w_mod.shape[0]
    assert d == D_MODEL and batch_s == 1 and n_p % seq_s == 0 and batch + batch_s <= MOD_ROWS

    def row_batch(row):
        return jnp.minimum(row // seq_p, batch)

    x = (x_prompt.reshape(n_p, d), x_sample.reshape(n_s, d))
    c_all = jnp.zeros((MOD_ROWS, d), F32).at[:batch].set(c_prompt).at[batch:batch + 1].set(c_sample)

    mod = _modulation(c_all, w_mod, b_mod)
    fmod = _modulation(c_all, w_final_mod[None], b_final_mod[None])[0]
    fmod = fmod.reshape(MOD_ROWS * 2, 1, d)

    gain_mix = norm_mix.reshape(depth, 1, d)
    gain_ffn = norm_ffn.reshape(depth, 1, d)
    gain_final = norm_final.reshape(1, d)
    w_in_b = w_in.astype(BF16)
    w_out_b = w_out.astype(BF16)
    w_fourier_b = w_fourier.astype(BF16)
    dense_b = [w.astype(BF16) for w in (dense_w1, dense_w3, dense_w2)]
    moe_b = [w.astype(BF16) for w in (moe_w1, moe_w3, moe_w2)]
    rope_p = _rope_tables(seq_p)
    rope_s = _rope_tables(seq_s)

    for l in range(depth):
        mod_l = mod[l].reshape(MOD_ROWS * 6, 1, d)
        z = _in_proj(x, gain_mix, mod_l, w_in_b, l, row_batch)
        o_attn = _attention(z, None, *rope_p, seq_p, 0, batch)
        o_attn = _attention(z, o_attn, *rope_s, seq_s, n_p // seq_s, batch_s)
        o_f = _fourier(z, None, w_fourier_b, l, seq_p, 0, batch)
        o_f = _fourier(z, o_f, w_fourier_b, l, seq_s, n_p // seq_s, batch_s)
        x = _out_proj(o_attn, o_f, x, mod_l, w_out_b, l, row_batch)
        if l % 2 == 0:
            x = (_ffn_dense(x, gain_ffn, mod_l, *dense_b, l, l // 2, row_batch),)
        else:
            final = (gain_final, fmod, (n_p, n_s)) if l == depth - 1 else None
            x = _moe_layer(x, gain_ffn, mod_l, router_w, *moe_b, l, l // 2, row_batch, final)
            x = tuple(x) if final else (x,)

    if depth % 2 == 0:
        y_p, y_s = x
    else:
        y_p = _final(x[0], gain_final, fmod, 0, n_p, row_batch)
        y_s = _final(x[0], gain_final, fmod, n_p, n_s, row_batch)
    return (y_p.reshape(batch, seq_p, d), y_s.reshape(batch_s, seq_s, d))
```

```python
import functools

import numpy as np
import jax
import jax.numpy as jnp
from jax import lax
from jax.experimental import pallas as pl
from jax.experimental.pallas import tpu as pltpu

F32 = jnp.float32
BF16 = jnp.bfloat16

D_MODEL = 2048
HEAD_DIM = 128
N_ATTN_HEADS = 12
ATTN_WIDTH = N_ATTN_HEADS * HEAD_DIM
N_FOURIER_GROUPS = 4
FOURIER_GROUP = 128
FOURIER_WIDTH = N_FOURIER_GROUPS * FOURIER_GROUP
IN_WIDTH = 3 * ATTN_WIDTH + FOURIER_WIDTH
DILATIONS = (1, 4, 16)
HALF_NEIGHBOURS = 64
ROPE_THETA = 10000.0
N_EXPERTS = 8
EPS = 1e-6
NEG_INF = -1e30
LANES = 128
SUBLANES = 8
MOD_ROWS = 16

VMEM_LIMIT = 48 * 1024 * 1024
ATTN_VMEM_LIMIT = 58 * 1024 * 1024
BLOCK_GROUP = 4
DFT_GROUP = 8
NORM_ROWS = 16
ROW_DMA_UNROLL = 8
DOWN_CHUNK = 512
LARGE_VMEM_LIMIT = 56 * 1024 * 1024


def _cparams(sem, vmem=VMEM_LIMIT, **kw):
    return pltpu.CompilerParams(dimension_semantics=sem, vmem_limit_bytes=vmem, **kw)


def _norm_mod(x, g, sh, sc):
    ms = jnp.mean(x * x, axis=-1, keepdims=True)
    return (x * lax.rsqrt(ms + EPS)) * g * (1.0 + sc) + sh


def _norm_mod_rows(x_ref, g_ref, sh_ref, sc_ref, out_ref):
    g, sh, sc = g_ref[...], sh_ref[...], sc_ref[...]

    def body(c, carry):
        rows = pl.ds(pl.multiple_of(c * NORM_ROWS, NORM_ROWS), NORM_ROWS)
        out_ref[rows, :] = _norm_mod(x_ref[rows, :], g, sh, sc).astype(out_ref.dtype)
        return carry

    lax.fori_loop(0, x_ref.shape[0] // NORM_ROWS, body, 0, unroll=8)


def _silu(a):
    return a / (1.0 + jnp.exp(-a))


def _mod_kernel(c_ref, w_ref, b_ref, o_ref):
    cs = _silu(c_ref[...]).astype(BF16)
    o_ref[...] = jnp.dot(cs, w_ref[...].astype(BF16), preferred_element_type=F32) + b_ref[...]


def _modulation(c_all, w, b, tn=1024):
    nl, d, n = w.shape
    return pl.pallas_call(
        _mod_kernel,
        grid=(nl, n // tn),
        in_specs=[
            pl.BlockSpec((MOD_ROWS, d), lambda l, j: (0, 0)),
            pl.BlockSpec((None, d, tn), lambda l, j: (l, 0, j)),
            pl.BlockSpec((None, 1, tn), lambda l, j: (l, 0, j)),
        ],
        out_specs=pl.BlockSpec((None, MOD_ROWS, tn), lambda l, j: (l, 0, j)),
        out_shape=jax.ShapeDtypeStruct((nl, MOD_ROWS, n), F32),
        compiler_params=_cparams(("parallel", "parallel")),
    )(c_all, w, b.reshape(nl, 1, n))


def _mod_spec(which, n_vec, row_batch, tm):
    def index(i, *_):
        return (row_batch(i * tm) * n_vec + which, 0, 0)
    return pl.BlockSpec((None, 1, D_MODEL), index)


def _token_specs(xs, tm, width, column_tiled=False):
    specs, lo = [], 0
    for x in xs:
        nt = x.shape[0] // tm

        def index(i, j=0, *_, lo=lo, nt=nt):
            return (jnp.clip(i - lo, 0, nt - 1), j if column_tiled else 0)

        specs.append(pl.BlockSpec((tm, width), index))
        lo += nt
    return specs


def _with_token_tile(x_refs, src_rows, tm, fn):
    if len(x_refs) == 1:
        fn(x_refs[0])
        return
    i, lo = pl.program_id(0), 0
    for x_ref, rows in zip(x_refs, src_rows):
        nt = rows // tm
        pl.when((i >= lo) & (i < lo + nt))(functools.partial(fn, x_ref))
        lo += nt


def _in_proj_kernel(*refs, src_rows, tm):
    ns = len(src_rows)
    g_ref, sh_ref, sc_ref, w_ref, o_ref, h_scr = refs[ns:]

    def prepare(x_ref):
        _norm_mod_rows(x_ref, g_ref, sh_ref, sc_ref, h_scr)

    @pl.when(pl.program_id(1) == 0)
    def _():
        _with_token_tile(refs[:ns], src_rows, tm, prepare)

    o_ref[...] = jnp.dot(h_scr[...], w_ref[...], preferred_element_type=F32).astype(o_ref.dtype)


def _in_proj(xs, gain, mod, w_in, layer, row_batch, tn=1280):
    tm = 1024 if len(xs) == 1 else 512
    n = sum(x.shape[0] for x in xs)
    src_rows = tuple(x.shape[0] for x in xs)
    return pl.pallas_call(
        functools.partial(_in_proj_kernel, src_rows=src_rows, tm=tm),
        grid=(n // tm, IN_WIDTH // tn),
        in_specs=_token_specs(xs, tm, D_MODEL) + [
            pl.BlockSpec((None, 1, D_MODEL), lambda i, j: (layer, 0, 0)),
            _mod_spec(0, 6, row_batch, tm),
            _mod_spec(1, 6, row_batch, tm),
            pl.BlockSpec((None, D_MODEL, tn), lambda i, j: (layer, 0, j)),
        ],
        out_specs=pl.BlockSpec((tm, tn), lambda i, j: (i, j)),
        out_shape=jax.ShapeDtypeStruct((n, IN_WIDTH), BF16),
        scratch_shapes=[pltpu.VMEM((tm, D_MODEL), BF16)],
        compiler_params=_cparams(("parallel", "arbitrary")),
    )(*xs, gain, mod, mod, w_in)


def _attn_kernel(q_ref, k_ref, v_ref, cos_ref, sin_ref, o_ref,
                 qr, kr, vr, acc, mx, den, bias, *, seq, bq):
    chunk = bq * DILATIONS[-1]

    @pl.when((pl.program_id(0) == 0) & (pl.program_id(1) == 0))
    def _():
        for p, r in enumerate(DILATIONS):
            kw = min(bq + 2 * HALF_NEIGHBOURS, seq // r)
            rel = (lax.broadcasted_iota(jnp.int32, (bq, kw), 1)
                   - lax.broadcasted_iota(jnp.int32, (bq, kw), 0))
            for case, off in enumerate((0, -HALF_NEIGHBOURS, bq - kw)):
                bias[p, case, :, :kw] = jnp.where(jnp.abs(rel + off) <= HALF_NEIGHBOURS,
                                                  0.0, NEG_INF)

    cos = cos_ref[...]
    sin = sin_ref[...]
    q = q_ref[...].astype(F32)
    qr[...] = (q * cos + pltpu.roll(q, HEAD_DIM // 2, 1) * sin) * (1.0 / np.sqrt(HEAD_DIM))
    k = k_ref[...].astype(F32)
    kr[...] = k * cos + pltpu.roll(k, HEAD_DIM // 2, 1) * sin
    vr[...] = v_ref[...].astype(F32)

    def do_chunk(c, carry):
        base = c * chunk
        for p, r in enumerate(DILATIONS):
            sub_len = seq // r
            kw = min(bq + 2 * HALF_NEIGHBOURS, sub_len)
            nb = chunk // (r * bq)
            group = BLOCK_GROUP * (bq + 2 * HALF_NEIGHBOURS) // kw

            def do_blocks(t, carry2, p=p, r=r, sub_len=sub_len, kw=kw, nb=nb, group=group):
                work = []
                for g in range(group):
                    tg = t * group + g
                    m = tg // nb
                    i = tg % nb
                    q0 = (c * nb + i) * bq
                    k0 = jnp.clip(q0 - HALF_NEIGHBOURS, 0, sub_len - kw)
                    qb = qr[pl.ds(m + r * q0, bq, stride=r), :].astype(BF16)
                    kb = kr[pl.ds(m + r * k0, kw, stride=r), :].astype(BF16)
                    vb = vr[pl.ds(m + r * k0, kw, stride=r), :].astype(BF16)
                    dst = pl.ds(m + r * (i * bq), bq, stride=r)
                    work.append((qb, kb, vb, k0 - q0, dst))
                scores = [lax.dot_general(qb, kb, (((1,), (1,)), ((), ())),
                                          preferred_element_type=F32)
                          for qb, kb, _, _, _ in work]
                weights = []
                for s, (_, _, _, off, dst) in zip(scores, work):
                    case = jnp.where(off == 0, 0, jnp.where(off == -HALF_NEIGHBOURS, 1, 2))
                    s = s + bias[p, case, :, :kw]
                    smax = jnp.max(s, axis=-1, keepdims=True)
                    e = jnp.exp(s - smax)
                    mx[p, dst, :] = jnp.broadcast_to(smax, (bq, HEAD_DIM))
                    den[p, dst, :] = jnp.broadcast_to(jnp.sum(e, axis=-1, keepdims=True),
                                                      (bq, HEAD_DIM))
                    weights.append(e.astype(BF16))
                for e, (_, _, vb, _, dst) in zip(weights, work):
                    acc[p, dst, :] = jnp.dot(e, vb, preferred_element_type=F32)
                return carry2

            lax.fori_loop(0, (r * nb) // group, do_blocks, 0)

        top = jnp.maximum(jnp.maximum(mx[0], mx[1]), mx[2])
        num = jnp.zeros((chunk, HEAD_DIM), F32)
        tot = jnp.zeros((chunk, HEAD_DIM), F32)
        for p in range(len(DILATIONS)):
            w = jnp.exp(mx[p] - top)
            num = num + w * acc[p]
            tot = tot + w * den[p]
        o_ref[pl.ds(pl.multiple_of(base, chunk), chunk), :] = (num / tot).astype(o_ref.dtype)
        return carry

    lax.fori_loop(0, seq // chunk, do_chunk, 0)


def _attention(z, o_prev, rope_cos, rope_sin, seq, batch_lo, batch_n, bq=128):
    n = z.shape[0]
    chunk = bq * DILATIONS[-1]
    assert seq % chunk == 0 and n % seq == 0
    zb = z.reshape(n // seq, seq, IN_WIDTH)

    def col_spec(off):
        return pl.BlockSpec((None, seq, HEAD_DIM),
                            lambda b, h: (batch_lo + b, 0, off + h))

    in_specs = [
        col_spec(0), col_spec(N_ATTN_HEADS), col_spec(2 * N_ATTN_HEADS),
        pl.BlockSpec((seq, HEAD_DIM), lambda b, h: (0, 0), pipeline_mode=pl.Buffered(1)),
        pl.BlockSpec((seq, HEAD_DIM), lambda b, h: (0, 0), pipeline_mode=pl.Buffered(1)),
    ]
    args = [zb, zb, zb, rope_cos, rope_sin]
    aliases = {}
    if o_prev is not None:
        in_specs.append(pl.BlockSpec(memory_space=pl.ANY))
        args.append(o_prev.reshape(n // seq, seq, ATTN_WIDTH))
        aliases = {5: 0}

    def kern(*refs):
        if o_prev is not None:
            refs = refs[:5] + refs[6:]
        _attn_kernel(*refs, seq=seq, bq=bq)

    out = pl.pallas_call(
        kern,
        grid=(batch_n, N_ATTN_HEADS),
        in_specs=in_specs,
        out_specs=pl.BlockSpec((None, seq, HEAD_DIM), lambda b, h: (batch_lo + b, 0, h)),
        out_shape=jax.ShapeDtypeStruct((n // seq, seq, ATTN_WIDTH), BF16),
        scratch_shapes=[
            pltpu.VMEM((seq, HEAD_DIM), F32),
            pltpu.VMEM((seq, HEAD_DIM), F32),
            pltpu.VMEM((seq, HEAD_DIM), F32),
            pltpu.VMEM((len(DILATIONS), chunk, HEAD_DIM), F32),
            pltpu.VMEM((len(DILATIONS), chunk, HEAD_DIM), F32),
            pltpu.VMEM((len(DILATIONS), chunk, HEAD_DIM), F32),
            pltpu.VMEM((len(DILATIONS), 3, bq, bq + 2 * HALF_NEIGHBOURS), F32),
        ],
        input_output_aliases=aliases,
        compiler_params=_cparams(("arbitrary", "arbitrary"), vmem=ATTN_VMEM_LIMIT),
    )(*args)
    return out.reshape(n, ATTN_WIDTH)


def _rope_tables(seq):
    half = HEAD_DIM // 2
    inv = jnp.power(ROPE_THETA, -jnp.arange(half, dtype=F32) * 2.0 / HEAD_DIM)
    ang = jnp.arange(seq, dtype=F32)[:, None] * inv[None, :]
    cos, sin = jnp.cos(ang), jnp.sin(ang)
    return jnp.concatenate([cos, cos], axis=-1), jnp.concatenate([-sin, sin], axis=-1)


def _dft_factors(seq):
    s1 = {2048: 32, 8192: 64}.get(seq)
    if s1 is None:
        s1 = 1
        while s1 * s1 < seq:
            s1 *= 2
        s1 = seq // s1 if (seq // s1) * s1 == seq else s1
    return s1, seq // s1


@functools.lru_cache(maxsize=None)
def _dft_tables(seq):
    s1, s2 = _dft_factors(seq)
    c = np.arange(FOURIER_GROUP)
    ang_c = 2.0 * np.pi * ((c[:, None] * c[None, :]) % FOURIER_GROUP) / FOURIER_GROUP
    norm = 1.0 / np.sqrt(float(seq) * FOURIER_GROUP)
    w_chan = np.concatenate([np.cos(ang_c), -np.sin(ang_c)], axis=1) * norm
    k1 = np.arange(s1)[None, :, None]
    n1 = np.arange(s1)[None, None, :]
    n2 = np.arange(s2)[:, None, None]
    ang1 = 2.0 * np.pi * ((k1 * (n2 + s2 * n1)) % seq) / seq
    stage1 = np.concatenate([np.cos(ang1), np.sin(ang1)], axis=1)
    j = np.arange(s2)
    ang2 = 2.0 * np.pi * ((j[:, None] * j[None, :]) % s2) / s2
    return (np.asarray(w_chan, np.float32), np.asarray(stage1, np.float32),
            np.asarray(np.cos(ang2), np.float32), np.asarray(np.sin(ang2), np.float32))


def _fourier_kernel(f_ref, wc_ref, st1_ref, c2_ref, s2_ref, wf_ref, o_ref,
                    zr_scr, zi_scr, tr_scr, ti_scr, y_scr, *, s1, s2):
    g = FOURIER_GROUP
    z = jnp.dot(f_ref[...], wc_ref[...], preferred_element_type=F32)
    zr_scr[...] = z[:, :g]
    zi_scr[...] = z[:, g:]

    def stage1(t, carry):
        n2s = [t * DFT_GROUP + u for u in range(DFT_GROUP)]
        zs = []
        for n2 in n2s:
            rows = pl.ds(n2, s1, stride=s2)
            zs.append(jnp.concatenate([zr_scr[rows, :], zi_scr[rows, :]], axis=1).astype(BF16))
        prs = [jnp.dot(st1_ref[n2], z, preferred_element_type=F32)
               for n2, z in zip(n2s, zs)]
        for n2, pr in zip(n2s, prs):
            dst = pl.ds(pl.multiple_of(n2 * s1, s1), s1)
            tr_scr[dst, :] = pr[:s1, :g] + pr[s1:, g:]
            ti_scr[dst, :] = pr[:s1, g:] - pr[s1:, :g]
        return carry

    lax.fori_loop(0, s2 // DFT_GROUP, stage1, 0)

    def stage2(t, carry):
        rows = [pl.ds(t * DFT_GROUP + u, s2, stride=s1) for u in range(DFT_GROUP)]
        ts = [(tr_scr[r, :].astype(BF16), ti_scr[r, :].astype(BF16)) for r in rows]
        ys = [jnp.dot(c2_ref[...], tr, preferred_element_type=F32)
              + jnp.dot(s2_ref[...], ti, preferred_element_type=F32) for tr, ti in ts]
        for r, y in zip(rows, ys):
            y_scr[r, :] = y
        return carry

    lax.fori_loop(0, s1 // DFT_GROUP, stage2, 0)
    o_ref[...] = jnp.dot(y_scr[...].astype(BF16), wf_ref[...],
                         preferred_element_type=F32).astype(o_ref.dtype)


def _fourier(z, o_prev, w_fourier, layer, seq, batch_lo, batch_n):
    n = z.shape[0]
    s1, s2 = _dft_factors(seq)
    w_chan, stage1, c2, sn2 = (jnp.asarray(t, BF16) for t in _dft_tables(seq))
    zb = z.reshape(n // seq, seq, IN_WIDTH)
    f_col = 3 * N_ATTN_HEADS
    in_specs = [
        pl.BlockSpec((None, seq, FOURIER_GROUP), lambda b, g: (batch_lo + b, 0, f_col + g)),
        pl.BlockSpec((FOURIER_GROUP, 2 * FOURIER_GROUP), lambda b, g: (0, 0)),
        pl.BlockSpec((s2, 2 * s1, s1), lambda b, g: (0, 0, 0)),
        pl.BlockSpec((s2, s2), lambda b, g: (0, 0)),
        pl.BlockSpec((s2, s2), lambda b, g: (0, 0)),
        pl.BlockSpec((None, None, FOURIER_GROUP, FOURIER_GROUP), lambda b, g: (layer, g, 0, 0)),
    ]
    args = [zb, w_chan, stage1, c2, sn2, w_fourier]
    aliases = {}
    if o_prev is not None:
        in_specs.append(pl.BlockSpec(memory_space=pl.ANY))
        args.append(o_prev.reshape(n // seq, seq, FOURIER_WIDTH))
        aliases = {6: 0}

    def kern(*refs):
        if o_prev is not None:
            refs = refs[:6] + refs[7:]
        _fourier_kernel(*refs, s1=s1, s2=s2)

    out = pl.pallas_call(
        kern,
        grid=(batch_n, N_FOURIER_GROUPS),
        in_specs=in_specs,
        out_specs=pl.BlockSpec((None, seq, FOURIER_GROUP), lambda b, g: (batch_lo + b, 0, g)),
        out_shape=jax.ShapeDtypeStruct((n // seq, seq, FOURIER_WIDTH), BF16),
        scratch_shapes=[pltpu.VMEM((seq, FOURIER_GROUP), F32)] * 5,
        input_output_aliases=aliases,
        compiler_params=_cparams(("parallel", "parallel")),
    )(*args)
    return out.reshape(n, FOURIER_WIDTH)


def _out_proj_kernel(oa_ref, of_ref, wa_ref, wf_ref, g_ref, *refs, src_rows, tm):
    ns = len(src_rows)
    o_ref = refs[ns]
    y = jnp.dot(oa_ref[...], wa_ref[...], preferred_element_type=F32)
    y = y + jnp.dot(of_ref[...], wf_ref[...], preferred_element_type=F32)
    gated = g_ref[...] * y

    def residual(x_ref):
        o_ref[...] = x_ref[...] + gated

    _with_token_tile(refs[:ns], src_rows, tm, residual)


def _out_proj(o_attn, o_f, xs, mod, w_out, layer, row_batch, tm=512, tn=D_MODEL):
    n = sum(x.shape[0] for x in xs)
    src_rows = tuple(x.shape[0] for x in xs)
    fblk = ATTN_WIDTH // FOURIER_WIDTH

    def gate_index(i, j):
        return (row_batch(i * tm) * 6 + 2, 0, j)

    return pl.pallas_call(
        functools.partial(_out_proj_kernel, src_rows=src_rows, tm=tm),
        grid=(n // tm, D_MODEL // tn),
        in_specs=[
            pl.BlockSpec((tm, ATTN_WIDTH), lambda i, j: (i, 0)),
            pl.BlockSpec((tm, FOURIER_WIDTH), lambda i, j: (i, 0)),
            pl.BlockSpec((None, ATTN_WIDTH, tn), lambda i, j: (layer, 0, j),
                         pipeline_mode=pl.Buffered(1)),
            pl.BlockSpec((None, FOURIER_WIDTH, tn), lambda i, j: (layer, fblk, j),
                         pipeline_mode=pl.Buffered(1)),
            pl.BlockSpec((None, 1, tn), gate_index),
        ] + _token_specs(xs, tm, tn, column_tiled=True),
        out_specs=pl.BlockSpec((tm, tn), lambda i, j: (i, j)),
        out_shape=jax.ShapeDtypeStruct((n, D_MODEL), F32),
        compiler_params=_cparams(("parallel", "parallel")),
    )(o_attn, o_f, w_out, w_out, mod, *xs)


def _swiglu_accumulate(h_scr, w1_ref, w3_ref, w2_ref, o_ref):
    h = h_scr[...]
    a = jnp.dot(h, w1_ref[...], preferred_element_type=F32)
    b = jnp.dot(h, w3_ref[...], preferred_element_type=F32)
    g = (_silu(a) * b).astype(BF16)
    for c in range(0, D_MODEL, DOWN_CHUNK):
        o_ref[:, c:c + DOWN_CHUNK] += jnp.dot(g, w2_ref[:, c:c + DOWN_CHUNK],
                                              preferred_element_type=F32)


def _ffn_dense_kernel(x_ref, g_ref, sh_ref, sc_ref, gate_ref, w1_ref, w3_ref, w2_ref, *refs,
                      n_side):
    side_src, o_ref, side_dst, h_scr = (refs[:n_side], refs[n_side],
                                        refs[n_side + 1:2 * n_side + 1], refs[2 * n_side + 1])
    j = pl.program_id(1)

    @pl.when(j == 0)
    def _():
        _norm_mod_rows(x_ref, g_ref, sh_ref, sc_ref, h_scr)
        o_ref[...] = jnp.zeros_like(o_ref)

    _swiglu_accumulate(h_scr, w1_ref, w3_ref, w2_ref, o_ref)
    for src, dst in zip(side_src, side_dst):
        dst[...] = src[...].astype(dst.dtype)

    @pl.when(j == pl.num_programs(1) - 1)
    def _():
        o_ref[...] = x_ref[...] + gate_ref[...] * o_ref[...]


def _side_block_rows(rows, n_steps):
    for b in range(2 * SUBLANES, rows + 1, 2 * SUBLANES):
        if rows % b == 0 and rows // b <= n_steps:
            return b
    raise ValueError("no row block")


def _ffn_dense(x, gain, mod, w1, w3, w2, layer, idx, row_batch, side=(), tm=512, tf=512):
    n = x.shape[0]
    d_ff = w1.shape[-1]
    nj = d_ff // tf
    n_steps = (n // tm) * nj
    side_specs, side_shapes = [], []
    for a in side:
        rows, cols = a.shape
        blk = _side_block_rows(rows, n_steps)

        def index(i, j, last=rows // blk - 1):
            return (jnp.minimum(i * nj + j, last), 0)

        side_specs.append(pl.BlockSpec((blk, cols), index))
        side_shapes.append(jax.ShapeDtypeStruct((rows, cols), BF16))
    out = pl.pallas_call(
        functools.partial(_ffn_dense_kernel, n_side=len(side)),
        grid=(n // tm, nj),
        in_specs=[
            pl.BlockSpec((tm, D_MODEL), lambda i, j: (i, 0)),
            pl.BlockSpec((None, 1, D_MODEL), lambda i, j: (layer, 0, 0)),
            _mod_spec(3, 6, row_batch, tm),
            _mod_spec(4, 6, row_batch, tm),
            _mod_spec(5, 6, row_batch, tm),
            pl.BlockSpec((None, D_MODEL, tf), lambda i, j: (idx, 0, j)),
            pl.BlockSpec((None, D_MODEL, tf), lambda i, j: (idx, 0, j)),
            pl.BlockSpec((None, tf, D_MODEL), lambda i, j: (idx, j, 0)),
        ] + side_specs,
        out_specs=[pl.BlockSpec((tm, D_MODEL), lambda i, j: (i, 0))] + side_specs,
        out_shape=[jax.ShapeDtypeStruct((n, D_MODEL), F32)] + side_shapes,
        scratch_shapes=[pltpu.VMEM((tm, D_MODEL), BF16)],
        compiler_params=_cparams(("arbitrary", "arbitrary")),
    )(x, gain, mod, mod, mod, w1, w3, w2, *side)
    return out[0], out[1:]


def _router_kernel(x_ref, g_ref, sh_ref, sc_ref, rw_ref, h_ref, idx_ref, gate_ref, rank_ref,
                   cnt_ref, cnt_scr, *, tm):
    @pl.when(pl.program_id(0) == 0)
    def _():
        cnt_scr[...] = jnp.zeros_like(cnt_scr)

    h = _norm_mod(x_ref[...], g_ref[...], sh_ref[...], sc_ref[...])
    h_ref[...] = h
    logits = jnp.dot(h, rw_ref[...], preferred_element_type=F32,
                     precision=lax.Precision.HIGHEST)
    lane = lax.broadcasted_iota(jnp.int32, (tm, LANES), 1).astype(F32)
    logits = jnp.where(lane < N_EXPERTS, logits, -jnp.inf)
    v1 = jnp.max(logits, axis=-1, keepdims=True)
    i1 = jnp.min(jnp.where(logits == v1, lane, float(LANES)), axis=-1, keepdims=True)
    rest = jnp.where(lane == i1, -jnp.inf, logits)
    v2 = jnp.max(rest, axis=-1, keepdims=True)
    i2 = jnp.min(jnp.where(rest == v2, lane, float(LANES)), axis=-1, keepdims=True)
    e2 = jnp.exp(v2 - v1)
    gate1 = 1.0 / (1.0 + e2)
    gate2 = e2 / (1.0 + e2)

    hot1 = (lane == i1).astype(BF16)
    hot2 = (lane == i2).astype(BF16)
    r_i = lax.broadcasted_iota(jnp.int32, (tm, tm), 0)
    c_i = lax.broadcasted_iota(jnp.int32, (tm, tm), 1)
    before = (c_i < r_i).astype(BF16)
    pre1 = jnp.dot(before, hot1, preferred_element_type=F32)
    pre2 = jnp.dot(before, hot2, preferred_element_type=F32)
    tot1 = jnp.sum(hot1.astype(F32), axis=0, keepdims=True)
    tot2 = jnp.sum(hot2.astype(F32), axis=0, keepdims=True)
    cnt = cnt_scr[...]
    rank1 = jnp.sum(jnp.where(lane == i1, pre1 + cnt, 0.0), axis=-1, keepdims=True)
    rank2 = jnp.sum(jnp.where(lane == i2, pre2 + cnt + tot1, 0.0), axis=-1, keepdims=True)
    cnt = cnt + tot1 + tot2
    cnt_scr[...] = cnt
    cnt_ref[...] = jnp.broadcast_to(cnt, cnt_ref.shape).astype(jnp.int32)

    idx_ref[...] = jnp.where(lane == 0, i1, jnp.where(lane == 1, i2, 0.0)).astype(jnp.int32)
    gate_ref[...] = jnp.where(lane == 0, gate1, jnp.where(lane == 1, gate2, 0.0))
    rank_ref[...] = jnp.where(lane == 0, rank1, jnp.where(lane == 1, rank2, 0.0)).astype(jnp.int32)


def _router(x, gain, mod, router_w, layer, row_batch, tm=512):
    n = x.shape[0]
    rw = jnp.zeros((D_MODEL, LANES), F32).at[:, :N_EXPERTS].set(router_w)
    tile = lambda i: (i, 0)
    return pl.pallas_call(
        functools.partial(_router_kernel, tm=tm),
        grid=(n // tm,),
        in_specs=[
            pl.BlockSpec((tm, D_MODEL), tile),
            pl.BlockSpec((None, 1, D_MODEL), lambda i: (layer, 0, 0)),
            _mod_spec(3, 6, row_batch, tm),
            _mod_spec(4, 6, row_batch, tm),
            pl.BlockSpec((D_MODEL, LANES), lambda i: (0, 0)),
        ],
        out_specs=[
            pl.BlockSpec((tm, D_MODEL), tile),
            pl.BlockSpec((tm, LANES), tile),
            pl.BlockSpec((tm, LANES), tile),
            pl.BlockSpec((tm, LANES), tile),
            pl.BlockSpec((8, LANES), lambda i: (0, 0)),
        ],
        out_shape=[
            jax.ShapeDtypeStruct((n, D_MODEL), F32),
            jax.ShapeDtypeStruct((n, LANES), jnp.int32),
            jax.ShapeDtypeStruct((n, LANES), F32),
            jax.ShapeDtypeStruct((n, LANES), jnp.int32),
            jax.ShapeDtypeStruct((8, LANES), jnp.int32),
        ],
        scratch_shapes=[pltpu.VMEM((1, LANES), F32)],
        compiler_params=_cparams(("arbitrary",)),
    )(x, gain, mod, mod, rw)


def _dispatch_kernel(pos_ref, pad_ref, h_ref, xs_ref, zeros, sem, zero_sem, *, tt, pad_rows):
    base = pl.program_id(0) * tt

    @pl.when(pl.program_id(0) == 0)
    def _():
        zeros[...] = jnp.zeros_like(zeros)

        def fill(e):
            start = pl.multiple_of(pad_ref[e], SUBLANES)
            return pltpu.make_async_copy(zeros, xs_ref.at[pl.ds(start, pad_rows), :], zero_sem)

        for e in range(N_EXPERTS):
            fill(e).start()
        for e in range(N_EXPERTS):
            fill(e).wait()

    def issue(t, c):
        for slot in range(2):
            pltpu.make_async_copy(h_ref.at[pl.ds(t, 1), :],
                                  xs_ref.at[pl.ds(pos_ref[2 * (base + t) + slot], 1), :],
                                  sem).start()
        return c

    lax.fori_loop(0, tt, issue, 0, unroll=ROW_DMA_UNROLL)
    for slot in range(2):
        pltpu.make_async_copy(h_ref, xs_ref.at[pl.ds(0, tt), :], sem).wait()


def _dispatch(h, pos, pad_start, rows, pad_rows, tt=256):
    n = h.shape[0]
    return pl.pallas_call(
        functools.partial(_dispatch_kernel, tt=tt, pad_rows=pad_rows),
        grid_spec=pltpu.PrefetchScalarGridSpec(
            num_scalar_prefetch=2,
            grid=(n // tt,),
            in_specs=[pl.BlockSpec((tt, D_MODEL), lambda i, pos, pad: (i, 0))],
            out_specs=pl.BlockSpec(memory_space=pl.ANY),
            scratch_shapes=[pltpu.VMEM((pad_rows, D_MODEL), F32),
                            pltpu.SemaphoreType.DMA(()), pltpu.SemaphoreType.DMA(())],
        ),
        out_shape=jax.ShapeDtypeStruct((rows, D_MODEL), F32),
        compiler_params=_cparams(("arbitrary",), has_side_effects=True),
    )(pos, pad_start, h)


def _moe_ffn_kernel(te_ref, nv_ref, x_ref, w1_ref, w3_ref, w2_ref, o_ref, h_scr):
    i = pl.program_id(0)
    j = pl.program_id(1)

    @pl.when(i < nv_ref[0])
    def _():
        @pl.when(j == 0)
        def _():
            h_scr[...] = x_ref[...].astype(BF16)
            o_ref[...] = jnp.zeros_like(o_ref)

        _swiglu_accumulate(h_scr, w1_ref, w3_ref, w2_ref, o_ref)


def _moe_ffn(xs, tile_expert, n_valid, w1, w3, w2, idx, tm, tf=1024):
    rows = tile_expert.shape[0] * tm
    d_ff = w1.shape[-1]
    assert d_ff % tf == 0 and rows <= xs.shape[0]
    nj = d_ff // tf

    def row_index(i, j, te, nv):
        return (jnp.maximum(jnp.minimum(i, nv[0] - 1), 0), 0)

    def up_index(i, j, te, nv):
        return (idx, te[i], 0, jnp.where(i < nv[0], j, nj - 1))

    def down_index(i, j, te, nv):
        return (idx, te[i], jnp.where(i < nv[0], j, nj - 1), 0)

    return pl.pallas_call(
        _moe_ffn_kernel,
        grid_spec=pltpu.PrefetchScalarGridSpec(
            num_scalar_prefetch=2,
            grid=(rows // tm, nj),
            in_specs=[
                pl.BlockSpec((tm, D_MODEL), row_index),
                pl.BlockSpec((None, None, D_MODEL, tf), up_index),
                pl.BlockSpec((None, None, D_MODEL, tf), up_index),
                pl.BlockSpec((None, None, tf, D_MODEL), down_index),
            ],
            out_specs=pl.BlockSpec((tm, D_MODEL), row_index),
            scratch_shapes=[pltpu.VMEM((tm, D_MODEL), BF16)],
        ),
        out_shape=jax.ShapeDtypeStruct((rows, D_MODEL), F32),
        compiler_params=_cparams(("arbitrary", "arbitrary"), vmem=LARGE_VMEM_LIMIT),
    )(tile_expert, n_valid, xs, w1, w3, w2)


def _combine_kernel(pos_ref, ys_ref, x_ref, gate_ref, g2_ref, o_ref, buf, sem, *, tt):
    base = pl.program_id(0) * tt

    def issue(t, c):
        for slot in range(2):
            pltpu.make_async_copy(ys_ref.at[pl.ds(pos_ref[2 * (base + t) + slot], 1), :],
                                  buf.at[slot, pl.ds(t, 1), :], sem).start()
        return c

    lax.fori_loop(0, tt, issue, 0, unroll=ROW_DMA_UNROLL)
    for slot in range(2):
        pltpu.make_async_copy(ys_ref.at[pl.ds(0, tt), :], buf.at[slot], sem).wait()
    gates = gate_ref[...]
    f = gates[:, 0:1] * buf[0] + gates[:, 1:2] * buf[1]
    o_ref[...] = x_ref[...] + g2_ref[...] * f


def _combine_final_kernel(pos_ref, ys_ref, x_ref, gate_ref, g2_ref, gf_ref, shf_ref, scf_ref,
                          *refs, out_rows, tt):
    out_refs, (x_new, buf, sem) = refs[:len(out_rows)], refs[len(out_rows):]
    _combine_kernel(pos_ref, ys_ref, x_ref, gate_ref, g2_ref, x_new, buf, sem, tt=tt)
    y = _norm_mod(x_new[...], gf_ref[...], shf_ref[...], scf_ref[...])
    i, lo = pl.program_id(0), 0
    for o_ref, rows in zip(out_refs, out_rows):
        nt = rows // tt

        @pl.when((i >= lo) & (i < lo + nt))
        def _(o_ref=o_ref):
            o_ref[...] = y

        lo += nt


def _combine(ys, pos, x, gates, mod, row_batch, final=None, tt=256):
    n = x.shape[0]
    in_specs = [
        pl.BlockSpec(memory_space=pl.ANY),
        pl.BlockSpec((tt, D_MODEL), lambda i, pos: (i, 0)),
        pl.BlockSpec((tt, LANES), lambda i, pos: (i, 0)),
        _mod_spec(5, 6, row_batch, tt),
    ]
    args = [pos, ys, x, gates, mod]
    scratch = [pltpu.VMEM((2, tt, D_MODEL), F32), pltpu.SemaphoreType.DMA(())]
    if final is None:
        body = functools.partial(_combine_kernel, tt=tt)
        out_specs = pl.BlockSpec((tt, D_MODEL), lambda i, pos: (i, 0))
        out_shape = jax.ShapeDtypeStruct((n, D_MODEL), F32)
    else:
        gain, fmod, out_rows = final
        body = functools.partial(_combine_final_kernel, out_rows=out_rows, tt=tt)
        in_specs += [pl.BlockSpec((1, D_MODEL), lambda i, pos: (0, 0)),
                     _mod_spec(0, 2, row_batch, tt), _mod_spec(1, 2, row_batch, tt)]
        args += [gain, fmod, fmod]
        out_specs, out_shape, lo = [], [], 0
        for rows in out_rows:
            nt = rows // tt

            def index(i, pos, lo=lo, nt=nt):
                return (jnp.clip(i - lo, 0, nt - 1), 0)

            out_specs.append(pl.BlockSpec((tt, D_MODEL), index))
            out_shape.append(jax.ShapeDtypeStruct((rows, D_MODEL), F32))
            lo += nt
        scratch = [pltpu.VMEM((tt, D_MODEL), F32)] + scratch
    return pl.pallas_call(
        body,
        grid_spec=pltpu.PrefetchScalarGridSpec(
            num_scalar_prefetch=1, grid=(n // tt,), in_specs=in_specs, out_specs=out_specs,
            scratch_shapes=scratch),
        out_shape=out_shape,
        compiler_params=_cparams(("arbitrary",)),
    )(*args)


def _moe_layer(x, gain, mod, router_w, w1, w3, w2, layer, idx, row_batch, final=None, tm=512):
    n = x.shape[0]
    h, top_idx, gates, rank, counts = _router(x, gain, mod, router_w[idx], layer, row_batch)
    counts = counts[0, :N_EXPERTS]
    padded = ((counts + tm - 1) // tm) * tm
    ends = jnp.cumsum(padded)
    starts = ends - padded
    experts = top_idx[:, :2]
    pos = (starts[experts] + rank[:, :2]).reshape(-1).astype(jnp.int32)
    n_tiles = (2 * n) // tm + N_EXPERTS
    tile_start = jnp.arange(n_tiles, dtype=jnp.int32) * tm
    tile_expert = jnp.minimum(jnp.sum(tile_start[:, None] >= ends[None, :], axis=1),
                              N_EXPERTS - 1).astype(jnp.int32)
    n_valid = (ends[-1:] // tm).astype(jnp.int32)
    pad_start = ((starts + counts) // SUBLANES * SUBLANES).astype(jnp.int32)
    xs = _dispatch(h, pos, pad_start, (n_tiles + 2) * tm, tm + SUBLANES)
    ys = _moe_ffn(xs, tile_expert, n_valid, w1, w3, w2, idx, tm)
    return _combine(ys, pos, x, gates, mod, row_batch, final)


def _final_kernel(x_ref, g_ref, sh_ref, sc_ref, o_ref):
    o_ref[...] = _norm_mod(x_ref[...], g_ref[...], sh_ref[...], sc_ref[...])


def _final(x, gain, fmod, row_lo, rows, row_batch, tm=512):
    off = row_lo // tm

    def mod_spec(which):
        return pl.BlockSpec((None, 1, D_MODEL),
                            lambda i: (row_batch((i + off) * tm) * 2 + which, 0, 0))

    return pl.pallas_call(
        _final_kernel,
        grid=(rows // tm,),
        in_specs=[
            pl.BlockSpec((tm, D_MODEL), lambda i: (i + off, 0)),
            pl.BlockSpec((1, D_MODEL), lambda i: (0, 0)),
            mod_spec(0),
            mod_spec(1),
        ],
        out_specs=pl.BlockSpec((tm, D_MODEL), lambda i: (i, 0)),
        out_shape=jax.ShapeDtypeStruct((rows, D_MODEL), F32),
        compiler_params=_cparams(("parallel",)),
    )(x, gain, fmod, fmod)


def kernel(x_prompt, x_sample, c_prompt, c_sample, w_mod, b_mod, norm_mix, w_in, w_fourier, w_out,
           norm_ffn, dense_w1, dense_w3, dense_w2, router_w, moe_w1, moe_w3, moe_w2,
           w_final_mod, b_final_mod, norm_final):
    batch, seq_p, d = x_prompt.shape
    batch_s, seq_s, _ = x_sample.shape
    n_p, n_s = batch * seq_p, batch_s * seq_s
    n = n_p + n_s
    depth = w_mod.shape[0]
    assert d == D_MODEL and batch_s == 1 and n_p % seq_s == 0 and batch + batch_s <= MOD_ROWS

    def row_batch(row):
        return jnp.minimum(row // seq_p, batch)

    x = (x_prompt.reshape(n_p, d), x_sample.reshape(n_s, d))
    c_all = jnp.zeros((MOD_ROWS, d), F32).at[:batch].set(c_prompt).at[batch:batch + 1].set(c_sample)

    mod = _modulation(c_all, w_mod, b_mod)
    fmod = _modulation(c_all, w_final_mod[None], b_final_mod[None])[0]
    fmod = fmod.reshape(MOD_ROWS * 2, 1, d)

    gain_mix = norm_mix.reshape(depth, 1, d)
    gain_ffn = norm_ffn.reshape(depth, 1, d)
    gain_final = norm_final.reshape(1, d)
    w_in_b = w_in.astype(BF16)
    w_out_b = w_out.astype(BF16)
    w_fourier_b = w_fourier.astype(BF16)
    dense_b = [w.astype(BF16) for w in (dense_w1, dense_w3, dense_w2)]
    moe_f32 = (moe_w1, moe_w3, moe_w2)
    moe_b = None
    rope_p = _rope_tables(seq_p)
    rope_s = _rope_tables(seq_s)

    for l in range(depth):
        mod_l = mod[l].reshape(MOD_ROWS * 6, 1, d)
        z = _in_proj(x, gain_mix, mod_l, w_in_b, l, row_batch)
        o_attn = _attention(z, None, *rope_p, seq_p, 0, batch)
        o_attn = _attention(z, o_attn, *rope_s, seq_s, n_p // seq_s, batch_s)
        o_f = _fourier(z, None, w_fourier_b, l, seq_p, 0, batch)
        o_f = _fourier(z, o_f, w_fourier_b, l, seq_s, n_p // seq_s, batch_s)
        x = _out_proj(o_attn, o_f, x, mod_l, w_out_b, l, row_batch)
        if l % 2 == 0:
            side = () if moe_b is not None or not moe_w1.size else tuple(
                w.reshape(-1, w.shape[-1]) for w in moe_f32)
            x, cast = _ffn_dense(x, gain_ffn, mod_l, *dense_b, l, l // 2, row_batch, side)
            x = (x,)
            if side:
                moe_b = [c.reshape(w.shape) for c, w in zip(cast, moe_f32)]
        else:
            if moe_b is None:
                moe_b = [w.astype(BF16) for w in moe_f32]
            final = (gain_final, fmod, (n_p, n_s)) if l == depth - 1 else None
            x = _moe_layer(x, gain_ffn, mod_l, router_w, *moe_b, l, l // 2, row_batch, final)
            x = tuple(x) if final else (x,)

    if depth % 2 == 0:
        y_p, y_s = x
    else:
        y_p = _final(x[0], gain_final, fmod, 0, n_p, row_batch)
        y_s = _final(x[0], gain_final, fmod, n_p, n_s, row_batch)
    return (y_p.reshape(batch, seq_p, d), y_s.reshape(batch_s, seq_s, d))
```

```python
import functools

import numpy as np
import jax
import jax.numpy as jnp
from jax import lax
from jax.experimental import pallas as pl
from jax.experimental.pallas import tpu as pltpu

F32 = jnp.float32
BF16 = jnp.bfloat16

D_MODEL = 2048
HEAD_DIM = 128
N_ATTN_HEADS = 12
ATTN_WIDTH = N_ATTN_HEADS * HEAD_DIM
N_FOURIER_GROUPS = 4
FOURIER_GROUP = 128
FOURIER_WIDTH = N_FOURIER_GROUPS * FOURIER_GROUP
IN_WIDTH = 3 * ATTN_WIDTH + FOURIER_WIDTH
DILATIONS = (1, 4, 16)
HALF_NEIGHBOURS = 64
ROPE_THETA = 10000.0
N_EXPERTS = 8
EPS = 1e-6
NEG_INF = -1e30
LANES = 128
SUBLANES = 8
MOD_ROWS = 16

VMEM_LIMIT = 48 * 1024 * 1024
ATTN_VMEM_LIMIT = 58 * 1024 * 1024
BLOCK_GROUP = 4
DFT_GROUP = 8
NORM_ROWS = 16
ROW_DMA_UNROLL = 8
DOWN_CHUNK = 512
LARGE_VMEM_LIMIT = 56 * 1024 * 1024


def _cparams(sem, vmem=VMEM_LIMIT, **kw):
    return pltpu.CompilerParams(dimension_semantics=sem, vmem_limit_bytes=vmem, **kw)


def _norm_mod(x, g, sh, sc):
    ms = jnp.mean(x * x, axis=-1, keepdims=True)
    return (x * lax.rsqrt(ms + EPS)) * g * (1.0 + sc) + sh


def _norm_mod_rows(x_ref, g_ref, sh_ref, sc_ref, out_ref):
    g, sh, sc = g_ref[...], sh_ref[...], sc_ref[...]

    def body(c, carry):
        rows = pl.ds(pl.multiple_of(c * NORM_ROWS, NORM_ROWS), NORM_ROWS)
        out_ref[rows, :] = _norm_mod(x_ref[rows, :], g, sh, sc).astype(out_ref.dtype)
        return carry

    lax.fori_loop(0, x_ref.shape[0] // NORM_ROWS, body, 0, unroll=8)


def _silu(a):
    return a / (1.0 + jnp.exp(-a))


def _mod_kernel(c_ref, w_ref, b_ref, o_ref):
    cs = _silu(c_ref[...]).astype(BF16)
    o_ref[...] = jnp.dot(cs, w_ref[...].astype(BF16), preferred_element_type=F32) + b_ref[...]


def _modulation(c_all, w, b, tn=1024):
    nl, d, n = w.shape
    return pl.pallas_call(
        _mod_kernel,
        grid=(nl, n // tn),
        in_specs=[
            pl.BlockSpec((MOD_ROWS, d), lambda l, j: (0, 0)),
            pl.BlockSpec((None, d, tn), lambda l, j: (l, 0, j)),
            pl.BlockSpec((None, 1, tn), lambda l, j: (l, 0, j)),
        ],
        out_specs=pl.BlockSpec((None, MOD_ROWS, tn), lambda l, j: (l, 0, j)),
        out_shape=jax.ShapeDtypeStruct((nl, MOD_ROWS, n), F32),
        compiler_params=_cparams(("parallel", "parallel")),
    )(c_all, w, b.reshape(nl, 1, n))


def _mod_spec(which, n_vec, row_batch, tm):
    def index(i, *_):
        return (row_batch(i * tm) * n_vec + which, 0, 0)
    return pl.BlockSpec((None, 1, D_MODEL), index)


def _token_specs(xs, tm, width, column_tiled=False):
    specs, lo = [], 0
    for x in xs:
        nt = x.shape[0] // tm

        def index(i, j=0, *_, lo=lo, nt=nt):
            return (jnp.clip(i - lo, 0, nt - 1), j if column_tiled else 0)

        specs.append(pl.BlockSpec((tm, width), index))
        lo += nt
    return specs


def _with_token_tile(x_refs, src_rows, tm, fn):
    if len(x_refs) == 1:
        fn(x_refs[0])
        return
    i, lo = pl.program_id(0), 0
    for x_ref, rows in zip(x_refs, src_rows):
        nt = rows // tm
        pl.when((i >= lo) & (i < lo + nt))(functools.partial(fn, x_ref))
        lo += nt


def _in_proj_kernel(*refs, src_rows, tm):
    ns = len(src_rows)
    g_ref, sh_ref, sc_ref, w_ref, o_ref, h_scr = refs[ns:]

    def prepare(x_ref):
        _norm_mod_rows(x_ref, g_ref, sh_ref, sc_ref, h_scr)

    @pl.when(pl.program_id(1) == 0)
    def _():
        _with_token_tile(refs[:ns], src_rows, tm, prepare)

    o_ref[...] = jnp.dot(h_scr[...], w_ref[...], preferred_element_type=F32).astype(o_ref.dtype)


def _in_proj(xs, gain, mod, w_in, layer, row_batch, tn=1280):
    tm = 1024 if len(xs) == 1 else 512
    n = sum(x.shape[0] for x in xs)
    src_rows = tuple(x.shape[0] for x in xs)
    return pl.pallas_call(
        functools.partial(_in_proj_kernel, src_rows=src_rows, tm=tm),
        grid=(n // tm, IN_WIDTH // tn),
        in_specs=_token_specs(xs, tm, D_MODEL) + [
            pl.BlockSpec((None, 1, D_MODEL), lambda i, j: (layer, 0, 0)),
            _mod_spec(0, 6, row_batch, tm),
            _mod_spec(1, 6, row_batch, tm),
            pl.BlockSpec((None, D_MODEL, tn), lambda i, j: (layer, 0, j)),
        ],
        out_specs=pl.BlockSpec((tm, tn), lambda i, j: (i, j)),
        out_shape=jax.ShapeDtypeStruct((n, IN_WIDTH), BF16),
        scratch_shapes=[pltpu.VMEM((tm, D_MODEL), BF16)],
        compiler_params=_cparams(("parallel", "arbitrary")),
    )(*xs, gain, mod, mod, w_in)


def _attn_kernel(q_ref, k_ref, v_ref, cos_ref, sin_ref, o_ref,
                 qr, kr, vr, acc, mx, den, bias, *, seq, bq):
    chunk = bq * DILATIONS[-1]

    @pl.when((pl.program_id(0) == 0) & (pl.program_id(1) == 0))
    def _():
        for p, r in enumerate(DILATIONS):
            kw = min(bq + 2 * HALF_NEIGHBOURS, seq // r)
            rel = (lax.broadcasted_iota(jnp.int32, (bq, kw), 1)
                   - lax.broadcasted_iota(jnp.int32, (bq, kw), 0))
            for case, off in enumerate((0, -HALF_NEIGHBOURS, bq - kw)):
                bias[p, case, :, :kw] = jnp.where(jnp.abs(rel + off) <= HALF_NEIGHBOURS,
                                                  0.0, NEG_INF)

    cos = cos_ref[...]
    sin = sin_ref[...]
    q = q_ref[...].astype(F32)
    qr[...] = (q * cos + pltpu.roll(q, HEAD_DIM // 2, 1) * sin) * (1.0 / np.sqrt(HEAD_DIM))
    k = k_ref[...].astype(F32)
    kr[...] = k * cos + pltpu.roll(k, HEAD_DIM // 2, 1) * sin
    vr[...] = v_ref[...].astype(F32)

    def do_chunk(c, carry):
        base = c * chunk
        for p, r in enumerate(DILATIONS):
            sub_len = seq // r
            kw = min(bq + 2 * HALF_NEIGHBOURS, sub_len)
            nb = chunk // (r * bq)
            group = BLOCK_GROUP * (bq + 2 * HALF_NEIGHBOURS) // kw

            def do_blocks(t, carry2, p=p, r=r, sub_len=sub_len, kw=kw, nb=nb, group=group):
                work = []
                for g in range(group):
                    tg = t * group + g
                    m = tg // nb
                    i = tg % nb
                    q0 = (c * nb + i) * bq
                    k0 = jnp.clip(q0 - HALF_NEIGHBOURS, 0, sub_len - kw)
                    qb = qr[pl.ds(m + r * q0, bq, stride=r), :].astype(BF16)
                    kb = kr[pl.ds(m + r * k0, kw, stride=r), :].astype(BF16)
                    vb = vr[pl.ds(m + r * k0, kw, stride=r), :].astype(BF16)
                    dst = pl.ds(m + r * (i * bq), bq, stride=r)
                    work.append((qb, kb, vb, k0 - q0, dst))
                scores = [lax.dot_general(qb, kb, (((1,), (1,)), ((), ())),
                                          preferred_element_type=F32)
                          for qb, kb, _, _, _ in work]
                weights = []
                for s, (_, _, _, off, dst) in zip(scores, work):
                    case = jnp.where(off == 0, 0, jnp.where(off == -HALF_NEIGHBOURS, 1, 2))
                    s = s + bias[p, case, :, :kw]
                    smax = jnp.max(s, axis=-1, keepdims=True)
                    e = jnp.exp(s - smax)
                    mx[p, dst, :] = jnp.broadcast_to(smax, (bq, HEAD_DIM))
                    den[p, dst, :] = jnp.broadcast_to(jnp.sum(e, axis=-1, keepdims=True),
                                                      (bq, HEAD_DIM))
                    weights.append(e.astype(BF16))
                for e, (_, _, vb, _, dst) in zip(weights, work):
                    acc[p, dst, :] = jnp.dot(e, vb, preferred_element_type=F32)
                return carry2

            lax.fori_loop(0, (r * nb) // group, do_blocks, 0)

        top = jnp.maximum(jnp.maximum(mx[0], mx[1]), mx[2])
        num = jnp.zeros((chunk, HEAD_DIM), F32)
        tot = jnp.zeros((chunk, HEAD_DIM), F32)
        for p in range(len(DILATIONS)):
            w = jnp.exp(mx[p] - top)
            num = num + w * acc[p]
            tot = tot + w * den[p]
        o_ref[pl.ds(pl.multiple_of(base, chunk), chunk), :] = (num / tot).astype(o_ref.dtype)
        return carry

    lax.fori_loop(0, seq // chunk, do_chunk, 0)


def _attention(z, o_prev, rope_cos, rope_sin, seq, batch_lo, batch_n, side=(), bq=128):
    n = z.shape[0]
    chunk = bq * DILATIONS[-1]
    assert seq % chunk == 0 and n % seq == 0
    zb = z.reshape(n // seq, seq, IN_WIDTH)

    def col_spec(off):
        return pl.BlockSpec((None, seq, HEAD_DIM),
                            lambda b, h: (batch_lo + b, 0, off + h))

    in_specs = [
        col_spec(0), col_spec(N_ATTN_HEADS), col_spec(2 * N_ATTN_HEADS),
        pl.BlockSpec((seq, HEAD_DIM), lambda b, h: (0, 0), pipeline_mode=pl.Buffered(1)),
        pl.BlockSpec((seq, HEAD_DIM), lambda b, h: (0, 0), pipeline_mode=pl.Buffered(1)),
    ]
    args = [zb, zb, zb, rope_cos, rope_sin]
    aliases = {}
    if o_prev is not None:
        in_specs.append(pl.BlockSpec(memory_space=pl.ANY))
        args.append(o_prev.reshape(n // seq, seq, ATTN_WIDTH))
        aliases = {5: 0}
    n_main = len(args)
    side_specs, side_shapes = _side_cast_specs(side, batch_n * N_ATTN_HEADS,
                                               lambda b, h: b * N_ATTN_HEADS + h)

    def kern(*refs):
        ins, rest = refs[:n_main + len(side)], refs[n_main + len(side):]
        o_ref, side_dst, scratch = rest[0], rest[1:1 + len(side)], rest[1 + len(side):]
        _attn_kernel(*ins[:5], o_ref, *scratch, seq=seq, bq=bq)
        for src, dst in zip(ins[n_main:], side_dst):
            dst[...] = src[...].astype(dst.dtype)

    out = pl.pallas_call(
        kern,
        grid=(batch_n, N_ATTN_HEADS),
        in_specs=in_specs + side_specs,
        out_specs=[pl.BlockSpec((None, seq, HEAD_DIM), lambda b, h: (batch_lo + b, 0, h))]
        + side_specs,
        out_shape=[jax.ShapeDtypeStruct((n // seq, seq, ATTN_WIDTH), BF16)] + side_shapes,
        scratch_shapes=[
            pltpu.VMEM((seq, HEAD_DIM), F32),
            pltpu.VMEM((seq, HEAD_DIM), F32),
            pltpu.VMEM((seq, HEAD_DIM), F32),
            pltpu.VMEM((len(DILATIONS), chunk, HEAD_DIM), F32),
            pltpu.VMEM((len(DILATIONS), chunk, HEAD_DIM), F32),
            pltpu.VMEM((len(DILATIONS), chunk, HEAD_DIM), F32),
            pltpu.VMEM((len(DILATIONS), 3, bq, bq + 2 * HALF_NEIGHBOURS), F32),
        ],
        input_output_aliases=aliases,
        compiler_params=_cparams(("arbitrary", "arbitrary"), vmem=ATTN_VMEM_LIMIT),
    )(*args, *side)
    return out[0].reshape(n, ATTN_WIDTH), out[1:]


def _rope_tables(seq):
    half = HEAD_DIM // 2
    inv = jnp.power(ROPE_THETA, -jnp.arange(half, dtype=F32) * 2.0 / HEAD_DIM)
    ang = jnp.arange(seq, dtype=F32)[:, None] * inv[None, :]
    cos, sin = jnp.cos(ang), jnp.sin(ang)
    return jnp.concatenate([cos, cos], axis=-1), jnp.concatenate([-sin, sin], axis=-1)


def _dft_factors(seq):
    s1 = {2048: 32, 8192: 64}.get(seq)
    if s1 is None:
        s1 = 1
        while s1 * s1 < seq:
            s1 *= 2
        s1 = seq // s1 if (seq // s1) * s1 == seq else s1
    return s1, seq // s1


@functools.lru_cache(maxsize=None)
def _dft_tables(seq):
    s1, s2 = _dft_factors(seq)
    c = np.arange(FOURIER_GROUP)
    ang_c = 2.0 * np.pi * ((c[:, None] * c[None, :]) % FOURIER_GROUP) / FOURIER_GROUP
    norm = 1.0 / np.sqrt(float(seq) * FOURIER_GROUP)
    w_chan = np.concatenate([np.cos(ang_c), -np.sin(ang_c)], axis=1) * norm
    k1 = np.arange(s1)[None, :, None]
    n1 = np.arange(s1)[None, None, :]
    n2 = np.arange(s2)[:, None, None]
    ang1 = 2.0 * np.pi * ((k1 * (n2 + s2 * n1)) % seq) / seq
    stage1 = np.concatenate([np.cos(ang1), np.sin(ang1)], axis=1)
    j = np.arange(s2)
    ang2 = 2.0 * np.pi * ((j[:, None] * j[None, :]) % s2) / s2
    return (np.asarray(w_chan, np.float32), np.asarray(stage1, np.float32),
            np.asarray(np.cos(ang2), np.float32), np.asarray(np.sin(ang2), np.float32))


def _fourier_kernel(f_ref, wc_ref, st1_ref, c2_ref, s2_ref, wf_ref, o_ref,
                    zr_scr, zi_scr, tr_scr, ti_scr, y_scr, *, s1, s2):
    g = FOURIER_GROUP
    z = jnp.dot(f_ref[...], wc_ref[...], preferred_element_type=F32)
    zr_scr[...] = z[:, :g]
    zi_scr[...] = z[:, g:]

    def stage1(t, carry):
        n2s = [t * DFT_GROUP + u for u in range(DFT_GROUP)]
        zs = []
        for n2 in n2s:
            rows = pl.ds(n2, s1, stride=s2)
            zs.append(jnp.concatenate([zr_scr[rows, :], zi_scr[rows, :]], axis=1).astype(BF16))
        prs = [jnp.dot(st1_ref[n2], z, preferred_element_type=F32)
               for n2, z in zip(n2s, zs)]
        for n2, pr in zip(n2s, prs):
            dst = pl.ds(pl.multiple_of(n2 * s1, s1), s1)
            tr_scr[dst, :] = pr[:s1, :g] + pr[s1:, g:]
            ti_scr[dst, :] = pr[:s1, g:] - pr[s1:, :g]
        return carry

    lax.fori_loop(0, s2 // DFT_GROUP, stage1, 0)

    def stage2(t, carry):
        rows = [pl.ds(t * DFT_GROUP + u, s2, stride=s1) for u in range(DFT_GROUP)]
        ts = [(tr_scr[r, :].astype(BF16), ti_scr[r, :].astype(BF16)) for r in rows]
        ys = [jnp.dot(c2_ref[...], tr, preferred_element_type=F32)
              + jnp.dot(s2_ref[...], ti, preferred_element_type=F32) for tr, ti in ts]
        for r, y in zip(rows, ys):
            y_scr[r, :] = y
        return carry

    lax.fori_loop(0, s1 // DFT_GROUP, stage2, 0)
    o_ref[...] = jnp.dot(y_scr[...].astype(BF16), wf_ref[...],
                         preferred_element_type=F32).astype(o_ref.dtype)


def _fourier(z, o_prev, w_fourier, layer, seq, batch_lo, batch_n):
    n = z.shape[0]
    s1, s2 = _dft_factors(seq)
    w_chan, stage1, c2, sn2 = (jnp.asarray(t, BF16) for t in _dft_tables(seq))
    zb = z.reshape(n // seq, seq, IN_WIDTH)
    f_col = 3 * N_ATTN_HEADS
    in_specs = [
        pl.BlockSpec((None, seq, FOURIER_GROUP), lambda b, g: (batch_lo + b, 0, f_col + g)),
        pl.BlockSpec((FOURIER_GROUP, 2 * FOURIER_GROUP), lambda b, g: (0, 0)),
        pl.BlockSpec((s2, 2 * s1, s1), lambda b, g: (0, 0, 0)),
        pl.BlockSpec((s2, s2), lambda b, g: (0, 0)),
        pl.BlockSpec((s2, s2), lambda b, g: (0, 0)),
        pl.BlockSpec((None, None, FOURIER_GROUP, FOURIER_GROUP), lambda b, g: (layer, g, 0, 0)),
    ]
    args = [zb, w_chan, stage1, c2, sn2, w_fourier]
    aliases = {}
    if o_prev is not None:
        in_specs.append(pl.BlockSpec(memory_space=pl.ANY))
        args.append(o_prev.reshape(n // seq, seq, FOURIER_WIDTH))
        aliases = {6: 0}

    def kern(*refs):
        if o_prev is not None:
            refs = refs[:6] + refs[7:]
        _fourier_kernel(*refs, s1=s1, s2=s2)

    out = pl.pallas_call(
        kern,
        grid=(batch_n, N_FOURIER_GROUPS),
        in_specs=in_specs,
        out_specs=pl.BlockSpec((None, seq, FOURIER_GROUP), lambda b, g: (batch_lo + b, 0, g)),
        out_shape=jax.ShapeDtypeStruct((n // seq, seq, FOURIER_WIDTH), BF16),
        scratch_shapes=[pltpu.VMEM((seq, FOURIER_GROUP), F32)] * 5,
        input_output_aliases=aliases,
        compiler_params=_cparams(("parallel", "parallel")),
    )(*args)
    return out.reshape(n, FOURIER_WIDTH)


def _out_proj_kernel(oa_ref, of_ref, wa_ref, wf_ref, g_ref, *refs, src_rows, tm):
    ns = len(src_rows)
    o_ref = refs[ns]
    y = jnp.dot(oa_ref[...], wa_ref[...], preferred_element_type=F32)
    y = y + jnp.dot(of_ref[...], wf_ref[...], preferred_element_type=F32)
    gated = g_ref[...] * y

    def residual(x_ref):
        o_ref[...] = x_ref[...] + gated

    _with_token_tile(refs[:ns], src_rows, tm, residual)


def _out_proj(o_attn, o_f, xs, mod, w_out, layer, row_batch, tm=512, tn=D_MODEL):
    n = sum(x.shape[0] for x in xs)
    src_rows = tuple(x.shape[0] for x in xs)
    fblk = ATTN_WIDTH // FOURIER_WIDTH

    def gate_index(i, j):
        return (row_batch(i * tm) * 6 + 2, 0, j)

    return pl.pallas_call(
        functools.partial(_out_proj_kernel, src_rows=src_rows, tm=tm),
        grid=(n // tm, D_MODEL // tn),
        in_specs=[
            pl.BlockSpec((tm, ATTN_WIDTH), lambda i, j: (i, 0)),
            pl.BlockSpec((tm, FOURIER_WIDTH), lambda i, j: (i, 0)),
            pl.BlockSpec((None, ATTN_WIDTH, tn), lambda i, j: (layer, 0, j),
                         pipeline_mode=pl.Buffered(1)),
            pl.BlockSpec((None, FOURIER_WIDTH, tn), lambda i, j: (layer, fblk, j),
                         pipeline_mode=pl.Buffered(1)),
            pl.BlockSpec((None, 1, tn), gate_index),
        ] + _token_specs(xs, tm, tn, column_tiled=True),
        out_specs=pl.BlockSpec((tm, tn), lambda i, j: (i, j)),
        out_shape=jax.ShapeDtypeStruct((n, D_MODEL), F32),
        compiler_params=_cparams(("parallel", "parallel")),
    )(o_attn, o_f, w_out, w_out, mod, *xs)


def _swiglu_accumulate(h_scr, w1_ref, w3_ref, w2_ref, o_ref):
    h = h_scr[...]
    a = jnp.dot(h, w1_ref[...], preferred_element_type=F32)
    b = jnp.dot(h, w3_ref[...], preferred_element_type=F32)
    g = (_silu(a) * b).astype(BF16)
    for c in range(0, D_MODEL, DOWN_CHUNK):
        o_ref[:, c:c + DOWN_CHUNK] += jnp.dot(g, w2_ref[:, c:c + DOWN_CHUNK],
                                              preferred_element_type=F32)


def _ffn_dense_kernel(x_ref, g_ref, sh_ref, sc_ref, gate_ref, w1_ref, w3_ref, w2_ref, *refs,
                      n_side):
    side_src, o_ref, side_dst, h_scr = (refs[:n_side], refs[n_side],
                                        refs[n_side + 1:2 * n_side + 1], refs[2 * n_side + 1])
    j = pl.program_id(1)

    @pl.when(j == 0)
    def _():
        _norm_mod_rows(x_ref, g_ref, sh_ref, sc_ref, h_scr)
        o_ref[...] = jnp.zeros_like(o_ref)

    _swiglu_accumulate(h_scr, w1_ref, w3_ref, w2_ref, o_ref)
    for src, dst in zip(side_src, side_dst):
        dst[...] = src[...].astype(dst.dtype)

    @pl.when(j == pl.num_programs(1) - 1)
    def _():
        o_ref[...] = x_ref[...] + gate_ref[...] * o_ref[...]


def _side_block_rows(rows, n_steps):
    for b in range(2 * SUBLANES, rows + 1, 2 * SUBLANES):
        if rows % b == 0 and rows // b <= n_steps:
            return b
    raise ValueError("no row block")


def _side_cast_specs(side, n_steps, step):
    specs, shapes = [], []
    for a in side:
        rows, cols = a.shape
        blk = _side_block_rows(rows, n_steps)

        def index(i, j, last=rows // blk - 1):
            return (jnp.minimum(step(i, j), last), 0)

        specs.append(pl.BlockSpec((blk, cols), index))
        shapes.append(jax.ShapeDtypeStruct((rows, cols), BF16))
    return specs, shapes


def _ffn_dense(x, gain, mod, w1, w3, w2, layer, idx, row_batch, side=(), tm=512, tf=512):
    n = x.shape[0]
    d_ff = w1.shape[-1]
    nj = d_ff // tf
    side_specs, side_shapes = _side_cast_specs(side, (n // tm) * nj, lambda i, j: i * nj + j)
    out = pl.pallas_call(
        functools.partial(_ffn_dense_kernel, n_side=len(side)),
        grid=(n // tm, nj),
        in_specs=[
            pl.BlockSpec((tm, D_MODEL), lambda i, j: (i, 0)),
            pl.BlockSpec((None, 1, D_MODEL), lambda i, j: (layer, 0, 0)),
            _mod_spec(3, 6, row_batch, tm),
            _mod_spec(4, 6, row_batch, tm),
            _mod_spec(5, 6, row_batch, tm),
            pl.BlockSpec((None, D_MODEL, tf), lambda i, j: (idx, 0, j)),
            pl.BlockSpec((None, D_MODEL, tf), lambda i, j: (idx, 0, j)),
            pl.BlockSpec((None, tf, D_MODEL), lambda i, j: (idx, j, 0)),
        ] + side_specs,
        out_specs=[pl.BlockSpec((tm, D_MODEL), lambda i, j: (i, 0))] + side_specs,
        out_shape=[jax.ShapeDtypeStruct((n, D_MODEL), F32)] + side_shapes,
        scratch_shapes=[pltpu.VMEM((tm, D_MODEL), BF16)],
        compiler_params=_cparams(("arbitrary", "arbitrary")),
    )(x, gain, mod, mod, mod, w1, w3, w2, *side)
    return out[0], out[1:]


def _router_kernel(x_ref, g_ref, sh_ref, sc_ref, rw_ref, h_ref, idx_ref, gate_ref, rank_ref,
                   cnt_ref, cnt_scr, *, tm):
    @pl.when(pl.program_id(0) == 0)
    def _():
        cnt_scr[...] = jnp.zeros_like(cnt_scr)

    h = _norm_mod(x_ref[...], g_ref[...], sh_ref[...], sc_ref[...])
    h_ref[...] = h
    logits = jnp.dot(h, rw_ref[...], preferred_element_type=F32,
                     precision=lax.Precision.HIGHEST)
    lane = lax.broadcasted_iota(jnp.int32, (tm, LANES), 1).astype(F32)
    logits = jnp.where(lane < N_EXPERTS, logits, -jnp.inf)
    v1 = jnp.max(logits, axis=-1, keepdims=True)
    i1 = jnp.min(jnp.where(logits == v1, lane, float(LANES)), axis=-1, keepdims=True)
    rest = jnp.where(lane == i1, -jnp.inf, logits)
    v2 = jnp.max(rest, axis=-1, keepdims=True)
    i2 = jnp.min(jnp.where(rest == v2, lane, float(LANES)), axis=-1, keepdims=True)
    e2 = jnp.exp(v2 - v1)
    gate1 = 1.0 / (1.0 + e2)
    gate2 = e2 / (1.0 + e2)

    hot1 = (lane == i1).astype(BF16)
    hot2 = (lane == i2).astype(BF16)
    r_i = lax.broadcasted_iota(jnp.int32, (tm, tm), 0)
    c_i = lax.broadcasted_iota(jnp.int32, (tm, tm), 1)
    before = (c_i < r_i).astype(BF16)
    pre1 = jnp.dot(before, hot1, preferred_element_type=F32)
    pre2 = jnp.dot(before, hot2, preferred_element_type=F32)
    tot1 = jnp.sum(hot1.astype(F32), axis=0, keepdims=True)
    tot2 = jnp.sum(hot2.astype(F32), axis=0, keepdims=True)
    cnt = cnt_scr[...]
    rank1 = jnp.sum(jnp.where(lane == i1, pre1 + cnt, 0.0), axis=-1, keepdims=True)
    rank2 = jnp.sum(jnp.where(lane == i2, pre2 + cnt + tot1, 0.0), axis=-1, keepdims=True)
    cnt = cnt + tot1 + tot2
    cnt_scr[...] = cnt
    cnt_ref[...] = jnp.broadcast_to(cnt, cnt_ref.shape).astype(jnp.int32)

    idx_ref[...] = jnp.where(lane == 0, i1, jnp.where(lane == 1, i2, 0.0)).astype(jnp.int32)
    gate_ref[...] = jnp.where(lane == 0, gate1, jnp.where(lane == 1, gate2, 0.0))
    rank_ref[...] = jnp.where(lane == 0, rank1, jnp.where(lane == 1, rank2, 0.0)).astype(jnp.int32)


def _router(x, gain, mod, router_w, layer, row_batch, tm=512):
    n = x.shape[0]
    rw = jnp.zeros((D_MODEL, LANES), F32).at[:, :N_EXPERTS].set(router_w)
    tile = lambda i: (i, 0)
    return pl.pallas_call(
        functools.partial(_router_kernel, tm=tm),
        grid=(n // tm,),
        in_specs=[
            pl.BlockSpec((tm, D_MODEL), tile),
            pl.BlockSpec((None, 1, D_MODEL), lambda i: (layer, 0, 0)),
            _mod_spec(3, 6, row_batch, tm),
            _mod_spec(4, 6, row_batch, tm),
            pl.BlockSpec((D_MODEL, LANES), lambda i: (0, 0)),
        ],
        out_specs=[
            pl.BlockSpec((tm, D_MODEL), tile),
            pl.BlockSpec((tm, LANES), tile),
            pl.BlockSpec((tm, LANES), tile),
            pl.BlockSpec((tm, LANES), tile),
            pl.BlockSpec((8, LANES), lambda i: (0, 0)),
        ],
        out_shape=[
            jax.ShapeDtypeStruct((n, D_MODEL), F32),
            jax.ShapeDtypeStruct((n, LANES), jnp.int32),
            jax.ShapeDtypeStruct((n, LANES), F32),
            jax.ShapeDtypeStruct((n, LANES), jnp.int32),
            jax.ShapeDtypeStruct((8, LANES), jnp.int32),
        ],
        scratch_shapes=[pltpu.VMEM((1, LANES), F32)],
        compiler_params=_cparams(("arbitrary",)),
    )(x, gain, mod, mod, rw)


def _dispatch_kernel(pos_ref, pad_ref, h_ref, xs_ref, zeros, sem, zero_sem, *, tt, pad_rows):
    base = pl.program_id(0) * tt

    @pl.when(pl.program_id(0) == 0)
    def _():
        zeros[...] = jnp.zeros_like(zeros)

        def fill(e):
            start = pl.multiple_of(pad_ref[e], SUBLANES)
            return pltpu.make_async_copy(zeros, xs_ref.at[pl.ds(start, pad_rows), :], zero_sem)

        for e in range(N_EXPERTS):
            fill(e).start()
        for e in range(N_EXPERTS):
            fill(e).wait()

    def issue(t, c):
        for slot in range(2):
            pltpu.make_async_copy(h_ref.at[pl.ds(t, 1), :],
                                  xs_ref.at[pl.ds(pos_ref[2 * (base + t) + slot], 1), :],
                                  sem).start()
        return c

    lax.fori_loop(0, tt, issue, 0, unroll=ROW_DMA_UNROLL)
    for slot in range(2):
        pltpu.make_async_copy(h_ref, xs_ref.at[pl.ds(0, tt), :], sem).wait()


def _dispatch(h, pos, pad_start, rows, pad_rows, tt=256):
    n = h.shape[0]
    return pl.pallas_call(
        functools.partial(_dispatch_kernel, tt=tt, pad_rows=pad_rows),
        grid_spec=pltpu.PrefetchScalarGridSpec(
            num_scalar_prefetch=2,
            grid=(n // tt,),
            in_specs=[pl.BlockSpec((tt, D_MODEL), lambda i, pos, pad: (i, 0))],
            out_specs=pl.BlockSpec(memory_space=pl.ANY),
            scratch_shapes=[pltpu.VMEM((pad_rows, D_MODEL), F32),
                            pltpu.SemaphoreType.DMA(()), pltpu.SemaphoreType.DMA(())],
        ),
        out_shape=jax.ShapeDtypeStruct((rows, D_MODEL), F32),
        compiler_params=_cparams(("arbitrary",), has_side_effects=True),
    )(pos, pad_start, h)


def _moe_ffn_kernel(te_ref, nv_ref, x_ref, w1_ref, w3_ref, w2_ref, o_ref, h_scr):
    i = pl.program_id(0)
    j = pl.program_id(1)

    @pl.when(i < nv_ref[0])
    def _():
        @pl.when(j == 0)
        def _():
            h_scr[...] = x_ref[...].astype(BF16)
            o_ref[...] = jnp.zeros_like(o_ref)

        _swiglu_accumulate(h_scr, w1_ref, w3_ref, w2_ref, o_ref)


def _moe_ffn(xs, tile_expert, n_valid, w1, w3, w2, idx, tm, tf=1024):
    rows = tile_expert.shape[0] * tm
    d_ff = w1.shape[-1]
    assert d_ff % tf == 0 and rows <= xs.shape[0]
    nj = d_ff // tf

    def row_index(i, j, te, nv):
        return (jnp.maximum(jnp.minimum(i, nv[0] - 1), 0), 0)

    def up_index(i, j, te, nv):
        return (idx, te[i], 0, jnp.where(i < nv[0], j, nj - 1))

    def down_index(i, j, te, nv):
        return (idx, te[i], jnp.where(i < nv[0], j, nj - 1), 0)

    return pl.pallas_call(
        _moe_ffn_kernel,
        grid_spec=pltpu.PrefetchScalarGridSpec(
            num_scalar_prefetch=2,
            grid=(rows // tm, nj),
            in_specs=[
                pl.BlockSpec((tm, D_MODEL), row_index),
                pl.BlockSpec((None, None, D_MODEL, tf), up_index),
                pl.BlockSpec((None, None, D_MODEL, tf), up_index),
                pl.BlockSpec((None, None, tf, D_MODEL), down_index),
            ],
            out_specs=pl.BlockSpec((tm, D_MODEL), row_index),
            scratch_shapes=[pltpu.VMEM((tm, D_MODEL), BF16)],
        ),
        out_shape=jax.ShapeDtypeStruct((rows, D_MODEL), F32),
        compiler_params=_cparams(("arbitrary", "arbitrary"), vmem=LARGE_VMEM_LIMIT),
    )(tile_expert, n_valid, xs, w1, w3, w2)


def _combine_kernel(pos_ref, ys_ref, x_ref, gate_ref, g2_ref, o_ref, buf, sem, *, tt):
    i = pl.program_id(0)
    cur = i % 2

    def gather(tile, half):
        def issue(t, c):
            for slot in range(2):
                row = pos_ref[2 * (tile * tt + t) + slot]
                pltpu.make_async_copy(ys_ref.at[pl.ds(row, 1), :],
                                      buf.at[half, slot, pl.ds(t, 1), :], sem.at[half]).start()
            return c

        lax.fori_loop(0, tt, issue, 0, unroll=ROW_DMA_UNROLL)

    @pl.when(i == 0)
    def _():
        gather(0, 0)

    for slot in range(2):
        pltpu.make_async_copy(ys_ref.at[pl.ds(0, tt), :], buf.at[cur, slot], sem.at[cur]).wait()

    @pl.when(i + 1 < pl.num_programs(0))
    def _():
        gather(i + 1, 1 - cur)

    gates = gate_ref[...]
    f = gates[:, 0:1] * buf[cur, 0] + gates[:, 1:2] * buf[cur, 1]
    o_ref[...] = x_ref[...] + g2_ref[...] * f


def _combine_final_kernel(pos_ref, ys_ref, x_ref, gate_ref, g2_ref, gf_ref, shf_ref, scf_ref,
                          *refs, out_rows, tt):
    out_refs, (x_new, buf, sem) = refs[:len(out_rows)], refs[len(out_rows):]
    _combine_kernel(pos_ref, ys_ref, x_ref, gate_ref, g2_ref, x_new, buf, sem, tt=tt)
    y = _norm_mod(x_new[...], gf_ref[...], shf_ref[...], scf_ref[...])
    i, lo = pl.program_id(0), 0
    for o_ref, rows in zip(out_refs, out_rows):
        nt = rows // tt

        @pl.when((i >= lo) & (i < lo + nt))
        def _(o_ref=o_ref):
            o_ref[...] = y

        lo += nt


def _combine(ys, pos, x, gates, mod, row_batch, final=None, tt=256):
    n = x.shape[0]
    in_specs = [
        pl.BlockSpec(memory_space=pl.ANY),
        pl.BlockSpec((tt, D_MODEL), lambda i, pos: (i, 0)),
        pl.BlockSpec((tt, LANES), lambda i, pos: (i, 0)),
        _mod_spec(5, 6, row_batch, tt),
    ]
    args = [pos, ys, x, gates, mod]
    scratch = [pltpu.VMEM((2, 2, tt, D_MODEL), F32), pltpu.SemaphoreType.DMA((2,))]
    if final is None:
        body = functools.partial(_combine_kernel, tt=tt)
        out_specs = pl.BlockSpec((tt, D_MODEL), lambda i, pos: (i, 0))
        out_shape = jax.ShapeDtypeStruct((n, D_MODEL), F32)
    else:
        gain, fmod, out_rows = final
        body = functools.partial(_combine_final_kernel, out_rows=out_rows, tt=tt)
        in_specs += [pl.BlockSpec((1, D_MODEL), lambda i, pos: (0, 0)),
                     _mod_spec(0, 2, row_batch, tt), _mod_spec(1, 2, row_batch, tt)]
        args += [gain, fmod, fmod]
        out_specs, out_shape, lo = [], [], 0
        for rows in out_rows:
            nt = rows // tt

            def index(i, pos, lo=lo, nt=nt):
                return (jnp.clip(i - lo, 0, nt - 1), 0)

            out_specs.append(pl.BlockSpec((tt, D_MODEL), index))
            out_shape.append(jax.ShapeDtypeStruct((rows, D_MODEL), F32))
            lo += nt
        scratch = [pltpu.VMEM((tt, D_MODEL), F32)] + scratch
    return pl.pallas_call(
        body,
        grid_spec=pltpu.PrefetchScalarGridSpec(
            num_scalar_prefetch=1, grid=(n // tt,), in_specs=in_specs, out_specs=out_specs,
            scratch_shapes=scratch),
        out_shape=out_shape,
        compiler_params=_cparams(("arbitrary",)),
    )(*args)


def _moe_layer(x, gain, mod, router_w, w1, w3, w2, layer, idx, row_batch, final=None, tm=512):
    n = x.shape[0]
    h, top_idx, gates, rank, counts = _router(x, gain, mod, router_w[idx], layer, row_batch)
    counts = counts[0, :N_EXPERTS]
    padded = ((counts + tm - 1) // tm) * tm
    ends = jnp.cumsum(padded)
    starts = ends - padded
    experts = top_idx[:, :2]
    pos = (starts[experts] + rank[:, :2]).reshape(-1).astype(jnp.int32)
    n_tiles = (2 * n) // tm + N_EXPERTS
    tile_start = jnp.arange(n_tiles, dtype=jnp.int32) * tm
    tile_expert = jnp.minimum(jnp.sum(tile_start[:, None] >= ends[None, :], axis=1),
                              N_EXPERTS - 1).astype(jnp.int32)
    n_valid = (ends[-1:] // tm).astype(jnp.int32)
    pad_start = ((starts + counts) // SUBLANES * SUBLANES).astype(jnp.int32)
    xs = _dispatch(h, pos, pad_start, (n_tiles + 2) * tm, tm + SUBLANES)
    ys = _moe_ffn(xs, tile_expert, n_valid, w1, w3, w2, idx, tm)
    return _combine(ys, pos, x, gates, mod, row_batch, final)


def _final_kernel(x_ref, g_ref, sh_ref, sc_ref, o_ref):
    o_ref[...] = _norm_mod(x_ref[...], g_ref[...], sh_ref[...], sc_ref[...])


def _final(x, gain, fmod, row_lo, rows, row_batch, tm=512):
    off = row_lo // tm

    def mod_spec(which):
        return pl.BlockSpec((None, 1, D_MODEL),
                            lambda i: (row_batch((i + off) * tm) * 2 + which, 0, 0))

    return pl.pallas_call(
        _final_kernel,
        grid=(rows // tm,),
        in_specs=[
            pl.BlockSpec((tm, D_MODEL), lambda i: (i + off, 0)),
            pl.BlockSpec((1, D_MODEL), lambda i: (0, 0)),
            mod_spec(0),
            mod_spec(1),
        ],
        out_specs=pl.BlockSpec((tm, D_MODEL), lambda i: (i, 0)),
        out_shape=jax.ShapeDtypeStruct((rows, D_MODEL), F32),
        compiler_params=_cparams(("parallel",)),
    )(x, gain, fmod, fmod)


def kernel(x_prompt, x_sample, c_prompt, c_sample, w_mod, b_mod, norm_mix, w_in, w_fourier, w_out,
           norm_ffn, dense_w1, dense_w3, dense_w2, router_w, moe_w1, moe_w3, moe_w2,
           w_final_mod, b_final_mod, norm_final):
    batch, seq_p, d = x_prompt.shape
    batch_s, seq_s, _ = x_sample.shape
    n_p, n_s = batch * seq_p, batch_s * seq_s
    n = n_p + n_s
    depth = w_mod.shape[0]
    assert d == D_MODEL and batch_s == 1 and n_p % seq_s == 0 and batch + batch_s <= MOD_ROWS

    def row_batch(row):
        return jnp.minimum(row // seq_p, batch)

    x = (x_prompt.reshape(n_p, d), x_sample.reshape(n_s, d))
    c_all = jnp.zeros((MOD_ROWS, d), F32).at[:batch].set(c_prompt).at[batch:batch + 1].set(c_sample)

    mod = _modulation(c_all, w_mod, b_mod)
    fmod = _modulation(c_all, w_final_mod[None], b_final_mod[None])[0]
    fmod = fmod.reshape(MOD_ROWS * 2, 1, d)

    gain_mix = norm_mix.reshape(depth, 1, d)
    gain_ffn = norm_ffn.reshape(depth, 1, d)
    gain_final = norm_final.reshape(1, d)
    w_in_b = w_in.astype(BF16)
    w_fourier_b = w_fourier.astype(BF16)
    early_f32 = (w_out, dense_w1, dense_w3, dense_w2)
    w_out_b = dense_b = None
    moe_f32 = (moe_w1, moe_w3, moe_w2)
    moe_b = None
    rope_p = _rope_tables(seq_p)
    rope_s = _rope_tables(seq_s)

    for l in range(depth):
        mod_l = mod[l].reshape(MOD_ROWS * 6, 1, d)
        z = _in_proj(x, gain_mix, mod_l, w_in_b, l, row_batch)
        side = () if w_out_b is not None else tuple(
            w.reshape(-1, w.shape[-1]) for w in early_f32)
        o_attn, cast = _attention(z, None, *rope_p, seq_p, 0, batch, side)
        if side:
            w_out_b, *dense_b = [c.reshape(w.shape) for c, w in zip(cast, early_f32)]
        o_attn, _ = _attention(z, o_attn, *rope_s, seq_s, n_p // seq_s, batch_s)
        o_f = _fourier(z, None, w_fourier_b, l, seq_p, 0, batch)
        o_f = _fourier(z, o_f, w_fourier_b, l, seq_s, n_p // seq_s, batch_s)
        x = _out_proj(o_attn, o_f, x, mod_l, w_out_b, l, row_batch)
        if l % 2 == 0:
            side = () if moe_b is not None or not moe_w1.size else tuple(
                w.reshape(-1, w.shape[-1]) for w in moe_f32)
            x, cast = _ffn_dense(x, gain_ffn, mod_l, *dense_b, l, l // 2, row_batch, side)
            x = (x,)
            if side:
                moe_b = [c.reshape(w.shape) for c, w in zip(cast, moe_f32)]
        else:
            if moe_b is None:
                moe_b = [w.astype(BF16) for w in moe_f32]
            final = (gain_final, fmod, (n_p, n_s)) if l == depth - 1 else None
            x = _moe_layer(x, gain_ffn, mod_l, router_w, *moe_b, l, l // 2, row_batch, final)
            x = tuple(x) if final else (x,)

    if depth % 2 == 0:
        y_p, y_s = x
    else:
        y_p = _final(x[0], gain_final, fmod, 0, n_p, row_batch)
        y_s = _final(x[0], gain_final, fmod, n_p, n_s, row_batch)
    return (y_p.reshape(batch, seq_p, d), y_s.reshape(batch_s, seq_s, d))
```

```python
import functools

import numpy as np
import jax
import jax.numpy as jnp
from jax import lax
from jax.experimental import pallas as pl
from jax.experimental.pallas import tpu as pltpu

F32 = jnp.float32
BF16 = jnp.bfloat16

D_MODEL = 2048
HEAD_DIM = 128
N_ATTN_HEADS = 12
ATTN_WIDTH = N_ATTN_HEADS * HEAD_DIM
N_FOURIER_GROUPS = 4
FOURIER_GROUP = 128
FOURIER_WIDTH = N_FOURIER_GROUPS * FOURIER_GROUP
IN_WIDTH = 3 * ATTN_WIDTH + FOURIER_WIDTH
DILATIONS = (1, 4, 16)
HALF_NEIGHBOURS = 64
ROPE_THETA = 10000.0
N_EXPERTS = 8
EPS = 1e-6
NEG_INF = -1e30
LANES = 128
SUBLANES = 8
MOD_ROWS = 16

VMEM_LIMIT = 48 * 1024 * 1024
ATTN_VMEM_LIMIT = 58 * 1024 * 1024
BLOCK_GROUP = 4
DFT_GROUP = 8
NORM_ROWS = 16
ROW_DMA_UNROLL = 8
DOWN_CHUNK = 512
LARGE_VMEM_LIMIT = 56 * 1024 * 1024


def _cparams(sem, vmem=VMEM_LIMIT, **kw):
    return pltpu.CompilerParams(dimension_semantics=sem, vmem_limit_bytes=vmem, **kw)


def _norm_mod(x, g, sh, sc):
    ms = jnp.mean(x * x, axis=-1, keepdims=True)
    return (x * lax.rsqrt(ms + EPS)) * g * (1.0 + sc) + sh


def _norm_mod_rows(x_ref, g_ref, sh_ref, sc_ref, out_ref):
    g, sh, sc = g_ref[...], sh_ref[...], sc_ref[...]

    def body(c, carry):
        rows = pl.ds(pl.multiple_of(c * NORM_ROWS, NORM_ROWS), NORM_ROWS)
        out_ref[rows, :] = _norm_mod(x_ref[rows, :], g, sh, sc).astype(out_ref.dtype)
        return carry

    lax.fori_loop(0, x_ref.shape[0] // NORM_ROWS, body, 0, unroll=8)


def _silu(a):
    return a / (1.0 + jnp.exp(-a))


def _mod_kernel(c_ref, w_ref, b_ref, o_ref):
    cs = _silu(c_ref[...]).astype(BF16)
    o_ref[...] = jnp.dot(cs, w_ref[...].astype(BF16), preferred_element_type=F32) + b_ref[...]


def _modulation(c_all, w, b, tn=1024):
    nl, d, n = w.shape
    return pl.pallas_call(
        _mod_kernel,
        grid=(nl, n // tn),
        in_specs=[
            pl.BlockSpec((MOD_ROWS, d), lambda l, j: (0, 0)),
            pl.BlockSpec((None, d, tn), lambda l, j: (l, 0, j)),
            pl.BlockSpec((None, 1, tn), lambda l, j: (l, 0, j)),
        ],
        out_specs=pl.BlockSpec((None, MOD_ROWS, tn), lambda l, j: (l, 0, j)),
        out_shape=jax.ShapeDtypeStruct((nl, MOD_ROWS, n), F32),
        compiler_params=_cparams(("parallel", "parallel")),
    )(c_all, w, b.reshape(nl, 1, n))


def _mod_spec(which, n_vec, row_batch, tm):
    def index(i, *_):
        return (row_batch(i * tm) * n_vec + which, 0, 0)
    return pl.BlockSpec((None, 1, D_MODEL), index)


def _token_specs(xs, tm, width, column_tiled=False):
    specs, lo = [], 0
    for x in xs:
        nt = x.shape[0] // tm

        def index(i, j=0, *_, lo=lo, nt=nt):
            return (jnp.clip(i - lo, 0, nt - 1), j if column_tiled else 0)

        specs.append(pl.BlockSpec((tm, width), index))
        lo += nt
    return specs


def _with_token_tile(x_refs, src_rows, tm, fn):
    if len(x_refs) == 1:
        fn(x_refs[0])
        return
    i, lo = pl.program_id(0), 0
    for x_ref, rows in zip(x_refs, src_rows):
        nt = rows // tm
        pl.when((i >= lo) & (i < lo + nt))(functools.partial(fn, x_ref))
        lo += nt


def _in_proj_kernel(*refs, src_rows, tm):
    ns = len(src_rows)
    g_ref, sh_ref, sc_ref, w_ref, o_ref, h_scr = refs[ns:]

    def prepare(x_ref):
        _norm_mod_rows(x_ref, g_ref, sh_ref, sc_ref, h_scr)

    @pl.when(pl.program_id(1) == 0)
    def _():
        _with_token_tile(refs[:ns], src_rows, tm, prepare)

    o_ref[...] = jnp.dot(h_scr[...], w_ref[...], preferred_element_type=F32).astype(o_ref.dtype)


def _in_proj(xs, gain, mod, w_in, layer, row_batch, tm=1024, tn=1280):
    n = sum(x.shape[0] for x in xs)
    z, lo = None, 0
    for x in xs:
        lo_t = lo // tm
        in_specs = [
            pl.BlockSpec((tm, D_MODEL), lambda i, j: (i, 0)),
            pl.BlockSpec((None, 1, D_MODEL), lambda i, j: (layer, 0, 0)),
            _mod_spec(0, 6, lambda row, lo=lo: row_batch(row + lo), tm),
            _mod_spec(1, 6, lambda row, lo=lo: row_batch(row + lo), tm),
            pl.BlockSpec((None, D_MODEL, tn), lambda i, j: (layer, 0, j)),
        ]
        args = [x, gain, mod, mod, w_in]
        if z is not None:
            in_specs.append(pl.BlockSpec(memory_space=pl.ANY))
            args.append(z)

        def kern(*refs, n_in=len(args)):
            _in_proj_kernel(*refs[:5], *refs[n_in:], src_rows=(tm,), tm=tm)

        z = pl.pallas_call(
            kern,
            grid=(x.shape[0] // tm, IN_WIDTH // tn),
            in_specs=in_specs,
            out_specs=pl.BlockSpec((tm, tn), lambda i, j, lo_t=lo_t: (i + lo_t, j)),
            out_shape=jax.ShapeDtypeStruct((n, IN_WIDTH), BF16),
            scratch_shapes=[pltpu.VMEM((tm, D_MODEL), BF16)],
            input_output_aliases={} if z is None else {5: 0},
            compiler_params=_cparams(("parallel", "arbitrary")),
        )(*args)
        lo += x.shape[0]
    return z


def _attn_kernel(q_ref, k_ref, v_ref, cos_ref, sin_ref, o_ref,
                 qr, kr, vr, acc, mx, den, bias, *, seq, bq):
    chunk = bq * DILATIONS[-1]

    @pl.when((pl.program_id(0) == 0) & (pl.program_id(1) == 0))
    def _():
        for p, r in enumerate(DILATIONS):
            kw = min(bq + 2 * HALF_NEIGHBOURS, seq // r)
            rel = (lax.broadcasted_iota(jnp.int32, (bq, kw), 1)
                   - lax.broadcasted_iota(jnp.int32, (bq, kw), 0))
            for case, off in enumerate((0, -HALF_NEIGHBOURS, bq - kw)):
                bias[p, case, :, :kw] = jnp.where(jnp.abs(rel + off) <= HALF_NEIGHBOURS,
                                                  0.0, NEG_INF)

    cos = cos_ref[...]
    sin = sin_ref[...]
    q = q_ref[...].astype(F32)
    qr[...] = (q * cos + pltpu.roll(q, HEAD_DIM // 2, 1) * sin) * (1.0 / np.sqrt(HEAD_DIM))
    k = k_ref[...].astype(F32)
    kr[...] = k * cos + pltpu.roll(k, HEAD_DIM // 2, 1) * sin
    vr[...] = v_ref[...].astype(F32)

    def do_chunk(c, carry):
        base = c * chunk
        for p, r in enumerate(DILATIONS):
            sub_len = seq // r
            kw = min(bq + 2 * HALF_NEIGHBOURS, sub_len)
            nb = chunk // (r * bq)
            group = BLOCK_GROUP * (bq + 2 * HALF_NEIGHBOURS) // kw

            def do_blocks(t, carry2, p=p, r=r, sub_len=sub_len, kw=kw, nb=nb, group=group):
                work = []
                for g in range(group):
                    tg = t * group + g
                    m = tg // nb
                    i = tg % nb
                    q0 = (c * nb + i) * bq
                    k0 = jnp.clip(q0 - HALF_NEIGHBOURS, 0, sub_len - kw)
                    qb = qr[pl.ds(m + r * q0, bq, stride=r), :].astype(BF16)
                    kb = kr[pl.ds(m + r * k0, kw, stride=r), :].astype(BF16)
                    vb = vr[pl.ds(m + r * k0, kw, stride=r), :].astype(BF16)
                    dst = pl.ds(m + r * (i * bq), bq, stride=r)
                    work.append((qb, kb, vb, k0 - q0, dst))
                scores = [lax.dot_general(qb, kb, (((1,), (1,)), ((), ())),
                                          preferred_element_type=F32)
                          for qb, kb, _, _, _ in work]
                weights = []
                for s, (_, _, _, off, dst) in zip(scores, work):
                    case = jnp.where(off == 0, 0, jnp.where(off == -HALF_NEIGHBOURS, 1, 2))
                    s = s + bias[p, case, :, :kw]
                    smax = jnp.max(s, axis=-1, keepdims=True)
                    e = jnp.exp(s - smax)
                    mx[p, dst, :] = jnp.broadcast_to(smax, (bq, HEAD_DIM))
                    den[p, dst, :] = jnp.broadcast_to(jnp.sum(e, axis=-1, keepdims=True),
                                                      (bq, HEAD_DIM))
                    weights.append(e.astype(BF16))
                for e, (_, _, vb, _, dst) in zip(weights, work):
                    acc[p, dst, :] = jnp.dot(e, vb, preferred_element_type=F32)
                return carry2

            lax.fori_loop(0, (r * nb) // group, do_blocks, 0)

        top = jnp.maximum(jnp.maximum(mx[0], mx[1]), mx[2])
        num = jnp.zeros((chunk, HEAD_DIM), F32)
        tot = jnp.zeros((chunk, HEAD_DIM), F32)
        for p in range(len(DILATIONS)):
            w = jnp.exp(mx[p] - top)
            num = num + w * acc[p]
            tot = tot + w * den[p]
        o_ref[pl.ds(pl.multiple_of(base, chunk), chunk), :] = (num / tot).astype(o_ref.dtype)
        return carry

    lax.fori_loop(0, seq // chunk, do_chunk, 0)


def _attention(z, o_prev, rope_cos, rope_sin, seq, batch_lo, batch_n, side=(), bq=128):
    n = z.shape[0]
    chunk = bq * DILATIONS[-1]
    assert seq % chunk == 0 and n % seq == 0
    zb = z.reshape(n // seq, seq, IN_WIDTH)

    def col_spec(off):
        return pl.BlockSpec((None, seq, HEAD_DIM),
                            lambda b, h: (batch_lo + b, 0, off + h))

    in_specs = [
        col_spec(0), col_spec(N_ATTN_HEADS), col_spec(2 * N_ATTN_HEADS),
        pl.BlockSpec((seq, HEAD_DIM), lambda b, h: (0, 0), pipeline_mode=pl.Buffered(1)),
        pl.BlockSpec((seq, HEAD_DIM), lambda b, h: (0, 0), pipeline_mode=pl.Buffered(1)),
    ]
    args = [zb, zb, zb, rope_cos, rope_sin]
    aliases = {}
    if o_prev is not None:
        in_specs.append(pl.BlockSpec(memory_space=pl.ANY))
        args.append(o_prev.reshape(n // seq, seq, ATTN_WIDTH))
        aliases = {5: 0}
    n_main = len(args)
    side_specs, side_shapes = _side_cast_specs(side, batch_n * N_ATTN_HEADS,
                                               lambda b, h: b * N_ATTN_HEADS + h)

    def kern(*refs):
        ins, rest = refs[:n_main + len(side)], refs[n_main + len(side):]
        o_ref, side_dst, scratch = rest[0], rest[1:1 + len(side)], rest[1 + len(side):]
        _attn_kernel(*ins[:5], o_ref, *scratch, seq=seq, bq=bq)
        for src, dst in zip(ins[n_main:], side_dst):
            dst[...] = src[...].astype(dst.dtype)

    out = pl.pallas_call(
        kern,
        grid=(batch_n, N_ATTN_HEADS),
        in_specs=in_specs + side_specs,
        out_specs=[pl.BlockSpec((None, seq, HEAD_DIM), lambda b, h: (batch_lo + b, 0, h))]
        + side_specs,
        out_shape=[jax.ShapeDtypeStruct((n // seq, seq, ATTN_WIDTH), BF16)] + side_shapes,
        scratch_shapes=[
            pltpu.VMEM((seq, HEAD_DIM), F32),
            pltpu.VMEM((seq, HEAD_DIM), F32),
            pltpu.VMEM((seq, HEAD_DIM), F32),
            pltpu.VMEM((len(DILATIONS), chunk, HEAD_DIM), F32),
            pltpu.VMEM((len(DILATIONS), chunk, HEAD_DIM), F32),
            pltpu.VMEM((len(DILATIONS), chunk, HEAD_DIM), F32),
            pltpu.VMEM((len(DILATIONS), 3, bq, bq + 2 * HALF_NEIGHBOURS), F32),
        ],
        input_output_aliases=aliases,
        compiler_params=_cparams(("arbitrary", "arbitrary"), vmem=ATTN_VMEM_LIMIT),
    )(*args, *side)
    return out[0].reshape(n, ATTN_WIDTH), out[1:]


def _rope_tables(seq):
    half = HEAD_DIM // 2
    inv = jnp.power(ROPE_THETA, -jnp.arange(half, dtype=F32) * 2.0 / HEAD_DIM)
    ang = jnp.arange(seq, dtype=F32)[:, None] * inv[None, :]
    cos, sin = jnp.cos(ang), jnp.sin(ang)
    return jnp.concatenate([cos, cos], axis=-1), jnp.concatenate([-sin, sin], axis=-1)


def _dft_factors(seq):
    s1 = {2048: 32, 8192: 64}.get(seq)
    if s1 is None:
        s1 = 1
        while s1 * s1 < seq:
            s1 *= 2
        s1 = seq // s1 if (seq // s1) * s1 == seq else s1
    return s1, seq // s1


@functools.lru_cache(maxsize=None)
def _dft_tables(seq):
    s1, s2 = _dft_factors(seq)
    c = np.arange(FOURIER_GROUP)
    ang_c = 2.0 * np.pi * ((c[:, None] * c[None, :]) % FOURIER_GROUP) / FOURIER_GROUP
    norm = 1.0 / np.sqrt(float(seq) * FOURIER_GROUP)
    w_chan = np.concatenate([np.cos(ang_c), -np.sin(ang_c)], axis=1) * norm
    k1 = np.arange(s1)[None, :, None]
    n1 = np.arange(s1)[None, None, :]
    n2 = np.arange(s2)[:, None, None]
    ang1 = 2.0 * np.pi * ((k1 * (n2 + s2 * n1)) % seq) / seq
    stage1 = np.concatenate([np.cos(ang1), np.sin(ang1)], axis=1)
    j = np.arange(s2)
    ang2 = 2.0 * np.pi * ((j[:, None] * j[None, :]) % s2) / s2
    return (np.asarray(w_chan, np.float32), np.asarray(stage1, np.float32),
            np.asarray(np.cos(ang2), np.float32), np.asarray(np.sin(ang2), np.float32))


def _fourier_kernel(f_ref, wc_ref, st1_ref, c2_ref, s2_ref, wf_ref, o_ref,
                    zr_scr, zi_scr, tr_scr, ti_scr, y_scr, *, s1, s2):
    g = FOURIER_GROUP
    z = jnp.dot(f_ref[...], wc_ref[...], preferred_element_type=F32)
    zr_scr[...] = z[:, :g]
    zi_scr[...] = z[:, g:]

    def stage1(t, carry):
        n2s = [t * DFT_GROUP + u for u in range(DFT_GROUP)]
        zs = []
        for n2 in n2s:
            rows = pl.ds(n2, s1, stride=s2)
            zs.append(jnp.concatenate([zr_scr[rows, :], zi_scr[rows, :]], axis=1).astype(BF16))
        prs = [jnp.dot(st1_ref[n2], z, preferred_element_type=F32)
               for n2, z in zip(n2s, zs)]
        for n2, pr in zip(n2s, prs):
            dst = pl.ds(pl.multiple_of(n2 * s1, s1), s1)
            tr_scr[dst, :] = pr[:s1, :g] + pr[s1:, g:]
            ti_scr[dst, :] = pr[:s1, g:] - pr[s1:, :g]
        return carry

    lax.fori_loop(0, s2 // DFT_GROUP, stage1, 0)

    def stage2(t, carry):
        rows = [pl.ds(t * DFT_GROUP + u, s2, stride=s1) for u in range(DFT_GROUP)]
        ts = [(tr_scr[r, :].astype(BF16), ti_scr[r, :].astype(BF16)) for r in rows]
        ys = [jnp.dot(c2_ref[...], tr, preferred_element_type=F32)
              + jnp.dot(s2_ref[...], ti, preferred_element_type=F32) for tr, ti in ts]
        for r, y in zip(rows, ys):
            y_scr[r, :] = y
        return carry

    lax.fori_loop(0, s1 // DFT_GROUP, stage2, 0)
    o_ref[...] = jnp.dot(y_scr[...].astype(BF16), wf_ref[...],
                         preferred_element_type=F32).astype(o_ref.dtype)


def _fourier(z, o_prev, w_fourier, layer, seq, batch_lo, batch_n):
    n = z.shape[0]
    s1, s2 = _dft_factors(seq)
    w_chan, stage1, c2, sn2 = (jnp.asarray(t, BF16) for t in _dft_tables(seq))
    zb = z.reshape(n // seq, seq, IN_WIDTH)
    f_col = 3 * N_ATTN_HEADS
    in_specs = [
        pl.BlockSpec((None, seq, FOURIER_GROUP), lambda b, g: (batch_lo + b, 0, f_col + g)),
        pl.BlockSpec((FOURIER_GROUP, 2 * FOURIER_GROUP), lambda b, g: (0, 0)),
        pl.BlockSpec((s2, 2 * s1, s1), lambda b, g: (0, 0, 0)),
        pl.BlockSpec((s2, s2), lambda b, g: (0, 0)),
        pl.BlockSpec((s2, s2), lambda b, g: (0, 0)),
        pl.BlockSpec((None, None, FOURIER_GROUP, FOURIER_GROUP), lambda b, g: (layer, g, 0, 0)),
    ]
    args = [zb, w_chan, stage1, c2, sn2, w_fourier]
    aliases = {}
    if o_prev is not None:
        in_specs.append(pl.BlockSpec(memory_space=pl.ANY))
        args.append(o_prev.reshape(n // seq, seq, FOURIER_WIDTH))
        aliases = {6: 0}

    def kern(*refs):
        if o_prev is not None:
            refs = refs[:6] + refs[7:]
        _fourier_kernel(*refs, s1=s1, s2=s2)

    out = pl.pallas_call(
        kern,
        grid=(batch_n, N_FOURIER_GROUPS),
        in_specs=in_specs,
        out_specs=pl.BlockSpec((None, seq, FOURIER_GROUP), lambda b, g: (batch_lo + b, 0, g)),
        out_shape=jax.ShapeDtypeStruct((n // seq, seq, FOURIER_WIDTH), BF16),
        scratch_shapes=[pltpu.VMEM((seq, FOURIER_GROUP), F32)] * 5,
        input_output_aliases=aliases,
        compiler_params=_cparams(("parallel", "parallel")),
    )(*args)
    return out.reshape(n, FOURIER_WIDTH)


def _out_proj_kernel(oa_ref, of_ref, wa_ref, wf_ref, g_ref, *refs, src_rows, tm):
    ns = len(src_rows)
    o_ref = refs[ns]
    y = jnp.dot(oa_ref[...], wa_ref[...], preferred_element_type=F32)
    y = y + jnp.dot(of_ref[...], wf_ref[...], preferred_element_type=F32)
    gated = g_ref[...] * y

    def residual(x_ref):
        o_ref[...] = x_ref[...] + gated

    _with_token_tile(refs[:ns], src_rows, tm, residual)


def _out_proj(o_attn, o_f, xs, mod, w_out, layer, row_batch, tm=512, tn=D_MODEL):
    n = sum(x.shape[0] for x in xs)
    src_rows = tuple(x.shape[0] for x in xs)
    fblk = ATTN_WIDTH // FOURIER_WIDTH

    def gate_index(i, j):
        return (row_batch(i * tm) * 6 + 2, 0, j)

    return pl.pallas_call(
        functools.partial(_out_proj_kernel, src_rows=src_rows, tm=tm),
        grid=(n // tm, D_MODEL // tn),
        in_specs=[
            pl.BlockSpec((tm, ATTN_WIDTH), lambda i, j: (i, 0)),
            pl.BlockSpec((tm, FOURIER_WIDTH), lambda i, j: (i, 0)),
            pl.BlockSpec((None, ATTN_WIDTH, tn), lambda i, j: (layer, 0, j),
                         pipeline_mode=pl.Buffered(1)),
            pl.BlockSpec((None, FOURIER_WIDTH, tn), lambda i, j: (layer, fblk, j),
                         pipeline_mode=pl.Buffered(1)),
            pl.BlockSpec((None, 1, tn), gate_index),
        ] + _token_specs(xs, tm, tn, column_tiled=True),
        out_specs=pl.BlockSpec((tm, tn), lambda i, j: (i, j)),
        out_shape=jax.ShapeDtypeStruct((n, D_MODEL), F32),
        compiler_params=_cparams(("parallel", "parallel")),
    )(o_attn, o_f, w_out, w_out, mod, *xs)


def _swiglu_accumulate(h_scr, w1_ref, w3_ref, w2_ref, o_ref):
    h = h_scr[...]
    a = jnp.dot(h, w1_ref[...], preferred_element_type=F32)
    b = jnp.dot(h, w3_ref[...], preferred_element_type=F32)
    g = (_silu(a) * b).astype(BF16)
    for c in range(0, D_MODEL, DOWN_CHUNK):
        o_ref[:, c:c + DOWN_CHUNK] += jnp.dot(g, w2_ref[:, c:c + DOWN_CHUNK],
                                              preferred_element_type=F32)


def _ffn_dense_kernel(x_ref, g_ref, sh_ref, sc_ref, gate_ref, w1_ref, w3_ref, w2_ref, *refs,
                      n_side):
    side_src, o_ref, side_dst, h_scr = (refs[:n_side], refs[n_side],
                                        refs[n_side + 1:2 * n_side + 1], refs[2 * n_side + 1])
    j = pl.program_id(1)

    @pl.when(j == 0)
    def _():
        _norm_mod_rows(x_ref, g_ref, sh_ref, sc_ref, h_scr)
        o_ref[...] = jnp.zeros_like(o_ref)

    _swiglu_accumulate(h_scr, w1_ref, w3_ref, w2_ref, o_ref)
    for src, dst in zip(side_src, side_dst):
        dst[...] = src[...].astype(dst.dtype)

    @pl.when(j == pl.num_programs(1) - 1)
    def _():
        o_ref[...] = x_ref[...] + gate_ref[...] * o_ref[...]


def _side_block_rows(rows, n_steps):
    for b in range(2 * SUBLANES, rows + 1, 2 * SUBLANES):
        if rows % b == 0 and rows // b <= n_steps:
            return b
    raise ValueError("no row block")


def _side_cast_specs(side, n_steps, step):
    specs, shapes = [], []
    for a in side:
        rows, cols = a.shape
        blk = _side_block_rows(rows, n_steps)

        def index(i, j, last=rows // blk - 1):
            return (jnp.minimum(step(i, j), last), 0)

        specs.append(pl.BlockSpec((blk, cols), index))
        shapes.append(jax.ShapeDtypeStruct((rows, cols), BF16))
    return specs, shapes


def _ffn_dense(x, gain, mod, w1, w3, w2, layer, idx, row_batch, side=(), tm=512, tf=512):
    n = x.shape[0]
    d_ff = w1.shape[-1]
    nj = d_ff // tf
    side_specs, side_shapes = _side_cast_specs(side, (n // tm) * nj, lambda i, j: i * nj + j)
    out = pl.pallas_call(
        functools.partial(_ffn_dense_kernel, n_side=len(side)),
        grid=(n // tm, nj),
        in_specs=[
            pl.BlockSpec((tm, D_MODEL), lambda i, j: (i, 0)),
            pl.BlockSpec((None, 1, D_MODEL), lambda i, j: (layer, 0, 0)),
            _mod_spec(3, 6, row_batch, tm),
            _mod_spec(4, 6, row_batch, tm),
            _mod_spec(5, 6, row_batch, tm),
            pl.BlockSpec((None, D_MODEL, tf), lambda i, j: (idx, 0, j)),
            pl.BlockSpec((None, D_MODEL, tf), lambda i, j: (idx, 0, j)),
            pl.BlockSpec((None, tf, D_MODEL), lambda i, j: (idx, j, 0)),
        ] + side_specs,
        out_specs=[pl.BlockSpec((tm, D_MODEL), lambda i, j: (i, 0))] + side_specs,
        out_shape=[jax.ShapeDtypeStruct((n, D_MODEL), F32)] + side_shapes,
        scratch_shapes=[pltpu.VMEM((tm, D_MODEL), BF16)],
        compiler_params=_cparams(("arbitrary", "arbitrary")),
    )(x, gain, mod, mod, mod, w1, w3, w2, *side)
    return out[0], out[1:]


def _router_kernel(x_ref, g_ref, sh_ref, sc_ref, rw_ref, h_ref, idx_ref, gate_ref, rank_ref,
                   cnt_ref, cnt_scr, *, tm):
    @pl.when(pl.program_id(0) == 0)
    def _():
        cnt_scr[...] = jnp.zeros_like(cnt_scr)

    h = _norm_mod(x_ref[...], g_ref[...], sh_ref[...], sc_ref[...])
    h_ref[...] = h
    logits = jnp.dot(h, rw_ref[...], preferred_element_type=F32,
                     precision=lax.Precision.HIGHEST)
    lane = lax.broadcasted_iota(jnp.int32, (tm, LANES), 1).astype(F32)
    logits = jnp.where(lane < N_EXPERTS, logits, -jnp.inf)
    v1 = jnp.max(logits, axis=-1, keepdims=True)
    i1 = jnp.min(jnp.where(logits == v1, lane, float(LANES)), axis=-1, keepdims=True)
    rest = jnp.where(lane == i1, -jnp.inf, logits)
    v2 = jnp.max(rest, axis=-1, keepdims=True)
    i2 = jnp.min(jnp.where(rest == v2, lane, float(LANES)), axis=-1, keepdims=True)
    e2 = jnp.exp(v2 - v1)
    gate1 = 1.0 / (1.0 + e2)
    gate2 = e2 / (1.0 + e2)

    hot1 = (lane == i1).astype(BF16)
    hot2 = (lane == i2).astype(BF16)
    r_i = lax.broadcasted_iota(jnp.int32, (tm, tm), 0)
    c_i = lax.broadcasted_iota(jnp.int32, (tm, tm), 1)
    before = (c_i < r_i).astype(BF16)
    pre1 = jnp.dot(before, hot1, preferred_element_type=F32)
    pre2 = jnp.dot(before, hot2, preferred_element_type=F32)
    tot1 = jnp.sum(hot1.astype(F32), axis=0, keepdims=True)
    tot2 = jnp.sum(hot2.astype(F32), axis=0, keepdims=True)
    cnt = cnt_scr[...]
    rank1 = jnp.sum(jnp.where(lane == i1, pre1 + cnt, 0.0), axis=-1, keepdims=True)
    rank2 = jnp.sum(jnp.where(lane == i2, pre2 + cnt + tot1, 0.0), axis=-1, keepdims=True)
    cnt = cnt + tot1 + tot2
    cnt_scr[...] = cnt
    cnt_ref[...] = jnp.broadcast_to(cnt, cnt_ref.shape).astype(jnp.int32)

    idx_ref[...] = jnp.where(lane == 0, i1, jnp.where(lane == 1, i2, 0.0)).astype(jnp.int32)
    gate_ref[...] = jnp.where(lane == 0, gate1, jnp.where(lane == 1, gate2, 0.0))
    rank_ref[...] = jnp.where(lane == 0, rank1, jnp.where(lane == 1, rank2, 0.0)).astype(jnp.int32)


def _router(x, gain, mod, router_w, layer, row_batch, tm=512):
    n = x.shape[0]
    rw = jnp.zeros((D_MODEL, LANES), F32).at[:, :N_EXPERTS].set(router_w)
    tile = lambda i: (i, 0)
    return pl.pallas_call(
        functools.partial(_router_kernel, tm=tm),
        grid=(n // tm,),
        in_specs=[
            pl.BlockSpec((tm, D_MODEL), tile),
            pl.BlockSpec((None, 1, D_MODEL), lambda i: (layer, 0, 0)),
            _mod_spec(3, 6, row_batch, tm),
            _mod_spec(4, 6, row_batch, tm),
            pl.BlockSpec((D_MODEL, LANES), lambda i: (0, 0)),
        ],
        out_specs=[
            pl.BlockSpec((tm, D_MODEL), tile),
            pl.BlockSpec((tm, LANES), tile),
            pl.BlockSpec((tm, LANES), tile),
            pl.BlockSpec((tm, LANES), tile),
            pl.BlockSpec((8, LANES), lambda i: (0, 0)),
        ],
        out_shape=[
            jax.ShapeDtypeStruct((n, D_MODEL), F32),
            jax.ShapeDtypeStruct((n, LANES), jnp.int32),
            jax.ShapeDtypeStruct((n, LANES), F32),
            jax.ShapeDtypeStruct((n, LANES), jnp.int32),
            jax.ShapeDtypeStruct((8, LANES), jnp.int32),
        ],
        scratch_shapes=[pltpu.VMEM((1, LANES), F32)],
        compiler_params=_cparams(("arbitrary",)),
    )(x, gain, mod, mod, rw)


def _dispatch_kernel(pos_ref, pad_ref, h_ref, xs_ref, zeros, sem, zero_sem, *, tt, pad_rows):
    base = pl.program_id(0) * tt

    @pl.when(pl.program_id(0) == 0)
    def _():
        zeros[...] = jnp.zeros_like(zeros)

        def fill(e):
            start = pl.multiple_of(pad_ref[e], SUBLANES)
            return pltpu.make_async_copy(zeros, xs_ref.at[pl.ds(start, pad_rows), :], zero_sem)

        for e in range(N_EXPERTS):
            fill(e).start()
        for e in range(N_EXPERTS):
            fill(e).wait()

    def issue(t, c):
        for slot in range(2):
            pltpu.make_async_copy(h_ref.at[pl.ds(t, 1), :],
                                  xs_ref.at[pl.ds(pos_ref[2 * (base + t) + slot], 1), :],
                                  sem).start()
        return c

    lax.fori_loop(0, tt, issue, 0, unroll=ROW_DMA_UNROLL)
    for slot in range(2):
        pltpu.make_async_copy(h_ref, xs_ref.at[pl.ds(0, tt), :], sem).wait()


def _dispatch(h, pos, pad_start, rows, pad_rows, tt=256):
    n = h.shape[0]
    return pl.pallas_call(
        functools.partial(_dispatch_kernel, tt=tt, pad_rows=pad_rows),
        grid_spec=pltpu.PrefetchScalarGridSpec(
            num_scalar_prefetch=2,
            grid=(n // tt,),
            in_specs=[pl.BlockSpec((tt, D_MODEL), lambda i, pos, pad: (i, 0))],
            out_specs=pl.BlockSpec(memory_space=pl.ANY),
            scratch_shapes=[pltpu.VMEM((pad_rows, D_MODEL), F32),
                            pltpu.SemaphoreType.DMA(()), pltpu.SemaphoreType.DMA(())],
        ),
        out_shape=jax.ShapeDtypeStruct((rows, D_MODEL), F32),
        compiler_params=_cparams(("arbitrary",), has_side_effects=True),
    )(pos, pad_start, h)


def _moe_ffn_kernel(te_ref, nv_ref, x_ref, w1_ref, w3_ref, w2_ref, o_ref, h_scr):
    i = pl.program_id(0)
    j = pl.program_id(1)

    @pl.when(i < nv_ref[0])
    def _():
        @pl.when(j == 0)
        def _():
            h_scr[...] = x_ref[...].astype(BF16)
            o_ref[...] = jnp.zeros_like(o_ref)

        _swiglu_accumulate(h_scr, w1_ref, w3_ref, w2_ref, o_ref)


def _moe_ffn(xs, tile_expert, n_valid, w1, w3, w2, idx, tm, tf=1024):
    rows = tile_expert.shape[0] * tm
    d_ff = w1.shape[-1]
    assert d_ff % tf == 0 and rows <= xs.shape[0]
    nj = d_ff // tf

    def row_index(i, j, te, nv):
        return (jnp.maximum(jnp.minimum(i, nv[0] - 1), 0), 0)

    def up_index(i, j, te, nv):
        return (idx, te[i], 0, jnp.where(i < nv[0], j, nj - 1))

    def down_index(i, j, te, nv):
        return (idx, te[i], jnp.where(i < nv[0], j, nj - 1), 0)

    return pl.pallas_call(
        _moe_ffn_kernel,
        grid_spec=pltpu.PrefetchScalarGridSpec(
            num_scalar_prefetch=2,
            grid=(rows // tm, nj),
            in_specs=[
                pl.BlockSpec((tm, D_MODEL), row_index),
                pl.BlockSpec((None, None, D_MODEL, tf), up_index),
                pl.BlockSpec((None, None, D_MODEL, tf), up_index),
                pl.BlockSpec((None, None, tf, D_MODEL), down_index),
            ],
            out_specs=pl.BlockSpec((tm, D_MODEL), row_index),
            scratch_shapes=[pltpu.VMEM((tm, D_MODEL), BF16)],
        ),
        out_shape=jax.ShapeDtypeStruct((rows, D_MODEL), F32),
        compiler_params=_cparams(("arbitrary", "arbitrary"), vmem=LARGE_VMEM_LIMIT),
    )(tile_expert, n_valid, xs, w1, w3, w2)


def _combine_kernel(pos_ref, ys_ref, x_ref, gate_ref, g2_ref, o_ref, buf, sem, *, tt):
    i = pl.program_id(0)
    cur = i % 2

    def gather(tile, half):
        def issue(t, c):
            for slot in range(2):
                row = pos_ref[2 * (tile * tt + t) + slot]
                pltpu.make_async_copy(ys_ref.at[pl.ds(row, 1), :],
                                      buf.at[half, slot, pl.ds(t, 1), :], sem.at[half]).start()
            return c

        lax.fori_loop(0, tt, issue, 0, unroll=ROW_DMA_UNROLL)

    @pl.when(i == 0)
    def _():
        gather(0, 0)

    for slot in range(2):
        pltpu.make_async_copy(ys_ref.at[pl.ds(0, tt), :], buf.at[cur, slot], sem.at[cur]).wait()

    @pl.when(i + 1 < pl.num_programs(0))
    def _():
        gather(i + 1, 1 - cur)

    gates = gate_ref[...]
    f = gates[:, 0:1] * buf[cur, 0] + gates[:, 1:2] * buf[cur, 1]
    o_ref[...] = x_ref[...] + g2_ref[...] * f


def _combine_final_kernel(pos_ref, ys_ref, x_ref, gate_ref, g2_ref, gf_ref, shf_ref, scf_ref,
                          *refs, out_rows, tt):
    out_refs, (x_new, buf, sem) = refs[:len(out_rows)], refs[len(out_rows):]
    _combine_kernel(pos_ref, ys_ref, x_ref, gate_ref, g2_ref, x_new, buf, sem, tt=tt)
    y = _norm_mod(x_new[...], gf_ref[...], shf_ref[...], scf_ref[...])
    i, lo = pl.program_id(0), 0
    for o_ref, rows in zip(out_refs, out_rows):
        nt = rows // tt

        @pl.when((i >= lo) & (i < lo + nt))
        def _(o_ref=o_ref):
            o_ref[...] = y

        lo += nt


def _combine(ys, pos, x, gates, mod, row_batch, final=None, tt=256):
    n = x.shape[0]
    in_specs = [
        pl.BlockSpec(memory_space=pl.ANY),
        pl.BlockSpec((tt, D_MODEL), lambda i, pos: (i, 0)),
        pl.BlockSpec((tt, LANES), lambda i, pos: (i, 0)),
        _mod_spec(5, 6, row_batch, tt),
    ]
    args = [pos, ys, x, gates, mod]
    scratch = [pltpu.VMEM((2, 2, tt, D_MODEL), F32), pltpu.SemaphoreType.DMA((2,))]
    if final is None:
        body = functools.partial(_combine_kernel, tt=tt)
        out_specs = pl.BlockSpec((tt, D_MODEL), lambda i, pos: (i, 0))
        out_shape = jax.ShapeDtypeStruct((n, D_MODEL), F32)
    else:
        gain, fmod, out_rows = final
        body = functools.partial(_combine_final_kernel, out_rows=out_rows, tt=tt)
        in_specs += [pl.BlockSpec((1, D_MODEL), lambda i, pos: (0, 0)),
                     _mod_spec(0, 2, row_batch, tt), _mod_spec(1, 2, row_batch, tt)]
        args += [gain, fmod, fmod]
        out_specs, out_shape, lo = [], [], 0
        for rows in out_rows:
            nt = rows // tt

            def index(i, pos, lo=lo, nt=nt):
                return (jnp.clip(i - lo, 0, nt - 1), 0)

            out_specs.append(pl.BlockSpec((tt, D_MODEL), index))
            out_shape.append(jax.ShapeDtypeStruct((rows, D_MODEL), F32))
            lo += nt
        scratch = [pltpu.VMEM((tt, D_MODEL), F32)] + scratch
    return pl.pallas_call(
        body,
        grid_spec=pltpu.PrefetchScalarGridSpec(
            num_scalar_prefetch=1, grid=(n // tt,), in_specs=in_specs, out_specs=out_specs,
            scratch_shapes=scratch),
        out_shape=out_shape,
        compiler_params=_cparams(("arbitrary",)),
    )(*args)


def _moe_layer(x, gain, mod, router_w, w1, w3, w2, layer, idx, row_batch, final=None, tm=512):
    n = x.shape[0]
    h, top_idx, gates, rank, counts = _router(x, gain, mod, router_w[idx], layer, row_batch)
    counts = counts[0, :N_EXPERTS]
    padded = ((counts + tm - 1) // tm) * tm
    ends = jnp.cumsum(padded)
    starts = ends - padded
    experts = top_idx[:, :2]
    pos = (starts[experts] + rank[:, :2]).reshape(-1).astype(jnp.int32)
    n_tiles = (2 * n) // tm + N_EXPERTS
    tile_start = jnp.arange(n_tiles, dtype=jnp.int32) * tm
    tile_expert = jnp.minimum(jnp.sum(tile_start[:, None] >= ends[None, :], axis=1),
                              N_EXPERTS - 1).astype(jnp.int32)
    n_valid = (ends[-1:] // tm).astype(jnp.int32)
    pad_start = ((starts + counts) // SUBLANES * SUBLANES).astype(jnp.int32)
    xs = _dispatch(h, pos, pad_start, (n_tiles + 2) * tm, tm + SUBLANES)
    ys = _moe_ffn(xs, tile_expert, n_valid, w1, w3, w2, idx, tm)
    return _combine(ys, pos, x, gates, mod, row_batch, final)


def _final_kernel(x_ref, g_ref, sh_ref, sc_ref, o_ref):
    o_ref[...] = _norm_mod(x_ref[...], g_ref[...], sh_ref[...], sc_ref[...])


def _final(x, gain, fmod, row_lo, rows, row_batch, tm=512):
    off = row_lo // tm

    def mod_spec(which):
        return pl.BlockSpec((None, 1, D_MODEL),
                            lambda i: (row_batch((i + off) * tm) * 2 + which, 0, 0))

    return pl.pallas_call(
        _final_kernel,
        grid=(rows // tm,),
        in_specs=[
            pl.BlockSpec((tm, D_MODEL), lambda i: (i + off, 0)),
            pl.BlockSpec((1, D_MODEL), lambda i: (0, 0)),
            mod_spec(0),
            mod_spec(1),
        ],
        out_specs=pl.BlockSpec((tm, D_MODEL), lambda i: (i, 0)),
        out_shape=jax.ShapeDtypeStruct((rows, D_MODEL), F32),
        compiler_params=_cparams(("parallel",)),
    )(x, gain, fmod, fmod)


def kernel(x_prompt, x_sample, c_prompt, c_sample, w_mod, b_mod, norm_mix, w_in, w_fourier, w_out,
           norm_ffn, dense_w1, dense_w3, dense_w2, router_w, moe_w1, moe_w3, moe_w2,
           w_final_mod, b_final_mod, norm_final):
    batch, seq_p, d = x_prompt.shape
    batch_s, seq_s, _ = x_sample.shape
    n_p, n_s = batch * seq_p, batch_s * seq_s
    n = n_p + n_s
    depth = w_mod.shape[0]
    assert d == D_MODEL and batch_s == 1 and n_p % seq_s == 0 and batch + batch_s <= MOD_ROWS

    def row_batch(row):
        return jnp.minimum(row // seq_p, batch)

    x = (x_prompt.reshape(n_p, d), x_sample.reshape(n_s, d))
    c_all = jnp.zeros((MOD_ROWS, d), F32).at[:batch].set(c_prompt).at[batch:batch + 1].set(c_sample)

    mod = _modulation(c_all, w_mod, b_mod)
    fmod = _modulation(c_all, w_final_mod[None], b_final_mod[None])[0]
    fmod = fmod.reshape(MOD_ROWS * 2, 1, d)

    gain_mix = norm_mix.reshape(depth, 1, d)
    gain_ffn = norm_ffn.reshape(depth, 1, d)
    gain_final = norm_final.reshape(1, d)
    w_in_b = w_in.astype(BF16)
    w_fourier_b = w_fourier.astype(BF16)
    f32 = dict(w_out=w_out, dense_w1=dense_w1, dense_w3=dense_w3, dense_w2=dense_w2,
               moe_w1=moe_w1, moe_w3=moe_w3, moe_w2=moe_w2)
    bf = {}

    def jobs(names):
        names = [k for k in names if k not in bf and f32[k].size]
        return names, tuple(f32[k].reshape(-1, f32[k].shape[-1]) for k in names)

    def done(names, casts):
        for k, c in zip(names, casts):
            bf[k] = c.reshape(f32[k].shape)

    def get(*names):
        for k in names:
            if k not in bf:
                bf[k] = f32[k].astype(BF16)
        return [bf[k] for k in names]

    rope_p = _rope_tables(seq_p)
    rope_s = _rope_tables(seq_s)

    for l in range(depth):
        mod_l = mod[l].reshape(MOD_ROWS * 6, 1, d)
        z = _in_proj(x, gain_mix, mod_l, w_in_b, l, row_batch)
        names, side = jobs(["w_out", "dense_w1", "dense_w3", "dense_w2"]
                           + (["moe_w2"] if l % 2 else []))
        o_attn, cast = _attention(z, None, *rope_p, seq_p, 0, batch, side)
        done(names, cast)
        o_attn, _ = _attention(z, o_attn, *rope_s, seq_s, n_p // seq_s, batch_s)
        o_f = _fourier(z, None, w_fourier_b, l, seq_p, 0, batch)
        o_f = _fourier(z, o_f, w_fourier_b, l, seq_s, n_p // seq_s, batch_s)
        x = _out_proj(o_attn, o_f, x, mod_l, *get("w_out"), l, row_batch)
        if l % 2 == 0:
            names, side = jobs(["moe_w1", "moe_w3"])
            x, cast = _ffn_dense(x, gain_ffn, mod_l, *get("dense_w1", "dense_w3", "dense_w2"),
                                 l, l // 2, row_batch, side)
            done(names, cast)
            x = (x,)
        else:
            final = (gain_final, fmod, (n_p, n_s)) if l == depth - 1 else None
            x = _moe_layer(x, gain_ffn, mod_l, router_w, *get("moe_w1", "moe_w3", "moe_w2"),
                           l, l // 2, row_batch, final)
            x = tuple(x) if final else (x,)

    if depth % 2 == 0:
        y_p, y_s = x
    else:
        y_p = _final(x[0], gain_final, fmod, 0, n_p, row_batch)
        y_s = _final(x[0], gain_final, fmod, n_p, n_s, row_batch)
    return (y_p.reshape(batch, seq_p, d), y_s.reshape(batch_s, seq_s, d))
```

```python
import functools

import numpy as np
import jax
import jax.numpy as jnp
from jax import lax
from jax.experimental import pallas as pl
from jax.experimental.pallas import tpu as pltpu

F32 = jnp.float32
BF16 = jnp.bfloat16

D_MODEL = 2048
HEAD_DIM = 128
N_ATTN_HEADS = 12
ATTN_WIDTH = N_ATTN_HEADS * HEAD_DIM
N_FOURIER_GROUPS = 4
FOURIER_GROUP = 128
FOURIER_WIDTH = N_FOURIER_GROUPS * FOURIER_GROUP
IN_WIDTH = 3 * ATTN_WIDTH + FOURIER_WIDTH
DILATIONS = (1, 4, 16)
HALF_NEIGHBOURS = 64
ROPE_THETA = 10000.0
N_EXPERTS = 8
EPS = 1e-6
NEG_INF = -1e30
LANES = 128
SUBLANES = 8
MOD_ROWS = 16

VMEM_LIMIT = 48 * 1024 * 1024
ATTN_VMEM_LIMIT = 58 * 1024 * 1024
BLOCK_GROUP = 4
DFT_GROUP = 8
NORM_ROWS = 16
ROW_DMA_UNROLL = 8
DOWN_CHUNK = 512
LARGE_VMEM_LIMIT = 56 * 1024 * 1024


def _cparams(sem, vmem=VMEM_LIMIT, **kw):
    return pltpu.CompilerParams(dimension_semantics=sem, vmem_limit_bytes=vmem, **kw)


def _norm_mod(x, g, sh, sc):
    ms = jnp.mean(x * x, axis=-1, keepdims=True)
    return (x * lax.rsqrt(ms + EPS)) * g * (1.0 + sc) + sh


def _norm_mod_rows(x_ref, g_ref, sh_ref, sc_ref, out_ref):
    g, sh, sc = g_ref[...], sh_ref[...], sc_ref[...]

    def body(c, carry):
        rows = pl.ds(pl.multiple_of(c * NORM_ROWS, NORM_ROWS), NORM_ROWS)
        out_ref[rows, :] = _norm_mod(x_ref[rows, :], g, sh, sc).astype(out_ref.dtype)
        return carry

    lax.fori_loop(0, x_ref.shape[0] // NORM_ROWS, body, 0, unroll=8)


def _silu(a):
    return a / (1.0 + jnp.exp(-a))


def _mod_kernel(c_ref, w_ref, b_ref, o_ref):
    cs = _silu(c_ref[...]).astype(BF16)
    o_ref[...] = jnp.dot(cs, w_ref[...].astype(BF16), preferred_element_type=F32) + b_ref[...]


def _modulation(c_all, w, b, tn=1024):
    nl, d, n = w.shape
    return pl.pallas_call(
        _mod_kernel,
        grid=(nl, n // tn),
        in_specs=[
            pl.BlockSpec((MOD_ROWS, d), lambda l, j: (0, 0)),
            pl.BlockSpec((None, d, tn), lambda l, j: (l, 0, j)),
            pl.BlockSpec((None, 1, tn), lambda l, j: (l, 0, j)),
        ],
        out_specs=pl.BlockSpec((None, MOD_ROWS, tn), lambda l, j: (l, 0, j)),
        out_shape=jax.ShapeDtypeStruct((nl, MOD_ROWS, n), F32),
        compiler_params=_cparams(("parallel", "parallel")),
    )(c_all, w, b.reshape(nl, 1, n))


def _mod_spec(which, n_vec, row_batch, tm):
    def index(i, *_):
        return (row_batch(i * tm) * n_vec + which, 0, 0)
    return pl.BlockSpec((None, 1, D_MODEL), index)


def _token_specs(xs, tm, width, column_tiled=False):
    specs, lo = [], 0
    for x in xs:
        nt = x.shape[0] // tm

        def index(i, j=0, *_, lo=lo, nt=nt):
            return (jnp.clip(i - lo, 0, nt - 1), j if column_tiled else 0)

        specs.append(pl.BlockSpec((tm, width), index))
        lo += nt
    return specs


def _with_token_tile(x_refs, src_rows, tm, fn):
    if len(x_refs) == 1:
        fn(x_refs[0])
        return
    i, lo = pl.program_id(0), 0
    for x_ref, rows in zip(x_refs, src_rows):
        nt = rows // tm
        pl.when((i >= lo) & (i < lo + nt))(functools.partial(fn, x_ref))
        lo += nt


def _in_proj_kernel(*refs, src_rows, tm):
    ns = len(src_rows)
    g_ref, sh_ref, sc_ref, w_ref, o_ref, h_scr = refs[ns:]

    def prepare(x_ref):
        _norm_mod_rows(x_ref, g_ref, sh_ref, sc_ref, h_scr)

    @pl.when(pl.program_id(1) == 0)
    def _():
        _with_token_tile(refs[:ns], src_rows, tm, prepare)

    o_ref[...] = jnp.dot(h_scr[...], w_ref[...], preferred_element_type=F32).astype(o_ref.dtype)


def _in_proj(xs, gain, mod, w_in, layer, row_batch, tm=1024, tn=1280):
    n = sum(x.shape[0] for x in xs)
    z, lo = None, 0
    for x in xs:
        lo_t = lo // tm
        in_specs = [
            pl.BlockSpec((tm, D_MODEL), lambda i, j: (i, 0)),
            pl.BlockSpec((None, 1, D_MODEL), lambda i, j: (layer, 0, 0)),
            _mod_spec(0, 6, lambda row, lo=lo: row_batch(row + lo), tm),
            _mod_spec(1, 6, lambda row, lo=lo: row_batch(row + lo), tm),
            pl.BlockSpec((None, D_MODEL, tn), lambda i, j: (layer, 0, j)),
        ]
        args = [x, gain, mod, mod, w_in]
        if z is not None:
            in_specs.append(pl.BlockSpec(memory_space=pl.ANY))
            args.append(z)

        def kern(*refs, n_in=len(args)):
            _in_proj_kernel(*refs[:5], *refs[n_in:], src_rows=(tm,), tm=tm)

        z = pl.pallas_call(
            kern,
            grid=(x.shape[0] // tm, IN_WIDTH // tn),
            in_specs=in_specs,
            out_specs=pl.BlockSpec((tm, tn), lambda i, j, lo_t=lo_t: (i + lo_t, j)),
            out_shape=jax.ShapeDtypeStruct((n, IN_WIDTH), BF16),
            scratch_shapes=[pltpu.VMEM((tm, D_MODEL), BF16)],
            input_output_aliases={} if z is None else {5: 0},
            compiler_params=_cparams(("parallel", "arbitrary")),
        )(*args)
        lo += x.shape[0]
    return z


def _attn_kernel(q_ref, k_ref, v_ref, cos_ref, sin_ref, o_ref,
                 qr, kr, vr, acc, mx, den, bias, *, seq, bq):
    chunk = bq * DILATIONS[-1]

    @pl.when((pl.program_id(0) == 0) & (pl.program_id(1) == 0))
    def _():
        for p, r in enumerate(DILATIONS):
            kw = min(bq + 2 * HALF_NEIGHBOURS, seq // r)
            rel = (lax.broadcasted_iota(jnp.int32, (bq, kw), 1)
                   - lax.broadcasted_iota(jnp.int32, (bq, kw), 0))
            for case, off in enumerate((0, -HALF_NEIGHBOURS, bq - kw)):
                bias[p, case, :, :kw] = jnp.where(jnp.abs(rel + off) <= HALF_NEIGHBOURS,
                                                  0.0, NEG_INF)

    cos = cos_ref[...]
    sin = sin_ref[...]
    q = q_ref[...].astype(F32)
    qr[...] = (q * cos + pltpu.roll(q, HEAD_DIM // 2, 1) * sin) * (1.0 / np.sqrt(HEAD_DIM))
    k = k_ref[...].astype(F32)
    kr[...] = k * cos + pltpu.roll(k, HEAD_DIM // 2, 1) * sin
    vr[...] = v_ref[...].astype(F32)

    def do_chunk(c, carry):
        base = c * chunk
        for p, r in enumerate(DILATIONS):
            sub_len = seq // r
            kw = min(bq + 2 * HALF_NEIGHBOURS, sub_len)
            nb = chunk // (r * bq)
            group = BLOCK_GROUP * (bq + 2 * HALF_NEIGHBOURS) // kw

            def do_blocks(t, carry2, p=p, r=r, sub_len=sub_len, kw=kw, nb=nb, group=group):
                work = []
                for g in range(group):
                    tg = t * group + g
                    m = tg // nb
                    i = tg % nb
                    q0 = (c * nb + i) * bq
                    k0 = jnp.clip(q0 - HALF_NEIGHBOURS, 0, sub_len - kw)
                    qb = qr[pl.ds(m + r * q0, bq, stride=r), :].astype(BF16)
                    kb = kr[pl.ds(m + r * k0, kw, stride=r), :].astype(BF16)
                    vb = vr[pl.ds(m + r * k0, kw, stride=r), :].astype(BF16)
                    dst = pl.ds(m + r * (i * bq), bq, stride=r)
                    work.append((qb, kb, vb, k0 - q0, dst))
                scores = [lax.dot_general(qb, kb, (((1,), (1,)), ((), ())),
                                          preferred_element_type=F32)
                          for qb, kb, _, _, _ in work]
                weights = []
                for s, (_, _, _, off, dst) in zip(scores, work):
                    case = jnp.where(off == 0, 0, jnp.where(off == -HALF_NEIGHBOURS, 1, 2))
                    s = s + bias[p, case, :, :kw]
                    smax = jnp.max(s, axis=-1, keepdims=True)
                    e = jnp.exp(s - smax)
                    mx[p, dst, :] = jnp.broadcast_to(smax, (bq, HEAD_DIM))
                    den[p, dst, :] = jnp.broadcast_to(jnp.sum(e, axis=-1, keepdims=True),
                                                      (bq, HEAD_DIM))
                    weights.append(e.astype(BF16))
                for e, (_, _, vb, _, dst) in zip(weights, work):
                    acc[p, dst, :] = jnp.dot(e, vb, preferred_element_type=F32)
                return carry2

            lax.fori_loop(0, (r * nb) // group, do_blocks, 0)

        top = jnp.maximum(jnp.maximum(mx[0], mx[1]), mx[2])
        num = jnp.zeros((chunk, HEAD_DIM), F32)
        tot = jnp.zeros((chunk, HEAD_DIM), F32)
        for p in range(len(DILATIONS)):
            w = jnp.exp(mx[p] - top)
            num = num + w * acc[p]
            tot = tot + w * den[p]
        o_ref[pl.ds(pl.multiple_of(base, chunk), chunk), :] = (num / tot).astype(o_ref.dtype)
        return carry

    lax.fori_loop(0, seq // chunk, do_chunk, 0)


def _attention(z, o_prev, rope_cos, rope_sin, seq, batch_lo, batch_n, side=(), bq=128):
    n = z.shape[0]
    chunk = bq * DILATIONS[-1]
    assert seq % chunk == 0 and n % seq == 0
    zb = z.reshape(n // seq, seq, IN_WIDTH)

    def col_spec(off):
        return pl.BlockSpec((None, seq, HEAD_DIM),
                            lambda b, h: (batch_lo + b, 0, off + h))

    in_specs = [
        col_spec(0), col_spec(N_ATTN_HEADS), col_spec(2 * N_ATTN_HEADS),
        pl.BlockSpec((seq, HEAD_DIM), lambda b, h: (0, 0), pipeline_mode=pl.Buffered(1)),
        pl.BlockSpec((seq, HEAD_DIM), lambda b, h: (0, 0), pipeline_mode=pl.Buffered(1)),
    ]
    args = [zb, zb, zb, rope_cos, rope_sin]
    aliases = {}
    if o_prev is not None:
        in_specs.append(pl.BlockSpec(memory_space=pl.ANY))
        args.append(o_prev.reshape(n // seq, seq, ATTN_WIDTH))
        aliases = {5: 0}
    n_main = len(args)
    side_specs, side_shapes = _side_cast_specs(side, batch_n * N_ATTN_HEADS,
                                               lambda b, h: b * N_ATTN_HEADS + h)

    def kern(*refs):
        ins, rest = refs[:n_main + len(side)], refs[n_main + len(side):]
        o_ref, side_dst, scratch = rest[0], rest[1:1 + len(side)], rest[1 + len(side):]
        _attn_kernel(*ins[:5], o_ref, *scratch, seq=seq, bq=bq)
        for src, dst in zip(ins[n_main:], side_dst):
            dst[...] = src[...].astype(dst.dtype)

    out = pl.pallas_call(
        kern,
        grid=(batch_n, N_ATTN_HEADS),
        in_specs=in_specs + side_specs,
        out_specs=[pl.BlockSpec((None, seq, HEAD_DIM), lambda b, h: (batch_lo + b, 0, h))]
        + side_specs,
        out_shape=[jax.ShapeDtypeStruct((n // seq, seq, ATTN_WIDTH), BF16)] + side_shapes,
        scratch_shapes=[
            pltpu.VMEM((seq, HEAD_DIM), F32),
            pltpu.VMEM((seq, HEAD_DIM), F32),
            pltpu.VMEM((seq, HEAD_DIM), F32),
            pltpu.VMEM((len(DILATIONS), chunk, HEAD_DIM), F32),
            pltpu.VMEM((len(DILATIONS), chunk, HEAD_DIM), F32),
            pltpu.VMEM((len(DILATIONS), chunk, HEAD_DIM), F32),
            pltpu.VMEM((len(DILATIONS), 3, bq, bq + 2 * HALF_NEIGHBOURS), F32),
        ],
        input_output_aliases=aliases,
        compiler_params=_cparams(("arbitrary", "arbitrary"), vmem=ATTN_VMEM_LIMIT),
    )(*args, *side)
    return out[0].reshape(n, ATTN_WIDTH), out[1:]


def _rope_tables(seq):
    half = HEAD_DIM // 2
    inv = jnp.power(ROPE_THETA, -jnp.arange(half, dtype=F32) * 2.0 / HEAD_DIM)
    ang = jnp.arange(seq, dtype=F32)[:, None] * inv[None, :]
    cos, sin = jnp.cos(ang), jnp.sin(ang)
    return jnp.concatenate([cos, cos], axis=-1), jnp.concatenate([-sin, sin], axis=-1)


def _dft_factors(seq):
    s1 = {2048: 32, 8192: 64}.get(seq)
    if s1 is None:
        s1 = 1
        while s1 * s1 < seq:
            s1 *= 2
        s1 = seq // s1 if (seq // s1) * s1 == seq else s1
    return s1, seq // s1


@functools.lru_cache(maxsize=None)
def _dft_tables(seq):
    s1, s2 = _dft_factors(seq)
    c = np.arange(FOURIER_GROUP)
    ang_c = 2.0 * np.pi * ((c[:, None] * c[None, :]) % FOURIER_GROUP) / FOURIER_GROUP
    norm = 1.0 / np.sqrt(float(seq) * FOURIER_GROUP)
    w_chan = np.concatenate([np.cos(ang_c), -np.sin(ang_c)], axis=1) * norm
    k1 = np.arange(s1)[None, :, None]
    n1 = np.arange(s1)[None, None, :]
    n2 = np.arange(s2)[:, None, None]
    ang1 = 2.0 * np.pi * ((k1 * (n2 + s2 * n1)) % seq) / seq
    stage1 = np.concatenate([np.cos(ang1), np.sin(ang1)], axis=1)
    j = np.arange(s2)
    ang2 = 2.0 * np.pi * ((j[:, None] * j[None, :]) % s2) / s2
    return (np.asarray(w_chan, np.float32), np.asarray(stage1, np.float32),
            np.asarray(np.cos(ang2), np.float32), np.asarray(np.sin(ang2), np.float32))


def _fourier_kernel(f_ref, wc_ref, st1_ref, c2_ref, s2_ref, wf_ref, o_ref,
                    zr_scr, zi_scr, tr_scr, ti_scr, y_scr, *, s1, s2):
    g = FOURIER_GROUP
    z = jnp.dot(f_ref[...], wc_ref[...], preferred_element_type=F32)
    zr_scr[...] = z[:, :g]
    zi_scr[...] = z[:, g:]

    def stage1(t, carry):
        n2s = [t * DFT_GROUP + u for u in range(DFT_GROUP)]
        zs = []
        for n2 in n2s:
            rows = pl.ds(n2, s1, stride=s2)
            zs.append(jnp.concatenate([zr_scr[rows, :], zi_scr[rows, :]], axis=1).astype(BF16))
        prs = [jnp.dot(st1_ref[n2], z, preferred_element_type=F32)
               for n2, z in zip(n2s, zs)]
        for n2, pr in zip(n2s, prs):
            dst = pl.ds(pl.multiple_of(n2 * s1, s1), s1)
            tr_scr[dst, :] = pr[:s1, :g] + pr[s1:, g:]
            ti_scr[dst, :] = pr[:s1, g:] - pr[s1:, :g]
        return carry

    lax.fori_loop(0, s2 // DFT_GROUP, stage1, 0)

    def stage2(t, carry):
        rows = [pl.ds(t * DFT_GROUP + u, s2, stride=s1) for u in range(DFT_GROUP)]
        ts = [(tr_scr[r, :].astype(BF16), ti_scr[r, :].astype(BF16)) for r in rows]
        ys = [jnp.dot(c2_ref[...], tr, preferred_element_type=F32)
              + jnp.dot(s2_ref[...], ti, preferred_element_type=F32) for tr, ti in ts]
        for r, y in zip(rows, ys):
            y_scr[r, :] = y
        return carry

    lax.fori_loop(0, s1 // DFT_GROUP, stage2, 0)
    o_ref[...] = jnp.dot(y_scr[...].astype(BF16), wf_ref[...],
                         preferred_element_type=F32).astype(o_ref.dtype)


def _fourier(z, o_prev, w_fourier, layer, seq, batch_lo, batch_n):
    n = z.shape[0]
    s1, s2 = _dft_factors(seq)
    w_chan, stage1, c2, sn2 = (jnp.asarray(t, BF16) for t in _dft_tables(seq))
    zb = z.reshape(n // seq, seq, IN_WIDTH)
    f_col = 3 * N_ATTN_HEADS
    in_specs = [
        pl.BlockSpec((None, seq, FOURIER_GROUP), lambda b, g: (batch_lo + b, 0, f_col + g)),
        pl.BlockSpec((FOURIER_GROUP, 2 * FOURIER_GROUP), lambda b, g: (0, 0)),
        pl.BlockSpec((s2, 2 * s1, s1), lambda b, g: (0, 0, 0)),
        pl.BlockSpec((s2, s2), lambda b, g: (0, 0)),
        pl.BlockSpec((s2, s2), lambda b, g: (0, 0)),
        pl.BlockSpec((None, None, FOURIER_GROUP, FOURIER_GROUP), lambda b, g: (layer, g, 0, 0)),
    ]
    args = [zb, w_chan, stage1, c2, sn2, w_fourier]
    aliases = {}
    if o_prev is not None:
        in_specs.append(pl.BlockSpec(memory_space=pl.ANY))
        args.append(o_prev.reshape(n // seq, seq, FOURIER_WIDTH))
        aliases = {6: 0}

    def kern(*refs):
        if o_prev is not None:
            refs = refs[:6] + refs[7:]
        _fourier_kernel(*refs, s1=s1, s2=s2)

    out = pl.pallas_call(
        kern,
        grid=(batch_n, N_FOURIER_GROUPS),
        in_specs=in_specs,
        out_specs=pl.BlockSpec((None, seq, FOURIER_GROUP), lambda b, g: (batch_lo + b, 0, g)),
        out_shape=jax.ShapeDtypeStruct((n // seq, seq, FOURIER_WIDTH), BF16),
        scratch_shapes=[pltpu.VMEM((seq, FOURIER_GROUP), F32)] * 5,
        input_output_aliases=aliases,
        compiler_params=_cparams(("parallel", "parallel")),
    )(*args)
    return out.reshape(n, FOURIER_WIDTH)


def _out_proj_kernel(oa_ref, of_ref, wa_ref, wf_ref, g_ref, *refs, src_rows, tm):
    ns = len(src_rows)
    o_ref = refs[ns]
    y = jnp.dot(oa_ref[...], wa_ref[...], preferred_element_type=F32)
    y = y + jnp.dot(of_ref[...], wf_ref[...], preferred_element_type=F32)
    gated = g_ref[...] * y

    def residual(x_ref):
        o_ref[...] = x_ref[...] + gated

    _with_token_tile(refs[:ns], src_rows, tm, residual)


def _out_proj(o_attn, o_f, xs, mod, w_out, layer, row_batch, tm=512, tn=D_MODEL):
    n = sum(x.shape[0] for x in xs)
    src_rows = tuple(x.shape[0] for x in xs)
    fblk = ATTN_WIDTH // FOURIER_WIDTH

    def gate_index(i, j):
        return (row_batch(i * tm) * 6 + 2, 0, j)

    return pl.pallas_call(
        functools.partial(_out_proj_kernel, src_rows=src_rows, tm=tm),
        grid=(n // tm, D_MODEL // tn),
        in_specs=[
            pl.BlockSpec((tm, ATTN_WIDTH), lambda i, j: (i, 0)),
            pl.BlockSpec((tm, FOURIER_WIDTH), lambda i, j: (i, 0)),
            pl.BlockSpec((None, ATTN_WIDTH, tn), lambda i, j: (layer, 0, j),
                         pipeline_mode=pl.Buffered(1)),
            pl.BlockSpec((None, FOURIER_WIDTH, tn), lambda i, j: (layer, fblk, j),
                         pipeline_mode=pl.Buffered(1)),
            pl.BlockSpec((None, 1, tn), gate_index),
        ] + _token_specs(xs, tm, tn, column_tiled=True),
        out_specs=pl.BlockSpec((tm, tn), lambda i, j: (i, j)),
        out_shape=jax.ShapeDtypeStruct((n, D_MODEL), F32),
        compiler_params=_cparams(("parallel", "parallel")),
    )(o_attn, o_f, w_out, w_out, mod, *xs)


def _swiglu_accumulate(h_scr, w1_ref, w3_ref, w2_ref, o_ref):
    h = h_scr[...]
    a = jnp.dot(h, w1_ref[...], preferred_element_type=F32)
    b = jnp.dot(h, w3_ref[...], preferred_element_type=F32)
    g = (_silu(a) * b).astype(BF16)
    for c in range(0, D_MODEL, DOWN_CHUNK):
        o_ref[:, c:c + DOWN_CHUNK] += jnp.dot(g, w2_ref[:, c:c + DOWN_CHUNK],
                                              preferred_element_type=F32)


def _ffn_dense_kernel(x_ref, g_ref, sh_ref, sc_ref, gate_ref, w1_ref, w3_ref, w2_ref, *refs,
                      n_side):
    side_src, o_ref, side_dst, h_scr = (refs[:n_side], refs[n_side],
                                        refs[n_side + 1:2 * n_side + 1], refs[2 * n_side + 1])
    j = pl.program_id(1)

    @pl.when(j == 0)
    def _():
        _norm_mod_rows(x_ref, g_ref, sh_ref, sc_ref, h_scr)
        o_ref[...] = jnp.zeros_like(o_ref)

    _swiglu_accumulate(h_scr, w1_ref, w3_ref, w2_ref, o_ref)
    for src, dst in zip(side_src, side_dst):
        dst[...] = src[...].astype(dst.dtype)

    @pl.when(j == pl.num_programs(1) - 1)
    def _():
        o_ref[...] = x_ref[...] + gate_ref[...] * o_ref[...]


def _side_block_rows(rows, n_steps):
    for b in range(2 * SUBLANES, rows + 1, 2 * SUBLANES):
        if rows % b == 0 and rows // b <= n_steps:
            return b
    raise ValueError("no row block")


def _side_cast_specs(side, n_steps, step):
    specs, shapes = [], []
    for a in side:
        rows, cols = a.shape
        blk = _side_block_rows(rows, n_steps)

        def index(i, j, last=rows // blk - 1):
            return (jnp.minimum(step(i, j), last), 0)

        specs.append(pl.BlockSpec((blk, cols), index))
        shapes.append(jax.ShapeDtypeStruct((rows, cols), BF16))
    return specs, shapes


def _ffn_dense(x, gain, mod, w1, w3, w2, layer, idx, row_batch, side=(), tm=1024, tf=512):
    n = x.shape[0]
    d_ff = w1.shape[-1]
    nj = d_ff // tf
    side_specs, side_shapes = _side_cast_specs(side, (n // tm) * nj, lambda i, j: i * nj + j)
    out = pl.pallas_call(
        functools.partial(_ffn_dense_kernel, n_side=len(side)),
        grid=(n // tm, nj),
        in_specs=[
            pl.BlockSpec((tm, D_MODEL), lambda i, j: (i, 0), pipeline_mode=pl.Buffered(1)),
            pl.BlockSpec((None, 1, D_MODEL), lambda i, j: (layer, 0, 0)),
            _mod_spec(3, 6, row_batch, tm),
            _mod_spec(4, 6, row_batch, tm),
            _mod_spec(5, 6, row_batch, tm),
            pl.BlockSpec((None, D_MODEL, tf), lambda i, j: (idx, 0, j)),
            pl.BlockSpec((None, D_MODEL, tf), lambda i, j: (idx, 0, j)),
            pl.BlockSpec((None, tf, D_MODEL), lambda i, j: (idx, j, 0)),
        ] + side_specs,
        out_specs=[pl.BlockSpec((tm, D_MODEL), lambda i, j: (i, 0))] + side_specs,
        out_shape=[jax.ShapeDtypeStruct((n, D_MODEL), F32)] + side_shapes,
        scratch_shapes=[pltpu.VMEM((tm, D_MODEL), BF16)],
        compiler_params=_cparams(("arbitrary", "arbitrary"), vmem=LARGE_VMEM_LIMIT),
    )(x, gain, mod, mod, mod, w1, w3, w2, *side)
    return out[0], out[1:]


def _router_kernel(x_ref, g_ref, sh_ref, sc_ref, rw_ref, h_ref, idx_ref, gate_ref, rank_ref,
                   cnt_ref, cnt_scr, *, tm):
    @pl.when(pl.program_id(0) == 0)
    def _():
        cnt_scr[...] = jnp.zeros_like(cnt_scr)

    h = _norm_mod(x_ref[...], g_ref[...], sh_ref[...], sc_ref[...])
    h_ref[...] = h
    logits = jnp.dot(h, rw_ref[...], preferred_element_type=F32,
                     precision=lax.Precision.HIGHEST)
    lane = lax.broadcasted_iota(jnp.int32, (tm, LANES), 1).astype(F32)
    logits = jnp.where(lane < N_EXPERTS, logits, -jnp.inf)
    v1 = jnp.max(logits, axis=-1, keepdims=True)
    i1 = jnp.min(jnp.where(logits == v1, lane, float(LANES)), axis=-1, keepdims=True)
    rest = jnp.where(lane == i1, -jnp.inf, logits)
    v2 = jnp.max(rest, axis=-1, keepdims=True)
    i2 = jnp.min(jnp.where(rest == v2, lane, float(LANES)), axis=-1, keepdims=True)
    e2 = jnp.exp(v2 - v1)
    gate1 = 1.0 / (1.0 + e2)
    gate2 = e2 / (1.0 + e2)

    hot1 = (lane == i1).astype(BF16)
    hot2 = (lane == i2).astype(BF16)
    r_i = lax.broadcasted_iota(jnp.int32, (tm, tm), 0)
    c_i = lax.broadcasted_iota(jnp.int32, (tm, tm), 1)
    before = (c_i < r_i).astype(BF16)
    pre1 = jnp.dot(before, hot1, preferred_element_type=F32)
    pre2 = jnp.dot(before, hot2, preferred_element_type=F32)
    tot1 = jnp.sum(hot1.astype(F32), axis=0, keepdims=True)
    tot2 = jnp.sum(hot2.astype(F32), axis=0, keepdims=True)
    cnt = cnt_scr[...]
    rank1 = jnp.sum(jnp.where(lane == i1, pre1 + cnt, 0.0), axis=-1, keepdims=True)
    rank2 = jnp.sum(jnp.where(lane == i2, pre2 + cnt + tot1, 0.0), axis=-1, keepdims=True)
    cnt = cnt + tot1 + tot2
    cnt_scr[...] = cnt
    cnt_ref[...] = jnp.broadcast_to(cnt, cnt_ref.shape).astype(jnp.int32)

    idx_ref[...] = jnp.where(lane == 0, i1, jnp.where(lane == 1, i2, 0.0)).astype(jnp.int32)
    gate_ref[...] = jnp.where(lane == 0, gate1, jnp.where(lane == 1, gate2, 0.0))
    rank_ref[...] = jnp.where(lane == 0, rank1, jnp.where(lane == 1, rank2, 0.0)).astype(jnp.int32)


def _router(x, gain, mod, router_w, layer, row_batch, tm=512):
    n = x.shape[0]
    rw = jnp.zeros((D_MODEL, LANES), F32).at[:, :N_EXPERTS].set(router_w)
    tile = lambda i: (i, 0)
    return pl.pallas_call(
        functools.partial(_router_kernel, tm=tm),
        grid=(n // tm,),
        in_specs=[
            pl.BlockSpec((tm, D_MODEL), tile),
            pl.BlockSpec((None, 1, D_MODEL), lambda i: (layer, 0, 0)),
            _mod_spec(3, 6, row_batch, tm),
            _mod_spec(4, 6, row_batch, tm),
            pl.BlockSpec((D_MODEL, LANES), lambda i: (0, 0)),
        ],
        out_specs=[
            pl.BlockSpec((tm, D_MODEL), tile),
            pl.BlockSpec((tm, LANES), tile),
            pl.BlockSpec((tm, LANES), tile),
            pl.BlockSpec((tm, LANES), tile),
            pl.BlockSpec((8, LANES), lambda i: (0, 0)),
        ],
        out_shape=[
            jax.ShapeDtypeStruct((n, D_MODEL), F32),
            jax.ShapeDtypeStruct((n, LANES), jnp.int32),
            jax.ShapeDtypeStruct((n, LANES), F32),
            jax.ShapeDtypeStruct((n, LANES), jnp.int32),
            jax.ShapeDtypeStruct((8, LANES), jnp.int32),
        ],
        scratch_shapes=[pltpu.VMEM((1, LANES), F32)],
        compiler_params=_cparams(("arbitrary",)),
    )(x, gain, mod, mod, rw)


def _dispatch_kernel(pos_ref, pad_ref, h_ref, xs_ref, zeros, sem, zero_sem, *, tt, pad_rows):
    base = pl.program_id(0) * tt

    @pl.when(pl.program_id(0) == 0)
    def _():
        zeros[...] = jnp.zeros_like(zeros)

        def fill(e):
            start = pl.multiple_of(pad_ref[e], SUBLANES)
            return pltpu.make_async_copy(zeros, xs_ref.at[pl.ds(start, pad_rows), :], zero_sem)

        for e in range(N_EXPERTS):
            fill(e).start()
        for e in range(N_EXPERTS):
            fill(e).wait()

    def issue(t, c):
        for slot in range(2):
            pltpu.make_async_copy(h_ref.at[pl.ds(t, 1), :],
                                  xs_ref.at[pl.ds(pos_ref[2 * (base + t) + slot], 1), :],
                                  sem).start()
        return c

    lax.fori_loop(0, tt, issue, 0, unroll=ROW_DMA_UNROLL)
    for slot in range(2):
        pltpu.make_async_copy(h_ref, xs_ref.at[pl.ds(0, tt), :], sem).wait()


def _dispatch(h, pos, pad_start, rows, pad_rows, tt=256):
    n = h.shape[0]
    return pl.pallas_call(
        functools.partial(_dispatch_kernel, tt=tt, pad_rows=pad_rows),
        grid_spec=pltpu.PrefetchScalarGridSpec(
            num_scalar_prefetch=2,
            grid=(n // tt,),
            in_specs=[pl.BlockSpec((tt, D_MODEL), lambda i, pos, pad: (i, 0))],
            out_specs=pl.BlockSpec(memory_space=pl.ANY),
            scratch_shapes=[pltpu.VMEM((pad_rows, D_MODEL), F32),
                            pltpu.SemaphoreType.DMA(()), pltpu.SemaphoreType.DMA(())],
        ),
        out_shape=jax.ShapeDtypeStruct((rows, D_MODEL), F32),
        compiler_params=_cparams(("arbitrary",), has_side_effects=True),
    )(pos, pad_start, h)


def _moe_ffn_kernel(te_ref, nv_ref, x_ref, w1_ref, w3_ref, w2_ref, o_ref, h_scr):
    i = pl.program_id(0)
    j = pl.program_id(1)

    @pl.when(i < nv_ref[0])
    def _():
        @pl.when(j == 0)
        def _():
            h_scr[...] = x_ref[...].astype(BF16)
            o_ref[...] = jnp.zeros_like(o_ref)

        _swiglu_accumulate(h_scr, w1_ref, w3_ref, w2_ref, o_ref)


def _moe_ffn(xs, tile_expert, n_valid, w1, w3, w2, idx, tm, tf=1024):
    rows = tile_expert.shape[0] * tm
    d_ff = w1.shape[-1]
    assert d_ff % tf == 0 and rows <= xs.shape[0]
    nj = d_ff // tf

    def row_index(i, j, te, nv):
        return (jnp.maximum(jnp.minimum(i, nv[0] - 1), 0), 0)

    def up_index(i, j, te, nv):
        return (idx, te[i], 0, jnp.where(i < nv[0], j, nj - 1))

    def down_index(i, j, te, nv):
        return (idx, te[i], jnp.where(i < nv[0], j, nj - 1), 0)

    return pl.pallas_call(
        _moe_ffn_kernel,
        grid_spec=pltpu.PrefetchScalarGridSpec(
            num_scalar_prefetch=2,
            grid=(rows // tm, nj),
            in_specs=[
                pl.BlockSpec((tm, D_MODEL), row_index),
                pl.BlockSpec((None, None, D_MODEL, tf), up_index),
                pl.BlockSpec((None, None, D_MODEL, tf), up_index),
                pl.BlockSpec((None, None, tf, D_MODEL), down_index),
            ],
            out_specs=pl.BlockSpec((tm, D_MODEL), row_index),
            scratch_shapes=[pltpu.VMEM((tm, D_MODEL), BF16)],
        ),
        out_shape=jax.ShapeDtypeStruct((rows, D_MODEL), F32),
        compiler_params=_cparams(("arbitrary", "arbitrary"), vmem=LARGE_VMEM_LIMIT),
    )(tile_expert, n_valid, xs, w1, w3, w2)


def _combine_kernel(pos_ref, ys_ref, x_ref, gate_ref, g2_ref, o_ref, buf, sem, *, tt):
    i = pl.program_id(0)
    cur = i % 2

    def gather(tile, half):
        def issue(t, c):
            for slot in range(2):
                row = pos_ref[2 * (tile * tt + t) + slot]
                pltpu.make_async_copy(ys_ref.at[pl.ds(row, 1), :],
                                      buf.at[half, slot, pl.ds(t, 1), :], sem.at[half]).start()
            return c

        lax.fori_loop(0, tt, issue, 0, unroll=ROW_DMA_UNROLL)

    @pl.when(i == 0)
    def _():
        gather(0, 0)

    for slot in range(2):
        pltpu.make_async_copy(ys_ref.at[pl.ds(0, tt), :], buf.at[cur, slot], sem.at[cur]).wait()

    @pl.when(i + 1 < pl.num_programs(0))
    def _():
        gather(i + 1, 1 - cur)

    gates = gate_ref[...]
    f = gates[:, 0:1] * buf[cur, 0] + gates[:, 1:2] * buf[cur, 1]
    o_ref[...] = x_ref[...] + g2_ref[...] * f


def _combine_final_kernel(pos_ref, ys_ref, x_ref, gate_ref, g2_ref, gf_ref, shf_ref, scf_ref,
                          *refs, out_rows, tt):
    out_refs, (x_new, buf, sem) = refs[:len(out_rows)], refs[len(out_rows):]
    _combine_kernel(pos_ref, ys_ref, x_ref, gate_ref, g2_ref, x_new, buf, sem, tt=tt)
    y = _norm_mod(x_new[...], gf_ref[...], shf_ref[...], scf_ref[...])
    i, lo = pl.program_id(0), 0
    for o_ref, rows in zip(out_refs, out_rows):
        nt = rows // tt

        @pl.when((i >= lo) & (i < lo + nt))
        def _(o_ref=o_ref):
            o_ref[...] = y

        lo += nt


def _combine(ys, pos, x, gates, mod, row_batch, final=None, tt=256):
    n = x.shape[0]
    in_specs = [
        pl.BlockSpec(memory_space=pl.ANY),
        pl.BlockSpec((tt, D_MODEL), lambda i, pos: (i, 0)),
        pl.BlockSpec((tt, LANES), lambda i, pos: (i, 0)),
        _mod_spec(5, 6, row_batch, tt),
    ]
    args = [pos, ys, x, gates, mod]
    scratch = [pltpu.VMEM((2, 2, tt, D_MODEL), F32), pltpu.SemaphoreType.DMA((2,))]
    if final is None:
        body = functools.partial(_combine_kernel, tt=tt)
        out_specs = pl.BlockSpec((tt, D_MODEL), lambda i, pos: (i, 0))
        out_shape = jax.ShapeDtypeStruct((n, D_MODEL), F32)
    else:
        gain, fmod, out_rows = final
        body = functools.partial(_combine_final_kernel, out_rows=out_rows, tt=tt)
        in_specs += [pl.BlockSpec((1, D_MODEL), lambda i, pos: (0, 0)),
                     _mod_spec(0, 2, row_batch, tt), _mod_spec(1, 2, row_batch, tt)]
        args += [gain, fmod, fmod]
        out_specs, out_shape, lo = [], [], 0
        for rows in out_rows:
            nt = rows // tt

            def index(i, pos, lo=lo, nt=nt):
                return (jnp.clip(i - lo, 0, nt - 1), 0)

            out_specs.append(pl.BlockSpec((tt, D_MODEL), index))
            out_shape.append(jax.ShapeDtypeStruct((rows, D_MODEL), F32))
            lo += nt
        scratch = [pltpu.VMEM((tt, D_MODEL), F32)] + scratch
    return pl.pallas_call(
        body,
        grid_spec=pltpu.PrefetchScalarGridSpec(
            num_scalar_prefetch=1, grid=(n // tt,), in_specs=in_specs, out_specs=out_specs,
            scratch_shapes=scratch),
        out_shape=out_shape,
        compiler_params=_cparams(("arbitrary",)),
    )(*args)


def _moe_layer(x, gain, mod, router_w, w1, w3, w2, layer, idx, row_batch, final=None, tm=512):
    n = x.shape[0]
    h, top_idx, gates, rank, counts = _router(x, gain, mod, router_w[idx], layer, row_batch)
    counts = counts[0, :N_EXPERTS]
    padded = ((counts + tm - 1) // tm) * tm
    ends = jnp.cumsum(padded)
    starts = ends - padded
    experts = top_idx[:, :2]
    pos = (starts[experts] + rank[:, :2]).reshape(-1).astype(jnp.int32)
    n_tiles = (2 * n) // tm + N_EXPERTS
    tile_start = jnp.arange(n_tiles, dtype=jnp.int32) * tm
    tile_expert = jnp.minimum(jnp.sum(tile_start[:, None] >= ends[None, :], axis=1),
                              N_EXPERTS - 1).astype(jnp.int32)
    n_valid = (ends[-1:] // tm).astype(jnp.int32)
    pad_start = ((starts + counts) // SUBLANES * SUBLANES).astype(jnp.int32)
    xs = _dispatch(h, pos, pad_start, (n_tiles + 2) * tm, tm + SUBLANES)
    ys = _moe_ffn(xs, tile_expert, n_valid, w1, w3, w2, idx, tm)
    return _combine(ys, pos, x, gates, mod, row_batch, final)


def _final_kernel(x_ref, g_ref, sh_ref, sc_ref, o_ref):
    o_ref[...] = _norm_mod(x_ref[...], g_ref[...], sh_ref[...], sc_ref[...])


def _final(x, gain, fmod, row_lo, rows, row_batch, tm=512):
    off = row_lo // tm

    def mod_spec(which):
        return pl.BlockSpec((None, 1, D_MODEL),
                            lambda i: (row_batch((i + off) * tm) * 2 + which, 0, 0))

    return pl.pallas_call(
        _final_kernel,
        grid=(rows // tm,),
        in_specs=[
            pl.BlockSpec((tm, D_MODEL), lambda i: (i + off, 0)),
            pl.BlockSpec((1, D_MODEL), lambda i: (0, 0)),
            mod_spec(0),
            mod_spec(1),
        ],
        out_specs=pl.BlockSpec((tm, D_MODEL), lambda i: (i, 0)),
        out_shape=jax.ShapeDtypeStruct((rows, D_MODEL), F32),
        compiler_params=_cparams(("parallel",)),
    )(x, gain, fmod, fmod)


def kernel(x_prompt, x_sample, c_prompt, c_sample, w_mod, b_mod, norm_mix, w_in, w_fourier, w_out,
           norm_ffn, dense_w1, dense_w3, dense_w2, router_w, moe_w1, moe_w3, moe_w2,
           w_final_mod, b_final_mod, norm_final):
    batch, seq_p, d = x_prompt.shape
    batch_s, seq_s, _ = x_sample.shape
    n_p, n_s = batch * seq_p, batch_s * seq_s
    n = n_p + n_s
    depth = w_mod.shape[0]
    assert d == D_MODEL and batch_s == 1 and n_p % seq_s == 0 and batch + batch_s <= MOD_ROWS

    def row_batch(row):
        return jnp.minimum(row // seq_p, batch)

    x = (x_prompt.reshape(n_p, d), x_sample.reshape(n_s, d))
    c_all = jnp.zeros((MOD_ROWS, d), F32).at[:batch].set(c_prompt).at[batch:batch + 1].set(c_sample)

    mod = _modulation(c_all, w_mod, b_mod)
    fmod = _modulation(c_all, w_final_mod[None], b_final_mod[None])[0]
    fmod = fmod.reshape(MOD_ROWS * 2, 1, d)

    gain_mix = norm_mix.reshape(depth, 1, d)
    gain_ffn = norm_ffn.reshape(depth, 1, d)
    gain_final = norm_final.reshape(1, d)
    w_in_b = w_in.astype(BF16)
    w_fourier_b = w_fourier.astype(BF16)
    f32 = dict(w_out=w_out, dense_w1=dense_w1, dense_w3=dense_w3, dense_w2=dense_w2,
               moe_w1=moe_w1, moe_w3=moe_w3, moe_w2=moe_w2)
    bf = {}

    def jobs(names):
        names = [k for k in names if k not in bf and f32[k].size]
        return names, tuple(f32[k].reshape(-1, f32[k].shape[-1]) for k in names)

    def done(names, casts):
        for k, c in zip(names, casts):
            bf[k] = c.reshape(f32[k].shape)

    def get(*names):
        for k in names:
            if k not in bf:
                bf[k] = f32[k].astype(BF16)
        return [bf[k] for k in names]

    rope_p = _rope_tables(seq_p)
    rope_s = _rope_tables(seq_s)

    for l in range(depth):
        mod_l = mod[l].reshape(MOD_ROWS * 6, 1, d)
        z = _in_proj(x, gain_mix, mod_l, w_in_b, l, row_batch)
        names, side = jobs(["w_out", "dense_w1", "dense_w3", "dense_w2"]
                           + (["moe_w2"] if l % 2 else ["moe_w3"]))
        o_attn, cast = _attention(z, None, *rope_p, seq_p, 0, batch, side)
        done(names, cast)
        o_attn, _ = _attention(z, o_attn, *rope_s, seq_s, n_p // seq_s, batch_s)
        o_f = _fourier(z, None, w_fourier_b, l, seq_p, 0, batch)
        o_f = _fourier(z, o_f, w_fourier_b, l, seq_s, n_p // seq_s, batch_s)
        x = _out_proj(o_attn, o_f, x, mod_l, *get("w_out"), l, row_batch)
        if l % 2 == 0:
            names, side = jobs(["moe_w1", "moe_w3"])
            x, cast = _ffn_dense(x, gain_ffn, mod_l, *get("dense_w1", "dense_w3", "dense_w2"),
                                 l, l // 2, row_batch, side)
            done(names, cast)
            x = (x,)
        else:
            final = (gain_final, fmod, (n_p, n_s)) if l == depth - 1 else None
            x = _moe_layer(x, gain_ffn, mod_l, router_w, *get("moe_w1", "moe_w3", "moe_w2"),
                           l, l // 2, row_batch, final)
            x = tuple(x) if final else (x,)

    if depth % 2 == 0:
        y_p, y_s = x
    else:
        y_p = _final(x[0], gain_final, fmod, 0, n_p, row_batch)
        y_s = _final(x[0], gain_final, fmod, n_p, n_s, row_batch)
    return (y_p.reshape(batch, seq_p, d), y_s.reshape(batch_s, seq_s, d))
```

```python
import functools

import numpy as np
import jax
import jax.numpy as jnp
from jax import lax
from jax.experimental import pallas as pl
from jax.experimental.pallas import tpu as pltpu

F32 = jnp.float32
BF16 = jnp.bfloat16

D_MODEL = 2048
HEAD_DIM = 128
N_ATTN_HEADS = 12
ATTN_WIDTH = N_ATTN_HEADS * HEAD_DIM
N_FOURIER_GROUPS = 4
FOURIER_GROUP = 128
FOURIER_WIDTH = N_FOURIER_GROUPS * FOURIER_GROUP
IN_WIDTH = 3 * ATTN_WIDTH + FOURIER_WIDTH
DILATIONS = (1, 4, 16)
HALF_NEIGHBOURS = 64
ROPE_THETA = 10000.0
N_EXPERTS = 8
EPS = 1e-6
NEG_INF = -1e30
LANES = 128
SUBLANES = 8
MOD_ROWS = 16

VMEM_LIMIT = 48 * 1024 * 1024
ATTN_VMEM_LIMIT = 58 * 1024 * 1024
BLOCK_GROUP = 4
DFT_GROUP = 8
NORM_ROWS = 16
ROW_DMA_UNROLL = 8
DOWN_CHUNK = 512
LARGE_VMEM_LIMIT = 56 * 1024 * 1024


def _cparams(sem, vmem=VMEM_LIMIT, **kw):
    return pltpu.CompilerParams(dimension_semantics=sem, vmem_limit_bytes=vmem, **kw)


def _norm_mod(x, g, sh, sc):
    ms = jnp.mean(x * x, axis=-1, keepdims=True)
    return (x * lax.rsqrt(ms + EPS)) * g * (1.0 + sc) + sh


def _norm_mod_rows(x_ref, g_ref, sh_ref, sc_ref, out_ref):
    g, sh, sc = g_ref[...], sh_ref[...], sc_ref[...]

    def body(c, carry):
        rows = pl.ds(pl.multiple_of(c * NORM_ROWS, NORM_ROWS), NORM_ROWS)
        out_ref[rows, :] = _norm_mod(x_ref[rows, :], g, sh, sc).astype(out_ref.dtype)
        return carry

    lax.fori_loop(0, x_ref.shape[0] // NORM_ROWS, body, 0, unroll=8)


def _silu(a):
    return a / (1.0 + jnp.exp(-a))


def _mod_kernel(c_ref, w_ref, b_ref, o_ref):
    cs = _silu(c_ref[...]).astype(BF16)
    o_ref[...] = jnp.dot(cs, w_ref[...].astype(BF16), preferred_element_type=F32) + b_ref[...]


def _modulation(c_all, w, b, tn=1024):
    nl, d, n = w.shape
    return pl.pallas_call(
        _mod_kernel,
        grid=(nl, n // tn),
        in_specs=[
            pl.BlockSpec((MOD_ROWS, d), lambda l, j: (0, 0)),
            pl.BlockSpec((None, d, tn), lambda l, j: (l, 0, j)),
            pl.BlockSpec((None, 1, tn), lambda l, j: (l, 0, j)),
        ],
        out_specs=pl.BlockSpec((None, MOD_ROWS, tn), lambda l, j: (l, 0, j)),
        out_shape=jax.ShapeDtypeStruct((nl, MOD_ROWS, n), F32),
        compiler_params=_cparams(("parallel", "parallel")),
    )(c_all, w, b.reshape(nl, 1, n))


def _mod_spec(which, n_vec, row_batch, tm):
    def index(i, *_):
        return (row_batch(i * tm) * n_vec + which, 0, 0)
    return pl.BlockSpec((None, 1, D_MODEL), index)


def _token_specs(xs, tm, width, column_tiled=False):
    specs, lo = [], 0
    for x in xs:
        nt = x.shape[0] // tm

        def index(i, j=0, *_, lo=lo, nt=nt):
            return (jnp.clip(i - lo, 0, nt - 1), j if column_tiled else 0)

        specs.append(pl.BlockSpec((tm, width), index))
        lo += nt
    return specs


def _with_token_tile(x_refs, src_rows, tm, fn):
    if len(x_refs) == 1:
        fn(x_refs[0])
        return
    i, lo = pl.program_id(0), 0
    for x_ref, rows in zip(x_refs, src_rows):
        nt = rows // tm
        pl.when((i >= lo) & (i < lo + nt))(functools.partial(fn, x_ref))
        lo += nt


def _in_proj_kernel(*refs, src_rows, tm):
    ns = len(src_rows)
    g_ref, sh_ref, sc_ref, w_ref, o_ref, h_scr = refs[ns:]

    def prepare(x_ref):
        _norm_mod_rows(x_ref, g_ref, sh_ref, sc_ref, h_scr)

    @pl.when(pl.program_id(1) == 0)
    def _():
        _with_token_tile(refs[:ns], src_rows, tm, prepare)

    o_ref[...] = jnp.dot(h_scr[...], w_ref[...], preferred_element_type=F32).astype(o_ref.dtype)


def _in_proj(xs, gain, mod, w_in, layer, row_batch, tm=1024, tn=1280):
    n = sum(x.shape[0] for x in xs)
    z, lo = None, 0
    for x in xs:
        lo_t = lo // tm
        in_specs = [
            pl.BlockSpec((tm, D_MODEL), lambda i, j: (i, 0)),
            pl.BlockSpec((None, 1, D_MODEL), lambda i, j: (layer, 0, 0)),
            _mod_spec(0, 6, lambda row, lo=lo: row_batch(row + lo), tm),
            _mod_spec(1, 6, lambda row, lo=lo: row_batch(row + lo), tm),
            pl.BlockSpec((None, D_MODEL, tn), lambda i, j: (layer, 0, j)),
        ]
        args = [x, gain, mod, mod, w_in]
        if z is not None:
            in_specs.append(pl.BlockSpec(memory_space=pl.ANY))
            args.append(z)

        def kern(*refs, n_in=len(args)):
            _in_proj_kernel(*refs[:5], *refs[n_in:], src_rows=(tm,), tm=tm)

        z = pl.pallas_call(
            kern,
            grid=(x.shape[0] // tm, IN_WIDTH // tn),
            in_specs=in_specs,
            out_specs=pl.BlockSpec((tm, tn), lambda i, j, lo_t=lo_t: (i + lo_t, j)),
            out_shape=jax.ShapeDtypeStruct((n, IN_WIDTH), BF16),
            scratch_shapes=[pltpu.VMEM((tm, D_MODEL), BF16)],
            input_output_aliases={} if z is None else {5: 0},
            compiler_params=_cparams(("parallel", "arbitrary")),
        )(*args)
        lo += x.shape[0]
    return z


def _attn_kernel(q_ref, k_ref, v_ref, cos_ref, sin_ref, o_ref,
                 qr, kr, vr, acc, mx, den, bias, *, seq, bq):
    chunk = bq * DILATIONS[-1]

    @pl.when((pl.program_id(0) == 0) & (pl.program_id(1) == 0))
    def _():
        for p, r in enumerate(DILATIONS):
            kw = min(bq + 2 * HALF_NEIGHBOURS, seq // r)
            rel = (lax.broadcasted_iota(jnp.int32, (bq, kw), 1)
                   - lax.broadcasted_iota(jnp.int32, (bq, kw), 0))
            for case, off in enumerate((0, -HALF_NEIGHBOURS, bq - kw)):
                bias[p, case, :, :kw] = jnp.where(jnp.abs(rel + off) <= HALF_NEIGHBOURS,
                                                  0.0, NEG_INF)

    cos = cos_ref[...]
    sin = sin_ref[...]
    q = q_ref[...].astype(F32)
    qr[...] = (q * cos + pltpu.roll(q, HEAD_DIM // 2, 1) * sin) * (1.0 / np.sqrt(HEAD_DIM))
    k = k_ref[...].astype(F32)
    kr[...] = k * cos + pltpu.roll(k, HEAD_DIM // 2, 1) * sin
    vr[...] = v_ref[...].astype(F32)

    def do_chunk(c, carry):
        base = c * chunk
        for p, r in enumerate(DILATIONS):
            sub_len = seq // r
            kw = min(bq + 2 * HALF_NEIGHBOURS, sub_len)
            nb = chunk // (r * bq)
            group = BLOCK_GROUP * (bq + 2 * HALF_NEIGHBOURS) // kw

            def do_blocks(t, carry2, p=p, r=r, sub_len=sub_len, kw=kw, nb=nb, group=group):
                work = []
                for g in range(group):
                    tg = t * group + g
                    m = tg // nb
                    i = tg % nb
                    q0 = (c * nb + i) * bq
                    k0 = jnp.clip(q0 - HALF_NEIGHBOURS, 0, sub_len - kw)
                    qb = qr[pl.ds(m + r * q0, bq, stride=r), :].astype(BF16)
                    kb = kr[pl.ds(m + r * k0, kw, stride=r), :].astype(BF16)
                    vb = vr[pl.ds(m + r * k0, kw, stride=r), :].astype(BF16)
                    dst = pl.ds(m + r * (i * bq), bq, stride=r)
                    work.append((qb, kb, vb, k0 - q0, dst))
                scores = [lax.dot_general(qb, kb, (((1,), (1,)), ((), ())),
                                          preferred_element_type=F32)
                          for qb, kb, _, _, _ in work]
                weights = []
                for s, (_, _, _, off, dst) in zip(scores, work):
                    case = jnp.where(off == 0, 0, jnp.where(off == -HALF_NEIGHBOURS, 1, 2))
                    s = s + bias[p, case, :, :kw]
                    smax = jnp.max(s, axis=-1, keepdims=True)
                    e = jnp.exp(s - smax)
                    mx[p, dst, :] = jnp.broadcast_to(smax, (bq, HEAD_DIM))
                    den[p, dst, :] = jnp.broadcast_to(jnp.sum(e, axis=-1, keepdims=True),
                                                      (bq, HEAD_DIM))
                    weights.append(e.astype(BF16))
                for e, (_, _, vb, _, dst) in zip(weights, work):
                    acc[p, dst, :] = jnp.dot(e, vb, preferred_element_type=F32)
                return carry2

            lax.fori_loop(0, (r * nb) // group, do_blocks, 0)

        top = jnp.maximum(jnp.maximum(mx[0], mx[1]), mx[2])
        num = jnp.zeros((chunk, HEAD_DIM), F32)
        tot = jnp.zeros((chunk, HEAD_DIM), F32)
        for p in range(len(DILATIONS)):
            w = jnp.exp(mx[p] - top)
            num = num + w * acc[p]
            tot = tot + w * den[p]
        o_ref[pl.ds(pl.multiple_of(base, chunk), chunk), :] = (num / tot).astype(o_ref.dtype)
        return carry

    lax.fori_loop(0, seq // chunk, do_chunk, 0)


def _attention(z, o_prev, rope_cos, rope_sin, seq, batch_lo, batch_n, side=(), bq=128):
    n = z.shape[0]
    chunk = bq * DILATIONS[-1]
    assert seq % chunk == 0 and n % seq == 0
    zb = z.reshape(n // seq, seq, IN_WIDTH)

    def col_spec(off):
        return pl.BlockSpec((None, seq, HEAD_DIM),
                            lambda b, h: (batch_lo + b, 0, off + h))

    in_specs = [
        col_spec(0), col_spec(N_ATTN_HEADS), col_spec(2 * N_ATTN_HEADS),
        pl.BlockSpec((seq, HEAD_DIM), lambda b, h: (0, 0), pipeline_mode=pl.Buffered(1)),
        pl.BlockSpec((seq, HEAD_DIM), lambda b, h: (0, 0), pipeline_mode=pl.Buffered(1)),
    ]
    args = [zb, zb, zb, rope_cos, rope_sin]
    aliases = {}
    if o_prev is not None:
        in_specs.append(pl.BlockSpec(memory_space=pl.ANY))
        args.append(o_prev.reshape(n // seq, seq, ATTN_WIDTH))
        aliases = {5: 0}
    n_main = len(args)
    side_specs, side_shapes = _side_cast_specs(side, batch_n * N_ATTN_HEADS,
                                               lambda b, h: b * N_ATTN_HEADS + h)

    def kern(*refs):
        ins, rest = refs[:n_main + len(side)], refs[n_main + len(side):]
        o_ref, side_dst, scratch = rest[0], rest[1:1 + len(side)], rest[1 + len(side):]
        _attn_kernel(*ins[:5], o_ref, *scratch, seq=seq, bq=bq)
        for src, dst in zip(ins[n_main:], side_dst):
            dst[...] = src[...].astype(dst.dtype)

    out = pl.pallas_call(
        kern,
        grid=(batch_n, N_ATTN_HEADS),
        in_specs=in_specs + side_specs,
        out_specs=[pl.BlockSpec((None, seq, HEAD_DIM), lambda b, h: (batch_lo + b, 0, h))]
        + side_specs,
        out_shape=[jax.ShapeDtypeStruct((n // seq, seq, ATTN_WIDTH), BF16)] + side_shapes,
        scratch_shapes=[
            pltpu.VMEM((seq, HEAD_DIM), F32),
            pltpu.VMEM((seq, HEAD_DIM), F32),
            pltpu.VMEM((seq, HEAD_DIM), F32),
            pltpu.VMEM((len(DILATIONS), chunk, HEAD_DIM), F32),
            pltpu.VMEM((len(DILATIONS), chunk, HEAD_DIM), F32),
            pltpu.VMEM((len(DILATIONS), chunk, HEAD_DIM), F32),
            pltpu.VMEM((len(DILATIONS), 3, bq, bq + 2 * HALF_NEIGHBOURS), F32),
        ],
        input_output_aliases=aliases,
        compiler_params=_cparams(("arbitrary", "arbitrary"), vmem=ATTN_VMEM_LIMIT),
    )(*args, *side)
    return out[0].reshape(n, ATTN_WIDTH), out[1:]


def _rope_tables(seq):
    half = HEAD_DIM // 2
    inv = jnp.power(ROPE_THETA, -jnp.arange(half, dtype=F32) * 2.0 / HEAD_DIM)
    ang = jnp.arange(seq, dtype=F32)[:, None] * inv[None, :]
    cos, sin = jnp.cos(ang), jnp.sin(ang)
    return jnp.concatenate([cos, cos], axis=-1), jnp.concatenate([-sin, sin], axis=-1)


def _dft_factors(seq):
    s1 = {2048: 32, 8192: 64}.get(seq)
    if s1 is None:
        s1 = 1
        while s1 * s1 < seq:
            s1 *= 2
        s1 = seq // s1 if (seq // s1) * s1 == seq else s1
    return s1, seq // s1


@functools.lru_cache(maxsize=None)
def _dft_tables(seq):
    s1, s2 = _dft_factors(seq)
    c = np.arange(FOURIER_GROUP)
    ang_c = 2.0 * np.pi * ((c[:, None] * c[None, :]) % FOURIER_GROUP) / FOURIER_GROUP
    norm = 1.0 / np.sqrt(float(seq) * FOURIER_GROUP)
    w_chan = np.concatenate([np.cos(ang_c), -np.sin(ang_c)], axis=1) * norm
    k1 = np.arange(s1)[None, :, None]
    n1 = np.arange(s1)[None, None, :]
    n2 = np.arange(s2)[:, None, None]
    ang1 = 2.0 * np.pi * ((k1 * (n2 + s2 * n1)) % seq) / seq
    stage1 = np.concatenate([np.cos(ang1), np.sin(ang1)], axis=1)
    j = np.arange(s2)
    ang2 = 2.0 * np.pi * ((j[:, None] * j[None, :]) % s2) / s2
    return (np.asarray(w_chan, np.float32), np.asarray(stage1, np.float32),
            np.asarray(np.cos(ang2), np.float32), np.asarray(np.sin(ang2), np.float32))


def _fourier_kernel(f_ref, wc_ref, st1_ref, c2_ref, s2_ref, wf_ref, o_ref,
                    zr_scr, zi_scr, tr_scr, ti_scr, y_scr, *, s1, s2):
    g = FOURIER_GROUP
    c2 = c2_ref[...].astype(BF16)
    sn2 = s2_ref[...].astype(BF16)
    z = jnp.dot(f_ref[...], wc_ref[...].astype(BF16), preferred_element_type=F32)
    zr_scr[...] = z[:, :g]
    zi_scr[...] = z[:, g:]

    def stage1(t, carry):
        n2s = [t * DFT_GROUP + u for u in range(DFT_GROUP)]
        zs = []
        for n2 in n2s:
            rows = pl.ds(n2, s1, stride=s2)
            zs.append(jnp.concatenate([zr_scr[rows, :], zi_scr[rows, :]], axis=1).astype(BF16))
        prs = [jnp.dot(st1_ref[n2].astype(BF16), z, preferred_element_type=F32)
               for n2, z in zip(n2s, zs)]
        for n2, pr in zip(n2s, prs):
            dst = pl.ds(pl.multiple_of(n2 * s1, s1), s1)
            tr_scr[dst, :] = pr[:s1, :g] + pr[s1:, g:]
            ti_scr[dst, :] = pr[:s1, g:] - pr[s1:, :g]
        return carry

    lax.fori_loop(0, s2 // DFT_GROUP, stage1, 0)

    def stage2(t, carry):
        rows = [pl.ds(t * DFT_GROUP + u, s2, stride=s1) for u in range(DFT_GROUP)]
        ts = [(tr_scr[r, :].astype(BF16), ti_scr[r, :].astype(BF16)) for r in rows]
        ys = [jnp.dot(c2, tr, preferred_element_type=F32)
              + jnp.dot(sn2, ti, preferred_element_type=F32) for tr, ti in ts]
        for r, y in zip(rows, ys):
            y_scr[r, :] = y
        return carry

    lax.fori_loop(0, s1 // DFT_GROUP, stage2, 0)
    o_ref[...] = jnp.dot(y_scr[...].astype(BF16), wf_ref[...],
                         preferred_element_type=F32).astype(o_ref.dtype)


def _fourier(z, o_prev, w_fourier, layer, seq, batch_lo, batch_n):
    n = z.shape[0]
    s1, s2 = _dft_factors(seq)
    w_chan, stage1, c2, sn2 = (jnp.asarray(t, F32) for t in _dft_tables(seq))
    zb = z.reshape(n // seq, seq, IN_WIDTH)
    f_col = 3 * N_ATTN_HEADS
    in_specs = [
        pl.BlockSpec((None, seq, FOURIER_GROUP), lambda b, g: (batch_lo + b, 0, f_col + g)),
        pl.BlockSpec((FOURIER_GROUP, 2 * FOURIER_GROUP), lambda b, g: (0, 0)),
        pl.BlockSpec((s2, 2 * s1, s1), lambda b, g: (0, 0, 0), pipeline_mode=pl.Buffered(1)),
        pl.BlockSpec((s2, s2), lambda b, g: (0, 0)),
        pl.BlockSpec((s2, s2), lambda b, g: (0, 0)),
        pl.BlockSpec((None, None, FOURIER_GROUP, FOURIER_GROUP), lambda b, g: (layer, g, 0, 0)),
    ]
    args = [zb, w_chan, stage1, c2, sn2, w_fourier]
    aliases = {}
    if o_prev is not None:
        in_specs.append(pl.BlockSpec(memory_space=pl.ANY))
        args.append(o_prev.reshape(n // seq, seq, FOURIER_WIDTH))
        aliases = {6: 0}

    def kern(*refs):
        if o_prev is not None:
            refs = refs[:6] + refs[7:]
        _fourier_kernel(*refs, s1=s1, s2=s2)

    out = pl.pallas_call(
        kern,
        grid=(batch_n, N_FOURIER_GROUPS),
        in_specs=in_specs,
        out_specs=pl.BlockSpec((None, seq, FOURIER_GROUP), lambda b, g: (batch_lo + b, 0, g)),
        out_shape=jax.ShapeDtypeStruct((n // seq, seq, FOURIER_WIDTH), BF16),
        scratch_shapes=[pltpu.VMEM((seq, FOURIER_GROUP), F32)] * 5,
        input_output_aliases=aliases,
        compiler_params=_cparams(("parallel", "parallel")),
    )(*args)
    return out.reshape(n, FOURIER_WIDTH)


def _out_proj_kernel(oa_ref, of_ref, wa_ref, wf_ref, g_ref, *refs, src_rows, tm):
    ns = len(src_rows)
    o_ref = refs[ns]
    y = jnp.dot(oa_ref[...], wa_ref[...], preferred_element_type=F32)
    y = y + jnp.dot(of_ref[...], wf_ref[...], preferred_element_type=F32)
    gated = g_ref[...] * y

    def residual(x_ref):
        o_ref[...] = x_ref[...] + gated

    _with_token_tile(refs[:ns], src_rows, tm, residual)


def _out_proj(o_attn, o_f, xs, mod, w_out, layer, row_batch, tm=512, tn=D_MODEL):
    n = sum(x.shape[0] for x in xs)
    src_rows = tuple(x.shape[0] for x in xs)
    fblk = ATTN_WIDTH // FOURIER_WIDTH

    def gate_index(i, j):
        return (row_batch(i * tm) * 6 + 2, 0, j)

    return pl.pallas_call(
        functools.partial(_out_proj_kernel, src_rows=src_rows, tm=tm),
        grid=(n // tm, D_MODEL // tn),
        in_specs=[
            pl.BlockSpec((tm, ATTN_WIDTH), lambda i, j: (i, 0)),
            pl.BlockSpec((tm, FOURIER_WIDTH), lambda i, j: (i, 0)),
            pl.BlockSpec((None, ATTN_WIDTH, tn), lambda i, j: (layer, 0, j),
                         pipeline_mode=pl.Buffered(1)),
            pl.BlockSpec((None, FOURIER_WIDTH, tn), lambda i, j: (layer, fblk, j),
                         pipeline_mode=pl.Buffered(1)),
            pl.BlockSpec((None, 1, tn), gate_index),
        ] + _token_specs(xs, tm, tn, column_tiled=True),
        out_specs=pl.BlockSpec((tm, tn), lambda i, j: (i, j)),
        out_shape=jax.ShapeDtypeStruct((n, D_MODEL), F32),
        compiler_params=_cparams(("parallel", "parallel")),
    )(o_attn, o_f, w_out, w_out, mod, *xs)


def _swiglu_accumulate(h_scr, w1_ref, w3_ref, w2_ref, o_ref):
    h = h_scr[...]
    a = jnp.dot(h, w1_ref[...], preferred_element_type=F32)
    b = jnp.dot(h, w3_ref[...], preferred_element_type=F32)
    g = (_silu(a) * b).astype(BF16)
    for c in range(0, D_MODEL, DOWN_CHUNK):
        o_ref[:, c:c + DOWN_CHUNK] += jnp.dot(g, w2_ref[:, c:c + DOWN_CHUNK],
                                              preferred_element_type=F32)


def _ffn_dense_kernel(x_ref, g_ref, sh_ref, sc_ref, gate_ref, w1_ref, w3_ref, w2_ref, *refs,
                      n_side):
    side_src, o_ref, side_dst, h_scr = (refs[:n_side], refs[n_side],
                                        refs[n_side + 1:2 * n_side + 1], refs[2 * n_side + 1])
    j = pl.program_id(1)

    @pl.when(j == 0)
    def _():
        _norm_mod_rows(x_ref, g_ref, sh_ref, sc_ref, h_scr)
        o_ref[...] = jnp.zeros_like(o_ref)

    _swiglu_accumulate(h_scr, w1_ref, w3_ref, w2_ref, o_ref)
    for src, dst in zip(side_src, side_dst):
        dst[...] = src[...].astype(dst.dtype)

    @pl.when(j == pl.num_programs(1) - 1)
    def _():
        o_ref[...] = x_ref[...] + gate_ref[...] * o_ref[...]


def _side_block_rows(rows, n_steps):
    for b in range(2 * SUBLANES, rows + 1, 2 * SUBLANES):
        if rows % b == 0 and rows // b <= n_steps:
            return b
    raise ValueError("no row block")


def _side_cast_specs(side, n_steps, step):
    specs, shapes = [], []
    for a in side:
        rows, cols = a.shape
        blk = _side_block_rows(rows, n_steps)

        def index(i, j, last=rows // blk - 1):
            return (jnp.minimum(step(i, j), last), 0)

        specs.append(pl.BlockSpec((blk, cols), index))
        shapes.append(jax.ShapeDtypeStruct((rows, cols), BF16))
    return specs, shapes


def _ffn_dense(x, gain, mod, w1, w3, w2, layer, idx, row_batch, side=(), tm=512, tf=512):
    n = x.shape[0]
    d_ff = w1.shape[-1]
    nj = d_ff // tf
    side_specs, side_shapes = _side_cast_specs(side, (n // tm) * nj, lambda i, j: i * nj + j)
    out = pl.pallas_call(
        functools.partial(_ffn_dense_kernel, n_side=len(side)),
        grid=(n // tm, nj),
        in_specs=[
            pl.BlockSpec((tm, D_MODEL), lambda i, j: (i, 0)),
            pl.BlockSpec((None, 1, D_MODEL), lambda i, j: (layer, 0, 0)),
            _mod_spec(3, 6, row_batch, tm),
            _mod_spec(4, 6, row_batch, tm),
            _mod_spec(5, 6, row_batch, tm),
            pl.BlockSpec((None, D_MODEL, tf), lambda i, j: (idx, 0, j)),
            pl.BlockSpec((None, D_MODEL, tf), lambda i, j: (idx, 0, j)),
            pl.BlockSpec((None, tf, D_MODEL), lambda i, j: (idx, j, 0)),
        ] + side_specs,
        out_specs=[pl.BlockSpec((tm, D_MODEL), lambda i, j: (i, 0))] + side_specs,
        out_shape=[jax.ShapeDtypeStruct((n, D_MODEL), F32)] + side_shapes,
        scratch_shapes=[pltpu.VMEM((tm, D_MODEL), BF16)],
        compiler_params=_cparams(("arbitrary", "arbitrary")),
    )(x, gain, mod, mod, mod, w1, w3, w2, *side)
    return out[0], out[1:]


def _router_kernel(x_ref, g_ref, sh_ref, sc_ref, rw_hi_ref, rw_lo_ref, h_ref, idx_ref, gate_ref,
                   rank_ref, cnt_ref, cnt_scr, *, tm):
    @pl.when(pl.program_id(0) == 0)
    def _():
        cnt_scr[...] = jnp.zeros_like(cnt_scr)

    _norm_mod_rows(x_ref, g_ref, sh_ref, sc_ref, h_ref)
    h = h_ref[...]
    h_hi = h.astype(BF16)
    h_lo = (h - h_hi.astype(F32)).astype(BF16)
    logits = (jnp.dot(h_hi, rw_hi_ref[...], preferred_element_type=F32)
              + jnp.dot(h_lo, rw_hi_ref[...], preferred_element_type=F32)
              + jnp.dot(h_hi, rw_lo_ref[...], preferred_element_type=F32))
    lane = lax.broadcasted_iota(jnp.int32, (tm, LANES), 1).astype(F32)
    logits = jnp.where(lane < N_EXPERTS, logits, -jnp.inf)
    v1 = jnp.max(logits, axis=-1, keepdims=True)
    i1 = jnp.min(jnp.where(logits == v1, lane, float(LANES)), axis=-1, keepdims=True)
    rest = jnp.where(lane == i1, -jnp.inf, logits)
    v2 = jnp.max(rest, axis=-1, keepdims=True)
    i2 = jnp.min(jnp.where(rest == v2, lane, float(LANES)), axis=-1, keepdims=True)
    e2 = jnp.exp(v2 - v1)
    gate1 = 1.0 / (1.0 + e2)
    gate2 = e2 / (1.0 + e2)

    hot1 = (lane == i1).astype(BF16)
    hot2 = (lane == i2).astype(BF16)
    r_i = lax.broadcasted_iota(jnp.int32, (tm, tm), 0)
    c_i = lax.broadcasted_iota(jnp.int32, (tm, tm), 1)
    before = (c_i < r_i).astype(BF16)
    pre1 = jnp.dot(before, hot1, preferred_element_type=F32)
    pre2 = jnp.dot(before, hot2, preferred_element_type=F32)
    tot1 = jnp.sum(hot1.astype(F32), axis=0, keepdims=True)
    tot2 = jnp.sum(hot2.astype(F32), axis=0, keepdims=True)
    cnt = cnt_scr[...]
    rank1 = jnp.sum(jnp.where(lane == i1, pre1 + cnt, 0.0), axis=-1, keepdims=True)
    rank2 = jnp.sum(jnp.where(lane == i2, pre2 + cnt + tot1, 0.0), axis=-1, keepdims=True)
    cnt = cnt + tot1 + tot2
    cnt_scr[...] = cnt
    cnt_ref[...] = jnp.broadcast_to(cnt, cnt_ref.shape).astype(jnp.int32)

    idx_ref[...] = jnp.where(lane == 0, i1, jnp.where(lane == 1, i2, 0.0)).astype(jnp.int32)
    gate_ref[...] = jnp.where(lane == 0, gate1, jnp.where(lane == 1, gate2, 0.0))
    rank_ref[...] = jnp.where(lane == 0, rank1, jnp.where(lane == 1, rank2, 0.0)).astype(jnp.int32)


def _router(x, gain, mod, router_w, layer, row_batch, tm=512):
    n = x.shape[0]
    rw = jnp.zeros((D_MODEL, LANES), F32).at[:, :N_EXPERTS].set(router_w)
    rw_hi = rw.astype(BF16)
    rw_lo = (rw - rw_hi.astype(F32)).astype(BF16)
    tile = lambda i: (i, 0)
    return pl.pallas_call(
        functools.partial(_router_kernel, tm=tm),
        grid=(n // tm,),
        in_specs=[
            pl.BlockSpec((tm, D_MODEL), tile),
            pl.BlockSpec((None, 1, D_MODEL), lambda i: (layer, 0, 0)),
            _mod_spec(3, 6, row_batch, tm),
            _mod_spec(4, 6, row_batch, tm),
            pl.BlockSpec((D_MODEL, LANES), lambda i: (0, 0)),
            pl.BlockSpec((D_MODEL, LANES), lambda i: (0, 0)),
        ],
        out_specs=[
            pl.BlockSpec((tm, D_MODEL), tile),
            pl.BlockSpec((tm, LANES), tile),
            pl.BlockSpec((tm, LANES), tile),
            pl.BlockSpec((tm, LANES), tile),
            pl.BlockSpec((8, LANES), lambda i: (0, 0)),
        ],
        out_shape=[
            jax.ShapeDtypeStruct((n, D_MODEL), F32),
            jax.ShapeDtypeStruct((n, LANES), jnp.int32),
            jax.ShapeDtypeStruct((n, LANES), F32),
            jax.ShapeDtypeStruct((n, LANES), jnp.int32),
            jax.ShapeDtypeStruct((8, LANES), jnp.int32),
        ],
        scratch_shapes=[pltpu.VMEM((1, LANES), F32)],
        compiler_params=_cparams(("arbitrary",)),
    )(x, gain, mod, mod, rw_hi, rw_lo)


def _dispatch_kernel(pos_ref, pad_ref, h_ref, xs_ref, zeros, sem, zero_sem, *, tt, pad_rows):
    base = pl.program_id(0) * tt

    @pl.when(pl.program_id(0) == 0)
    def _():
        zeros[...] = jnp.zeros_like(zeros)

        def fill(e):
            start = pl.multiple_of(pad_ref[e], SUBLANES)
            return pltpu.make_async_copy(zeros, xs_ref.at[pl.ds(start, pad_rows), :], zero_sem)

        for e in range(N_EXPERTS):
            fill(e).start()
        for e in range(N_EXPERTS):
            fill(e).wait()

    def issue(t, c):
        for slot in range(2):
            pltpu.make_async_copy(h_ref.at[pl.ds(t, 1), :],
                                  xs_ref.at[pl.ds(pos_ref[2 * (base + t) + slot], 1), :],
                                  sem).start()
        return c

    lax.fori_loop(0, tt, issue, 0, unroll=ROW_DMA_UNROLL)
    for slot in range(2):
        pltpu.make_async_copy(h_ref, xs_ref.at[pl.ds(0, tt), :], sem).wait()


def _dispatch(h, pos, pad_start, rows, pad_rows, tt=256):
    n = h.shape[0]
    return pl.pallas_call(
        functools.partial(_dispatch_kernel, tt=tt, pad_rows=pad_rows),
        grid_spec=pltpu.PrefetchScalarGridSpec(
            num_scalar_prefetch=2,
            grid=(n // tt,),
            in_specs=[pl.BlockSpec((tt, D_MODEL), lambda i, pos, pad: (i, 0))],
            out_specs=pl.BlockSpec(memory_space=pl.ANY),
            scratch_shapes=[pltpu.VMEM((pad_rows, D_MODEL), F32),
                            pltpu.SemaphoreType.DMA(()), pltpu.SemaphoreType.DMA(())],
        ),
        out_shape=jax.ShapeDtypeStruct((rows, D_MODEL), F32),
        compiler_params=_cparams(("arbitrary",), has_side_effects=True),
    )(pos, pad_start, h)


def _moe_ffn_kernel(te_ref, nv_ref, x_ref, w1_ref, w3_ref, w2_ref, o_ref, h_scr):
    i = pl.program_id(0)
    j = pl.program_id(1)

    @pl.when(i < nv_ref[0])
    def _():
        @pl.when(j == 0)
        def _():
            h_scr[...] = x_ref[...].astype(BF16)
            o_ref[...] = jnp.zeros_like(o_ref)

        _swiglu_accumulate(h_scr, w1_ref, w3_ref, w2_ref, o_ref)


def _moe_ffn(xs, tile_expert, n_valid, w1, w3, w2, idx, tm, tf=1024):
    rows = tile_expert.shape[0] * tm
    d_ff = w1.shape[-1]
    assert d_ff % tf == 0 and rows <= xs.shape[0]
    nj = d_ff // tf

    def row_index(i, j, te, nv):
        return (jnp.maximum(jnp.minimum(i, nv[0] - 1), 0), 0)

    def up_index(i, j, te, nv):
        return (idx, te[i], 0, jnp.where(i < nv[0], j, nj - 1))

    def down_index(i, j, te, nv):
        return (idx, te[i], jnp.where(i < nv[0], j, nj - 1), 0)

    return pl.pallas_call(
        _moe_ffn_kernel,
        grid_spec=pltpu.PrefetchScalarGridSpec(
            num_scalar_prefetch=2,
            grid=(rows // tm, nj),
            in_specs=[
                pl.BlockSpec((tm, D_MODEL), row_index),
                pl.BlockSpec((None, None, D_MODEL, tf), up_index),
                pl.BlockSpec((None, None, D_MODEL, tf), up_index),
                pl.BlockSpec((None, None, tf, D_MODEL), down_index),
            ],
            out_specs=pl.BlockSpec((tm, D_MODEL), row_index),
            scratch_shapes=[pltpu.VMEM((tm, D_MODEL), BF16)],
        ),
        out_shape=jax.ShapeDtypeStruct((rows, D_MODEL), F32),
        compiler_params=_cparams(("arbitrary", "arbitrary"), vmem=LARGE_VMEM_LIMIT),
    )(tile_expert, n_valid, xs, w1, w3, w2)


def _combine_kernel(pos_ref, ys_ref, x_ref, gate_ref, g2_ref, o_ref, buf, sem, *, tt):
    i = pl.program_id(0)
    cur = i % 2

    def gather(tile, half):
        def issue(t, c):
            for slot in range(2):
                row = pos_ref[2 * (tile * tt + t) + slot]
                pltpu.make_async_copy(ys_ref.at[pl.ds(row, 1), :],
                                      buf.at[half, slot, pl.ds(t, 1), :], sem.at[half]).start()
            return c

        lax.fori_loop(0, tt, issue, 0, unroll=ROW_DMA_UNROLL)

    @pl.when(i == 0)
    def _():
        gather(0, 0)

    for slot in range(2):
        pltpu.make_async_copy(ys_ref.at[pl.ds(0, tt), :], buf.at[cur, slot], sem.at[cur]).wait()

    @pl.when(i + 1 < pl.num_programs(0))
    def _():
        gather(i + 1, 1 - cur)

    gates = gate_ref[...]
    f = gates[:, 0:1] * buf[cur, 0] + gates[:, 1:2] * buf[cur, 1]
    o_ref[...] = x_ref[...] + g2_ref[...] * f


def _combine_final_kernel(pos_ref, ys_ref, x_ref, gate_ref, g2_ref, gf_ref, shf_ref, scf_ref,
                          *refs, out_rows, tt):
    out_refs, (x_new, buf, sem) = refs[:len(out_rows)], refs[len(out_rows):]
    _combine_kernel(pos_ref, ys_ref, x_ref, gate_ref, g2_ref, x_new, buf, sem, tt=tt)
    i, lo = pl.program_id(0), 0
    for o_ref, rows in zip(out_refs, out_rows):
        nt = rows // tt

        @pl.when((i >= lo) & (i < lo + nt))
        def _(o_ref=o_ref):
            _norm_mod_rows(x_new, gf_ref, shf_ref, scf_ref, o_ref)

        lo += nt


def _combine(ys, pos, x, gates, mod, row_batch, final=None, tt=256):
    n = x.shape[0]
    in_specs = [
        pl.BlockSpec(memory_space=pl.ANY),
        pl.BlockSpec((tt, D_MODEL), lambda i, pos: (i, 0)),
        pl.BlockSpec((tt, LANES), lambda i, pos: (i, 0)),
        _mod_spec(5, 6, row_batch, tt),
    ]
    args = [pos, ys, x, gates, mod]
    scratch = [pltpu.VMEM((2, 2, tt, D_MODEL), F32), pltpu.SemaphoreType.DMA((2,))]
    if final is None:
        body = functools.partial(_combine_kernel, tt=tt)
        out_specs = pl.BlockSpec((tt, D_MODEL), lambda i, pos: (i, 0))
        out_shape = jax.ShapeDtypeStruct((n, D_MODEL), F32)
    else:
        gain, fmod, out_rows = final
        body = functools.partial(_combine_final_kernel, out_rows=out_rows, tt=tt)
        in_specs += [pl.BlockSpec((1, D_MODEL), lambda i, pos: (0, 0)),
                     _mod_spec(0, 2, row_batch, tt), _mod_spec(1, 2, row_batch, tt)]
        args += [gain, fmod, fmod]
        out_specs, out_shape, lo = [], [], 0
        for rows in out_rows:
            nt = rows // tt

            def index(i, pos, lo=lo, nt=nt):
                return (jnp.clip(i - lo, 0, nt - 1), 0)

            out_specs.append(pl.BlockSpec((tt, D_MODEL), index))
            out_shape.append(jax.ShapeDtypeStruct((rows, D_MODEL), F32))
            lo += nt
        scratch = [pltpu.VMEM((tt, D_MODEL), F32)] + scratch
    return pl.pallas_call(
        body,
        grid_spec=pltpu.PrefetchScalarGridSpec(
            num_scalar_prefetch=1, grid=(n // tt,), in_specs=in_specs, out_specs=out_specs,
            scratch_shapes=scratch),
        out_shape=out_shape,
        compiler_params=_cparams(("arbitrary",)),
    )(*args)


def _moe_layer(x, gain, mod, router_w, w1, w3, w2, layer, idx, row_batch, final=None, tm=512):
    n = x.shape[0]
    h, top_idx, gates, rank, counts = _router(x, gain, mod, router_w[idx], layer, row_batch)
    counts = counts[0, :N_EXPERTS]
    padded = ((counts + tm - 1) // tm) * tm
    ends = jnp.cumsum(padded)
    starts = ends - padded
    experts = top_idx[:, :2]
    pos = (starts[experts] + rank[:, :2]).reshape(-1).astype(jnp.int32)
    n_tiles = (2 * n) // tm + N_EXPERTS
    tile_start = jnp.arange(n_tiles, dtype=jnp.int32) * tm
    tile_expert = jnp.minimum(jnp.sum(tile_start[:, None] >= ends[None, :], axis=1),
                              N_EXPERTS - 1).astype(jnp.int32)
    n_valid = (ends[-1:] // tm).astype(jnp.int32)
    pad_start = ((starts + counts) // SUBLANES * SUBLANES).astype(jnp.int32)
    xs = _dispatch(h, pos, pad_start, (n_tiles + 2) * tm, tm + SUBLANES)
    ys = _moe_ffn(xs, tile_expert, n_valid, w1, w3, w2, idx, tm)
    return _combine(ys, pos, x, gates, mod, row_batch, final)


def _final_kernel(x_ref, g_ref, sh_ref, sc_ref, o_ref):
    o_ref[...] = _norm_mod(x_ref[...], g_ref[...], sh_ref[...], sc_ref[...])


def _final(x, gain, fmod, row_lo, rows, row_batch, tm=512):
    off = row_lo // tm

    def mod_spec(which):
        return pl.BlockSpec((None, 1, D_MODEL),
                            lambda i: (row_batch((i + off) * tm) * 2 + which, 0, 0))

    return pl.pallas_call(
        _final_kernel,
        grid=(rows // tm,),
        in_specs=[
            pl.BlockSpec((tm, D_MODEL), lambda i: (i + off, 0)),
            pl.BlockSpec((1, D_MODEL), lambda i: (0, 0)),
            mod_spec(0),
            mod_spec(1),
        ],
        out_specs=pl.BlockSpec((tm, D_MODEL), lambda i: (i, 0)),
        out_shape=jax.ShapeDtypeStruct((rows, D_MODEL), F32),
        compiler_params=_cparams(("parallel",)),
    )(x, gain, fmod, fmod)


def kernel(x_prompt, x_sample, c_prompt, c_sample, w_mod, b_mod, norm_mix, w_in, w_fourier, w_out,
           norm_ffn, dense_w1, dense_w3, dense_w2, router_w, moe_w1, moe_w3, moe_w2,
           w_final_mod, b_final_mod, norm_final):
    batch, seq_p, d = x_prompt.shape
    batch_s, seq_s, _ = x_sample.shape
    n_p, n_s = batch * seq_p, batch_s * seq_s
    n = n_p + n_s
    depth = w_mod.shape[0]
    assert d == D_MODEL and batch_s == 1 and n_p % seq_s == 0 and batch + batch_s <= MOD_ROWS

    def row_batch(row):
        return jnp.minimum(row // seq_p, batch)

    x = (x_prompt.reshape(n_p, d), x_sample.reshape(n_s, d))
    c_all = jnp.zeros((MOD_ROWS, d), F32).at[:batch].set(c_prompt).at[batch:batch + 1].set(c_sample)

    mod = _modulation(c_all, w_mod, b_mod)
    fmod = _modulation(c_all, w_final_mod[None], b_final_mod[None])[0]
    fmod = fmod.reshape(MOD_ROWS * 2, 1, d)

    gain_mix = norm_mix.reshape(depth, 1, d)
    gain_ffn = norm_ffn.reshape(depth, 1, d)
    gain_final = norm_final.reshape(1, d)
    w_in_b = w_in.astype(BF16)
    w_fourier_b = w_fourier.astype(BF16)
    f32 = dict(w_out=w_out, dense_w1=dense_w1, dense_w3=dense_w3, dense_w2=dense_w2,
               moe_w1=moe_w1, moe_w3=moe_w3, moe_w2=moe_w2)
    bf = {}

    def jobs(names):
        names = [k for k in names if k not in bf and f32[k].size]
        return names, tuple(f32[k].reshape(-1, f32[k].shape[-1]) for k in names)

    def done(names, casts):
        for k, c in zip(names, casts):
            bf[k] = c.reshape(f32[k].shape)

    def get(*names):
        for k in names:
            if k not in bf:
                bf[k] = f32[k].astype(BF16)
        return [bf[k] for k in names]

    rope_p = _rope_tables(seq_p)
    rope_s = _rope_tables(seq_s)

    for l in range(depth):
        mod_l = mod[l].reshape(MOD_ROWS * 6, 1, d)
        z = _in_proj(x, gain_mix, mod_l, w_in_b, l, row_batch)
        names, side = jobs(["w_out", "dense_w1", "dense_w3", "dense_w2"]
                           + (["moe_w2"] if l % 2 else []))
        o_attn, cast = _attention(z, None, *rope_p, seq_p, 0, batch, side)
        done(names, cast)
        o_attn, _ = _attention(z, o_attn, *rope_s, seq_s, n_p // seq_s, batch_s)
        o_f = _fourier(z, None, w_fourier_b, l, seq_p, 0, batch)
        o_f = _fourier(z, o_f, w_fourier_b, l, seq_s, n_p // seq_s, batch_s)
        x = _out_proj(o_attn, o_f, x, mod_l, *get("w_out"), l, row_batch)
        if l % 2 == 0:
            names, side = jobs(["moe_w1", "moe_w3"])
            x, cast = _ffn_dense(x, gain_ffn, mod_l, *get("dense_w1", "dense_w3", "dense_w2"),
                                 l, l // 2, row_batch, side)
            done(names, cast)
            x = (x,)
        else:
            final = (gain_final, fmod, (n_p, n_s)) if l == depth - 1 else None
            x = _moe_layer(x, gain_ffn, mod_l, router_w, *get("moe_w1", "moe_w3", "moe_w2"),
                           l, l // 2, row_batch, final)
            x = tuple(x) if final else (x,)

    if depth % 2 == 0:
        y_p, y_s = x
    else:
        y_p = _final(x[0], gain_final, fmod, 0, n_p, row_batch)
        y_s = _final(x[0], gain_final, fmod, n_p, n_s, row_batch)
    return (y_p.reshape(batch, seq_p, d), y_s.reshape(batch_s, seq_s, d))
```
